```python
import math
import jax, jax.numpy as jnp
from jax import lax
import numpy as np

D_MODEL = 2048
BATCH = 4
SEQ = 4096
DEPTH = 2

GRID_W = 64
CTX_LEN = 256
N_EVEN = (DEPTH + 1) // 2
N_ODD = DEPTH // 2
EPS = 1e-6

HEAD_DIM = 128
ATTN_HEADS = D_MODEL // (2 * HEAD_DIM)
ATTN_KV_HEADS = max(1, ATTN_HEADS // 4)
ATTN_GROUP = ATTN_HEADS // ATTN_KV_HEADS
ATTN_WIDTH = ATTN_HEADS * HEAD_DIM
KV_WIDTH = ATTN_KV_HEADS * HEAD_DIM
Q_BLOCK = 128
ROPE_THETA = 10000.0
AXIS_FREQS = HEAD_DIM // 4

S5_WIDTH = D_MODEL // 2
S5_GROUP = 16
S5_GROUPS = S5_WIDTH // S5_GROUP
S5_STATE = 64
AB_IN = ATTN_WIDTH + 2 * KV_WIDTH + S5_WIDTH
AB_OUT = ATTN_WIDTH + S5_WIDTH

HGRN_DK = 128
HGRN_HEADS = D_MODEL // HGRN_DK
HGRN_DV = D_MODEL // HGRN_HEADS
HGRN_KEY = HGRN_HEADS * HGRN_DK
HGRN_VAL = HGRN_HEADS * HGRN_DV
HGRN_CHUNK = 64
C_IN = 3 * HGRN_KEY + 2 * HGRN_VAL

D_FF = ((8 * D_MODEL) // 3 + 255) // 256 * 256

kernel_name = "hybrid_flow_backbone_attn_s5_hgrn2"

F32 = jnp.float32


def rmsnorm(x, g):
    xf = x.astype(F32)
    y = xf * lax.rsqrt(jnp.mean(xf * xf, axis=-1, keepdims=True) + EPS)
    return y * g.astype(F32)


def modulate(h, shift, scale):
    return h * (1.0 + scale) + shift


def axial_rope_tables(n_tokens):
    rows = n_tokens // GRID_W
    row = jnp.repeat(jnp.arange(rows, dtype=F32), GRID_W)
    col = jnp.tile(jnp.arange(GRID_W, dtype=F32), rows)
    inv = ROPE_THETA ** (-jnp.arange(AXIS_FREQS, dtype=F32) / AXIS_FREQS)
    ang = jnp.stack([row[:, None] * inv, col[:, None] * inv], axis=1)
    return jnp.cos(ang), jnp.sin(ang)


def apply_axial_rope(x, cos, sin):
    bsz, n, h, _ = x.shape
    xs = x.reshape(bsz, n, h, 2, 2, AXIS_FREQS)
    x1, x2 = xs[..., 0, :], xs[..., 1, :]
    c = cos[None, :, None]
    s = sin[None, :, None]
    out = jnp.stack([x1 * c - x2 * s, x2 * c + x1 * s], axis=-2)
    return out.reshape(bsz, n, h, HEAD_DIM)


def attend(q, k, v):
    s = jnp.einsum('bqhgd,bkhd->bhgqk', q, k, preferred_element_type=F32) * (HEAD_DIM ** -0.5)
    p = jax.nn.softmax(s, axis=-1)
    return jnp.einsum('bhgqk,bkhd->bqhgd', p, v.astype(F32))


def block_attention(q, k, v):
    bsz, n = q.shape[:2]
    qb = q.reshape(bsz, n // Q_BLOCK, Q_BLOCK, ATTN_KV_HEADS, ATTN_GROUP, HEAD_DIM).swapaxes(0, 1)
    ob = lax.map(lambda qi: attend(qi, k, v), qb)
    return ob.swapaxes(0, 1).reshape(bsz, n, ATTN_WIDTH)


def s5_discretise(lam_re, lam_im, log_dt, b_re, b_im):
    lam = lax.complex(jnp.minimum(lam_re.astype(F32), -1e-4), lam_im.astype(F32))
    dt = jnp.exp(log_dt.astype(F32))[:, None]
    lam_bar = jnp.exp(lam * dt)
    bmat = lax.complex(b_re.astype(F32), b_im.astype(F32))
    b_bar = ((lam_bar - 1.0) / lam)[..., None] * bmat
    return lam_bar, b_bar


def _linear_combine(e1, e2):
    a1, b1 = e1
    a2, b2 = e2
    return a1 * a2, a2 * b1 + b2


def s5_states(u, lam_bar, b_bar, h0, reverse):
    if reverse:
        u = jnp.flip(u, axis=1)
    bu = jnp.einsum('blgc,gpc->blgp', u.astype(jnp.complex64), b_bar)
    bu = bu.at[:, 0].add(lam_bar * h0)
    a = jnp.broadcast_to(lam_bar, (1, u.shape[1]) + lam_bar.shape)
    _, h = lax.associative_scan(_linear_combine, (a, bu), axis=1)
    h_last = h[:, -1]
    if reverse:
        h = jnp.flip(h, axis=1)
    return h, h_last


def s5_readout(h, cmat):
    return jnp.real(jnp.einsum('blgp,gcp->blgc', h, cmat))


def gla_chunk_scan(q, k, v, logf, s0, reverse):
    if reverse:
        q, k, v, logf = (jnp.flip(t, axis=1) for t in (q, k, v, logf))
    bsz, n, h, _ = q.shape
    nc = n // HGRN_CHUNK

    def chunks(t):
        return t.astype(F32).reshape(bsz, nc, HGRN_CHUNK, h, t.shape[-1]).transpose(1, 0, 3, 2, 4)

    mask = jnp.tril(jnp.ones((HGRN_CHUNK, HGRN_CHUNK), bool))

    def step(state, inp):
        qc, kc, vc, gc = inp
        b = jnp.cumsum(gc, axis=2)
        o = jnp.einsum('bhtk,bhkv->bhtv', qc * jnp.exp(b), state)
        diff = b[:, :, :, None, :] - b[:, :, None, :, :]
        decay = jnp.exp(jnp.where(mask[:, :, None], diff, -jnp.inf))
        scores = jnp.einsum('bhtk,bhsk,bhtsk->bhts', qc, kc, decay)
        o = o + jnp.einsum('bhts,bhsv->bhtv', scores, vc)
        b_last = b[:, :, -1:, :]
        state = jnp.exp(b_last[:, :, 0])[..., None] * state + jnp.einsum(
            'bhsk,bhsv->bhkv', kc * jnp.exp(b_last - b), vc)
        return state, o

    s_last, o = lax.scan(step, s0.astype(F32), (chunks(q), chunks(k), chunks(v), chunks(logf)))
    o = o.transpose(1, 0, 3, 2, 4).reshape(bsz, n, h, v.shape[-1])
    if reverse:
        o = jnp.flip(o, axis=1)
    return o, s_last


def mixer_ab(h_ctx, h_lat, cos, sin, w_in, w_out, q_norm_g, k_norm_g, lam_re, lam_im, log_dt,
             b_re, b_im, c_re, c_im, d_skip, glu_w, glu_b, need_ctx):
    split_at = [ATTN_WIDTH, ATTN_WIDTH + KV_WIDTH, ATTN_WIDTH + 2 * KV_WIDTH]

    def project(h):
        bsz, n = h.shape[:2]
        q, k, v, u = jnp.split(h @ w_in, split_at, axis=-1)
        q = rmsnorm(q.reshape(bsz, n, ATTN_HEADS, HEAD_DIM), q_norm_g)
        k = rmsnorm(k.reshape(bsz, n, ATTN_KV_HEADS, HEAD_DIM), k_norm_g)
        v = v.reshape(bsz, n, ATTN_KV_HEADS, HEAD_DIM)
        u = u.astype(F32).reshape(bsz, n, S5_GROUPS, S5_GROUP)
        return q, k, v, u

    def grp(t):
        return t.reshape(t.shape[:2] + (ATTN_KV_HEADS, ATTN_GROUP, HEAD_DIM))

    q_c, k_c, v_c, u_c = project(h_ctx)
    q_l, k_l, v_l, u_l = project(h_lat)
    q_l = apply_axial_rope(q_l, cos, sin)
    k_l = apply_axial_rope(k_l, cos, sin)
    bsz = h_lat.shape[0]

    k_all = jnp.concatenate([k_c, k_l], axis=1)
    v_all = jnp.concatenate([v_c.astype(F32), v_l.astype(F32)], axis=1)
    a_lat = block_attention(grp(q_l), k_all, v_all)

    d = d_skip.astype(F32).reshape(S5_GROUPS, S5_GROUP)
    y_l = d * u_l
    y_c = d * u_c if need_ctx else None
    h0 = jnp.zeros((bsz, S5_GROUPS, S5_STATE), jnp.complex64)
    for di, rev in enumerate((False, True)):
        lam_bar, b_bar = s5_discretise(lam_re[di], lam_im[di], log_dt[di], b_re[di], b_im[di])
        cmat = lax.complex(c_re[di].astype(F32), c_im[di].astype(F32))
        hc, hc_last = s5_states(u_c, lam_bar, b_bar, h0, rev)
        hl, _ = s5_states(u_l, lam_bar, b_bar, hc_last, rev)
        y_l = y_l + s5_readout(hl, cmat)
        if need_ctx:
            y_c = y_c + s5_readout(hc, cmat)

    def glu(y):
        z = jax.nn.gelu(y.reshape(y.shape[:2] + (S5_WIDTH,))) @ glu_w + glu_b
        a, g = jnp.split(z, 2, axis=-1)
        return a * jax.nn.sigmoid(g)

    out_l = jnp.concatenate([a_lat, glu(y_l)], axis=-1) @ w_out
    out_c = None
    if need_ctx:
        a_ctx = attend(grp(q_c), k_c, v_c).reshape(bsz, h_ctx.shape[1], ATTN_WIDTH)
        out_c = jnp.concatenate([a_ctx, glu(y_c)], axis=-1) @ w_out
    return out_c, out_l


def mixer_c(h_ctx, h_lat, w_in, w_out, lb, norm_g, need_ctx):
    split_at = [HGRN_KEY, 2 * HGRN_KEY, 3 * HGRN_KEY, 3 * HGRN_KEY + HGRN_VAL]

    def project(h):
        bsz, n = h.shape[:2]
        heads = lambda t, dim: t.reshape(bsz, n, HGRN_HEADS, dim)
        q, zf_f, zf_b, i, g = jnp.split((h @ w_in).astype(F32), split_at, axis=-1)
        q = heads(jax.nn.silu(q), HGRN_DK)
        v = heads(i, HGRN_DV)
        gates = []
        for di, zf in enumerate((zf_f, zf_b)):
            f = lb[di] + (1.0 - lb[di]) * jax.nn.sigmoid(zf)
            gates.append((heads(1.0 - f, HGRN_DK), heads(jnp.log(f), HGRN_DK)))
        return q, v, gates, g

    def readout(o_f, o_b, g):
        bsz, n = g.shape[:2]
        o = rmsnorm(o_f + o_b, norm_g) * jax.nn.silu(g).reshape(bsz, n, HGRN_HEADS, HGRN_DV)
        return o.reshape(bsz, n, HGRN_VAL) @ w_out

    q_c, v_c, gates_c, g_c = project(h_ctx)
    q_l, v_l, gates_l, g_l = project(h_lat)
    s0 = jnp.zeros((h_lat.shape[0], HGRN_HEADS, HGRN_DK, HGRN_DV), F32)
    outs_c, outs_l = [], []
    for di, rev in enumerate((False, True)):
        k_c, lf_c = gates_c[di]
        k_l, lf_l = gates_l[di]
        o_c, s_c = gla_chunk_scan(q_c, k_c, v_c, lf_c, s0, rev)
        o_l, _ = gla_chunk_scan(q_l, k_l, v_l, lf_l, s_c, rev)
        outs_c.append(o_c)
        outs_l.append(o_l)
    out_l = readout(outs_l[0], outs_l[1], g_l)
    out_c = readout(outs_c[0], outs_c[1], g_c) if need_ctx else None
    return out_c, out_l


def dwconv3(h, w, b):
    hp = jnp.pad(h, ((0, 0), (1, 1), (0, 0)))
    return hp[:, :-2] * w[0] + hp[:, 1:-1] * w[1] + hp[:, 2:] * w[2] + b


def conv_ffn(h, w_up, conv_w, conv_b, w_down):
    u = dwconv3(h @ w_up, conv_w, conv_b)
    a, v = jnp.split(u, 2, axis=-1)
    return (jax.nn.silu(a) * v) @ w_down


def setup_inputs(seed: int = 0) -> dict:
    key = jax.random.key(seed)
    ks = iter(jax.random.split(key, 40))

    def nrm(shape, scale):
        return scale * jax.random.normal(next(ks), shape, F32)

    G, P, C = S5_GROUPS, S5_STATE, S5_GROUP
    n_idx = jnp.arange(P, dtype=F32)
    return {
        "x": nrm((BATCH, SEQ, D_MODEL), 1.0),
        "c": nrm((BATCH, D_MODEL), 1.0),
        "ctx": nrm((BATCH, CTX_LEN, D_MODEL), 1.0),
        "c_ctx": nrm((D_MODEL,), 1.0),
        "mod_w": nrm((DEPTH, D_MODEL, 6 * D_MODEL), D_MODEL ** -0.5),
        "mod_b": nrm((DEPTH, 6 * D_MODEL), 0.01),
        "norm_g": 1.0 + nrm((DEPTH, 4, D_MODEL), 0.1),
        "ab_w_in": nrm((N_EVEN, D_MODEL, AB_IN), D_MODEL ** -0.5),
        "ab_w_out": nrm((N_EVEN, AB_OUT, D_MODEL), AB_OUT ** -0.5),
        "attn_q_norm": 1.0 + nrm((N_EVEN, HEAD_DIM), 0.1),
        "attn_k_norm": 1.0 + nrm((N_EVEN, HEAD_DIM), 0.1),
        "s5_lam_re": -0.5 + nrm((N_EVEN, 2, G, P), 0.01),
        "s5_lam_im": math.pi * n_idx + nrm((N_EVEN, 2, G, P), 0.01),
        "s5_log_dt": jax.random.uniform(next(ks), (N_EVEN, 2, G), F32, math.log(1e-3), math.log(1e-1)),
        "s5_b_re": nrm((N_EVEN, 2, G, P, C), (2.0 * C) ** -0.5),
        "s5_b_im": nrm((N_EVEN, 2, G, P, C), (2.0 * C) ** -0.5),
        "s5_c_re": nrm((N_EVEN, 2, G, C, P), 0.5),
        "s5_c_im": nrm((N_EVEN, 2, G, C, P), 0.5),
        "s5_d": nrm((N_EVEN, S5_WIDTH), 1.0),
        "s5_glu_w": nrm((N_EVEN, S5_WIDTH, 2 * S5_WIDTH), S5_WIDTH ** -0.5),
        "s5_glu_b": nrm((N_EVEN, 2 * S5_WIDTH), 0.01),
        "c_w_in": nrm((N_ODD, D_MODEL, C_IN), D_MODEL ** -0.5),
        "c_w_out": nrm((N_ODD, HGRN_VAL, D_MODEL), HGRN_VAL ** -0.5),
        "hgrn_lb_logits": nrm((DEPTH, 2, HGRN_KEY), 0.1),
        "hgrn_norm": 1.0 + nrm((N_ODD, HGRN_DV), 0.1),
        "ffn_w_up": nrm((DEPTH, D_MODEL, 2 * D_FF), D_MODEL ** -0.5),
        "ffn_conv_w": nrm((DEPTH, 3, 2 * D_FF), 3.0 ** -0.5),
        "ffn_conv_b": nrm((DEPTH, 2 * D_FF), 0.01),
        "ffn_w_down": nrm((DEPTH, D_FF, D_MODEL), D_FF ** -0.5),
    }


def reference(x, c, ctx, c_ctx, mod_w, mod_b, norm_g, ab_w_in, ab_w_out, attn_q_norm, attn_k_norm,
              s5_lam_re, s5_lam_im, s5_log_dt, s5_b_re, s5_b_im, s5_c_re, s5_c_im, s5_d,
              s5_glu_w, s5_glu_b, c_w_in, c_w_out, hgrn_lb_logits, hgrn_norm,
              ffn_w_up, ffn_conv_w, ffn_conv_b, ffn_w_down):
    n_tokens = x.shape[1]
    cos, sin = axial_rope_tables(n_tokens)
    lb_all = jnp.cumsum(jax.nn.softmax(hgrn_lb_logits.astype(F32), axis=0), axis=0)
    lb_all = lb_all - lb_all[:1]
    sc = jax.nn.silu(c)
    sc_ctx = jax.nn.silu(c_ctx)
    for l in range(DEPTH):
        last = l == DEPTH - 1
        m_lat = jnp.split((sc @ mod_w[l] + mod_b[l])[:, None, :], 6, axis=-1)
        m_ctx = jnp.split(sc_ctx @ mod_w[l] + mod_b[l], 6, axis=-1)
        g = norm_g[l]
        h_lat = modulate(rmsnorm(x, g[0]), m_lat[0], m_lat[1])
        h_ctx = modulate(rmsnorm(ctx, g[0]), m_ctx[0], m_ctx[1])
        if l % 2 == 0:
            e = l // 2
            y_ctx, y_lat = mixer_ab(h_ctx, h_lat, cos, sin, ab_w_in[e], ab_w_out[e], attn_q_norm[e],
                                    attn_k_norm[e], s5_lam_re[e], s5_lam_im[e], s5_log_dt[e],
                                    s5_b_re[e], s5_b_im[e], s5_c_re[e], s5_c_im[e], s5_d[e],
                                    s5_glu_w[e], s5_glu_b[e], not last)
        else:
            o = l // 2
            y_ctx, y_lat = mixer_c(h_ctx, h_lat, c_w_in[o], c_w_out[o], lb_all[l], hgrn_norm[o], not last)
        x = x + m_lat[2] * rmsnorm(y_lat, g[1])
        h_lat = modulate(rmsnorm(x, g[2]), m_lat[3], m_lat[4])
        x = x + m_lat[5] * rmsnorm(conv_ffn(h_lat, ffn_w_up[l], ffn_conv_w[l], ffn_conv_b[l], ffn_w_down[l]), g[3])
        if not last:
            ctx = ctx + m_ctx[2] * rmsnorm(y_ctx, g[1])
            h_ctx = modulate(rmsnorm(ctx, g[2]), m_ctx[3], m_ctx[4])
            ctx = ctx + m_ctx[5] * rmsnorm(conv_ffn(h_ctx, ffn_w_up[l], ffn_conv_w[l], ffn_conv_b[l], ffn_w_down[l]), g[3])
    return x
```

```python
import functools
import math

import jax
import jax.numpy as jnp
from jax import lax
from jax.experimental import pallas as pl
from jax.experimental.pallas import tpu as pltpu

F32 = jnp.float32
BF16 = jnp.bfloat16
EPS = 1e-6

LANES = 128
SUBLANES = 8
BF16_ROWS = 16

HEAD_DIM = 128
ATTN_HEADS = 8
ATTN_KV_HEADS = 2
ATTN_GROUP = ATTN_HEADS // ATTN_KV_HEADS
GRID_W = 64
ROPE_THETA = 10000.0
AXIS_FREQS = HEAD_DIM // 4

S5_GROUP = 16
S5_STATE = 64
S5_CHUNK = 16

HGRN_DK = 128
HGRN_CHUNK = 64

VMEM_LIMIT = 56 * 1024 * 1024


def _cparams(sem):
    return pltpu.CompilerParams(dimension_semantics=sem, vmem_limit_bytes=VMEM_LIMIT)


def _vec_map(vec, nd_grid):
    shared = vec.shape[0] == 1
    if nd_grid == 2:
        return lambda g, i: (0 if shared else g, 0, 0)
    return lambda g, i, j: (0 if shared else g, 0, 0)


def _rms(x):
    return x * lax.rsqrt(jnp.mean(x * x, axis=-1, keepdims=True) + EPS)


def _mod_kernel(s_ref, w_ref, b_ref, o_ref):
    s = s_ref[...]
    a = s * jax.nn.sigmoid(s)
    o_ref[0] = jnp.dot(a.astype(BF16), w_ref[0].astype(BF16), preferred_element_type=F32) + b_ref[0]


def _modulation(cvec, mod_w, mod_b):
    depth, d, n = mod_w.shape
    rows = cvec.shape[0]
    tn = 1024
    return pl.pallas_call(
        _mod_kernel,
        grid=(depth, n // tn),
        in_specs=[
            pl.BlockSpec((rows, d), lambda l, j: (0, 0)),
            pl.BlockSpec((1, d, tn), lambda l, j: (l, 0, j)),
            pl.BlockSpec((1, 1, tn), lambda l, j: (l, 0, j)),
        ],
        out_specs=pl.BlockSpec((1, rows, tn), lambda l, j: (l, 0, j)),
        out_shape=jax.ShapeDtypeStruct((depth, rows, n), F32),
        compiler_params=_cparams(("parallel", "parallel")),
        name="modulation",
    )(cvec, mod_w, mod_b.reshape(depth, 1, n))


def _proj_kernel(x_ref, g_ref, sh_ref, sc_ref, w_ref, *rest, qk_heads):
    if qk_heads:
        gain_ref, cos_ref, s1_ref, s2_ref, o_ref, h_ref = rest
    else:
        o_ref, h_ref = rest
    j = pl.program_id(2)

    @pl.when(j == 0)
    def _():
        h = _rms(x_ref[0]) * g_ref[...]
        h = h * (1.0 + sc_ref[0]) + sh_ref[0]
        h_ref[...] = h.astype(BF16)

    acc = jnp.dot(h_ref[...], w_ref[...], preferred_element_type=F32)

    if not qk_heads:
        o_ref[0] = acc.astype(o_ref.dtype)
        return

    @pl.when(j == 0)
    def _():
        cos, s1, s2 = cos_ref[...], s1_ref[...], s2_ref[...]
        for hh in range(qk_heads):
            sl = slice(hh * HEAD_DIM, (hh + 1) * HEAD_DIM)
            y = _rms(acc[:, sl]) * gain_ref[:, sl]
            y = y * cos + pltpu.roll(y, HEAD_DIM - AXIS_FREQS, 1) * s1 + pltpu.roll(y, AXIS_FREQS, 1) * s2
            o_ref[0, :, sl] = y.astype(o_ref.dtype)

    @pl.when(j != 0)
    def _():
        o_ref[0] = acc.astype(o_ref.dtype)


def _proj(x, g, shift, scale, w, *, tm, tn, qk=None, out_dtype=BF16):
    G, T, D = x.shape
    N = w.shape[1]
    tm = min(tm, T)
    in_specs = [
        pl.BlockSpec((1, tm, D), lambda b, i, j: (b, i, 0)),
        pl.BlockSpec((1, D), lambda b, i, j: (0, 0)),
        pl.BlockSpec((1, 1, D), _vec_map(shift, 3)),
        pl.BlockSpec((1, 1, D), _vec_map(scale, 3)),
        pl.BlockSpec((D, tn), lambda b, i, j: (0, j)),
    ]
    args = [x, g.reshape(1, D), shift, scale, w]
    qk_heads = 0
    if qk is not None:
        gain, cos, s1, s2 = qk
        qk_heads = gain.shape[1] // HEAD_DIM
        assert qk_heads * HEAD_DIM == tn
        in_specs += [pl.BlockSpec((1, tn), lambda b, i, j: (0, 0))]
        in_specs += [pl.BlockSpec((tm, HEAD_DIM), lambda b, i, j: (i, 0))] * 3
        args += [gain, cos, s1, s2]
    return pl.pallas_call(
        functools.partial(_proj_kernel, qk_heads=qk_heads),
        grid=(G, T // tm, N // tn),
        in_specs=in_specs,
        out_specs=pl.BlockSpec((1, tm, tn), lambda b, i, j: (b, i, j)),
        out_shape=jax.ShapeDtypeStruct((G, T, N), out_dtype),
        scratch_shapes=[pltpu.VMEM((tm, D), BF16)],
        compiler_params=_cparams(("parallel", "parallel", "arbitrary")),
        name="norm_mod_proj",
    )(*args)


def _attn_kernel(q_ref, k_ref, v_ref, *rest, tq, has_prefix):
    if has_prefix:
        kp_ref, vp_ref, o_ref = rest
    else:
        (o_ref,) = rest
    nt = (((1,), (1,)), ((), ()))
    q = jnp.concatenate(
        [q_ref[0, :, g * HEAD_DIM:(g + 1) * HEAD_DIM] for g in range(ATTN_GROUP)], axis=0)
    s = lax.dot_general(q, k_ref[0], nt, preferred_element_type=F32)
    m = jnp.max(s, axis=-1, keepdims=True)
    if has_prefix:
        sp = lax.dot_general(q, kp_ref[0], nt, preferred_element_type=F32)
        m = jnp.maximum(m, jnp.max(sp, axis=-1, keepdims=True))
    p = jnp.exp(s - m)
    l = jnp.sum(p, axis=-1, keepdims=True)
    o = jnp.dot(p.astype(BF16), v_ref[0], preferred_element_type=F32)
    if has_prefix:
        pp = jnp.exp(sp - m)
        l = l + jnp.sum(pp, axis=-1, keepdims=True)
        o = o + jnp.dot(pp.astype(BF16), vp_ref[0], preferred_element_type=F32)
    o = o * (1.0 / l)
    for g in range(ATTN_GROUP):
        o_ref[0, :, g * HEAD_DIM:(g + 1) * HEAD_DIM] = o[g * tq:(g + 1) * tq].astype(o_ref.dtype)


def _attention(qkvu, prefix, *, tq):
    B, L, _ = qkvu.shape
    tq = min(tq, L)
    gw = ATTN_GROUP * HEAD_DIM
    k_blk = ATTN_HEADS
    v_blk = ATTN_HEADS + ATTN_KV_HEADS
    in_specs = [
        pl.BlockSpec((1, tq, gw), lambda b, h, i: (b, i, h)),
        pl.BlockSpec((1, L, HEAD_DIM), lambda b, h, i: (b, 0, k_blk + h)),
        pl.BlockSpec((1, L, HEAD_DIM), lambda b, h, i: (b, 0, v_blk + h)),
    ]
    args = [qkvu, qkvu, qkvu]
    if prefix is not None:
        Lp = prefix.shape[1]
        in_specs += [
            pl.BlockSpec((1, Lp, HEAD_DIM), lambda b, h, i: (b, 0, k_blk + h)),
            pl.BlockSpec((1, Lp, HEAD_DIM), lambda b, h, i: (b, 0, v_blk + h)),
        ]
        args += [prefix, prefix]
    return pl.pallas_call(
        functools.partial(_attn_kernel, tq=tq, has_prefix=prefix is not None),
        grid=(B, ATTN_KV_HEADS, L // tq),
        in_specs=in_specs,
        out_specs=pl.BlockSpec((1, tq, gw), lambda b, h, i: (b, i, h)),
        out_shape=jax.ShapeDtypeStruct((B, L, ATTN_HEADS * HEAD_DIM), BF16),
        compiler_params=_cparams(("parallel", "parallel", "arbitrary")),
        name="gqa_attention",
    )(*args)


S5_GB = 4


def _s5_end_kernel(u_ref, w_ref, ef_ref, eb_ref):
    half = 2 * S5_STATE
    for i in range(S5_GB):
        e = jnp.dot(u_ref[i], w_ref[i], preferred_element_type=F32)
        ef_ref[:, i * half:(i + 1) * half] = e[:, :half]
        eb_ref[:, i * half:(i + 1) * half] = e[:, half:]


def _s5_scan_kernel(ef_ref, eb_ref, cf_ref, cb_ref, hf_ref, hb_ref, *, batch, nc_ctx, nc_all):
    cps = SUBLANES // batch
    width = ef_ref.shape[1]

    def advance(h, e, c_ref):
        return (c_ref[0:1, :] * h + c_ref[1:2, :] * pltpu.roll(h, width - S5_STATE, 1)
                + c_ref[2:3, :] * pltpu.roll(h, S5_STATE, 1) + e)

    def fwd_body(i, h):
        r = pl.multiple_of(i * SUBLANES, SUBLANES)
        e8 = ef_ref[pl.ds(r, SUBLANES), :]
        outs = []
        for c in range(cps):
            outs.append(h)
            h = advance(h, e8[c * batch:(c + 1) * batch], cf_ref)
        hf_ref[pl.ds(r, SUBLANES), :] = jnp.concatenate(outs, axis=0)
        return h

    lax.fori_loop(0, nc_all // cps, fwd_body, jnp.zeros((batch, width), F32))

    def bwd_body(lo, i, h):
        r = pl.multiple_of((lo - 1 - i) * SUBLANES, SUBLANES)
        e8 = eb_ref[pl.ds(r, SUBLANES), :]
        outs = [None] * cps
        for c in reversed(range(cps)):
            outs[c] = h
            h = advance(h, e8[c * batch:(c + 1) * batch], cb_ref)
        hb_ref[pl.ds(r, SUBLANES), :] = jnp.concatenate(outs, axis=0)
        return h

    h = jnp.zeros((batch, width), F32)
    h = lax.fori_loop(0, nc_ctx // cps, functools.partial(bwd_body, nc_ctx // cps), h)
    lax.fori_loop(0, (nc_all - nc_ctx) // cps, functools.partial(bwd_body, nc_all // cps), h)


def _s5_out_kernel(u_ref, tz_ref, hf_ref, hb_ref, wf_ref, wb_ref, d_ref, y_ref):
    half = 2 * S5_STATE
    for i in range(S5_GB):
        u = u_ref[i]
        y = jnp.dot(u, tz_ref[i], preferred_element_type=F32)
        y += jnp.dot(hf_ref[:, i * half:(i + 1) * half].astype(BF16), wf_ref[i], preferred_element_type=F32)
        y += jnp.dot(hb_ref[:, i * half:(i + 1) * half].astype(BF16), wb_ref[i], preferred_element_type=F32)
        y += u.astype(F32) * d_ref[i]
        y_ref[i] = y.astype(y_ref.dtype)


def _s5_tables(lam_re, lam_im, log_dt, b_re, b_im, c_re, c_im, d_skip):
    T, C, P = S5_CHUNK, S5_GROUP, S5_STATE
    G = lam_re.shape[1]
    hi = lax.Precision.HIGHEST
    lam = lax.complex(jnp.minimum(lam_re.astype(F32), -1e-4), lam_im.astype(F32))
    dt = jnp.exp(log_dt.astype(F32))[..., None]
    lam_dt = lam * dt
    lam_bar = jnp.exp(lam_dt)
    bmat = lax.complex(b_re.astype(F32), b_im.astype(F32))
    b_bar = ((lam_bar - 1.0) / lam)[..., None] * bmat
    cmat = lax.complex(c_re.astype(F32), c_im.astype(F32))
    steps = jnp.arange(T + 1, dtype=F32)
    pw = jnp.exp(lam_dt[:, None] * steps[None, :, None, None])

    kern = jnp.real(jnp.einsum('zgcp,zjgp,zgpd->zjgcd', cmat, pw[:, :T], b_bar, precision=hi))
    s_idx = jnp.arange(T)[:, None]
    t_idx = jnp.arange(T)[None, :]

    def toeplitz(k, lag):
        m = jnp.where((lag >= 0)[:, :, None, None, None], k[jnp.clip(lag, 0, T - 1)], 0.0)
        return m.transpose(2, 0, 4, 1, 3).reshape(G, T * C, T * C)

    eye = jnp.eye(T * C, dtype=F32)
    tz = toeplitz(kern[0], t_idx - s_idx) + toeplitz(kern[1], s_idx - t_idx)

    def end_w(p_sel, bb):
        w = p_sel[..., None] * bb[None]
        w = jnp.concatenate([jnp.real(w), jnp.imag(w)], axis=2)
        return w.transpose(1, 0, 3, 2).reshape(G, T * C, 2 * P)

    w_end = jnp.concatenate([end_w(pw[0, :T][::-1], b_bar[0]), end_w(pw[1, :T], b_bar[1])], axis=-1)

    def in_w(p_sel, cm):
        z = cm[None] * p_sel[:, :, None, :]
        w = jnp.concatenate([jnp.real(z), -jnp.imag(z)], axis=-1)
        return w.transpose(1, 3, 0, 2).reshape(G, 2 * P, T * C)

    w_in_f = in_w(pw[0, 1:], cmat[0])
    w_in_b = in_w(pw[1, 1:][::-1], cmat[1])

    def coef(z):
        zero = jnp.zeros_like(jnp.real(z))
        rows = [[jnp.real(z), jnp.real(z)], [-jnp.imag(z), zero], [zero, jnp.imag(z)]]
        return jnp.concatenate(
            [jnp.concatenate(r, axis=-1).reshape(1, G * 2 * P) for r in rows], axis=0)

    d_tile = jnp.tile(d_skip.astype(F32).reshape(G, 1, C), (1, T, 1)).reshape(G, 1, T * C)
    del eye
    return (tz.astype(BF16), w_end.astype(BF16), w_in_f.astype(BF16), w_in_b.astype(BF16),
            coef(pw[0, T]), coef(pw[1, T]), d_tile)


def _s5(u_ctx, u_lat, tables):
    tz, w_end, w_in_f, w_in_b, cf, cb, d_tile = tables
    T, C, P = S5_CHUNK, S5_GROUP, S5_STATE
    B, Lc, W = u_ctx.shape
    L = u_lat.shape[1]
    G = W // C
    nc_ctx, nc_all = Lc // T, (Lc + L) // T
    cps = SUBLANES // B
    assert SUBLANES % B == 0 and nc_ctx % cps == 0 and (nc_all - nc_ctx) % cps == 0
    M = nc_all * B
    half = 2 * P
    u = jnp.concatenate([u_ctx, u_lat], axis=1)
    u = u.reshape(B, nc_all, T, G, C).transpose(3, 1, 0, 2, 4).reshape(G, M, T * C)

    ef, eb = pl.pallas_call(
        _s5_end_kernel,
        grid=(G // S5_GB,),
        in_specs=[
            pl.BlockSpec((S5_GB, M, T * C), lambda g: (g, 0, 0)),
            pl.BlockSpec((S5_GB, T * C, 2 * half), lambda g: (g, 0, 0)),
        ],
        out_specs=[pl.BlockSpec((M, S5_GB * half), lambda g: (0, g))] * 2,
        out_shape=[jax.ShapeDtypeStruct((M, G * half), F32)] * 2,
        compiler_params=_cparams(("parallel",)),
        name="s5_chunk_end_states",
    )(u, w_end)

    cw = 512
    hf, hb = pl.pallas_call(
        functools.partial(_s5_scan_kernel, batch=B, nc_ctx=nc_ctx, nc_all=nc_all),
        grid=(G * half // cw,),
        in_specs=[pl.BlockSpec((M, cw), lambda j: (0, j))] * 2 + [pl.BlockSpec((3, cw), lambda j: (0, j))] * 2,
        out_specs=[pl.BlockSpec((M, cw), lambda j: (0, j))] * 2,
        out_shape=[jax.ShapeDtypeStruct((M, G * half), F32)] * 2,
        compiler_params=_cparams(("parallel",)),
        name="s5_chunk_scan",
    )(ef, eb, cf, cb)

    y = pl.pallas_call(
        _s5_out_kernel,
        grid=(G // S5_GB,),
        in_specs=[
            pl.BlockSpec((S5_GB, M, T * C), lambda g: (g, 0, 0)),
            pl.BlockSpec((S5_GB, T * C, T * C), lambda g: (g, 0, 0)),
            pl.BlockSpec((M, S5_GB * half), lambda g: (0, g)),
            pl.BlockSpec((M, S5_GB * half), lambda g: (0, g)),
            pl.BlockSpec((S5_GB, half, T * C), lambda g: (g, 0, 0)),
            pl.BlockSpec((S5_GB, half, T * C), lambda g: (g, 0, 0)),
            pl.BlockSpec((S5_GB, 1, T * C), lambda g: (g, 0, 0)),
        ],
        out_specs=pl.BlockSpec((S5_GB, M, T * C), lambda g: (g, 0, 0)),
        out_shape=jax.ShapeDtypeStruct((G, M, T * C), BF16),
        compiler_params=_cparams(("parallel",)),
        name="s5_chunk_outputs",
    )(u, tz, hf, hb, w_in_f, w_in_b, d_tile)

    y = y.reshape(G, nc_all, B, T, C).transpose(2, 1, 3, 0, 4).reshape(B, Lc + L, W)
    return y[:, :Lc], y[:, Lc:]


def _glu_kernel(y_ref, w_ref, b_ref, o_ref):
    y = y_ref[0].astype(F32)
    z = jnp.dot(jax.nn.gelu(y).astype(BF16), w_ref[...], preferred_element_type=F32) + b_ref[...]
    n = o_ref.shape[-1]
    o_ref[0] = (z[:, :n] * jax.nn.sigmoid(z[:, n:])).astype(o_ref.dtype)


def _glu(y, w, b, *, tm):
    G, T, W = y.shape
    tm = min(tm, T)
    return pl.pallas_call(
        _glu_kernel,
        grid=(G, T // tm),
        in_specs=[
            pl.BlockSpec((1, tm, W), lambda g, i: (g, i, 0)),
            pl.BlockSpec((W, 2 * W), lambda g, i: (0, 0)),
            pl.BlockSpec((1, 2 * W), lambda g, i: (0, 0)),
        ],
        out_specs=pl.BlockSpec((1, tm, W), lambda g, i: (g, i, 0)),
        out_shape=jax.ShapeDtypeStruct((G, T, W), BF16),
        compiler_params=_cparams(("parallel", "parallel")),
        name="s5_glu",
    )(y, w, b.reshape(1, 2 * W))


def _out_res_kernel(*refs, widths):
    n = len(widths)
    lhs = refs[:n]
    w_ref, x_ref, gate_ref, g_ref, o_ref = refs[n:]
    y = None
    off = 0
    for r, k in zip(lhs, widths):
        part = jnp.dot(r[0], w_ref[off:off + k, :], preferred_element_type=F32)
        y = part if y is None else y + part
        off += k
    o_ref[0] = x_ref[0] + gate_ref[0] * (_rms(y) * g_ref[...])


def _out_res(parts, w, x, gate, g, *, tm):
    G, T, D = x.shape
    tm = min(tm, T)
    widths = tuple(p.shape[-1] for p in parts)
    K = sum(widths)
    in_specs = [pl.BlockSpec((1, tm, k), lambda b, i: (b, i, 0)) for k in widths]
    in_specs += [
        pl.BlockSpec((K, D), lambda b, i: (0, 0)),
        pl.BlockSpec((1, tm, D), lambda b, i: (b, i, 0)),
        pl.BlockSpec((1, 1, D), _vec_map(gate, 2)),
        pl.BlockSpec((1, D), lambda b, i: (0, 0)),
    ]
    return pl.pallas_call(
        functools.partial(_out_res_kernel, widths=widths),
        grid=(G, T // tm),
        in_specs=in_specs,
        out_specs=pl.BlockSpec((1, tm, D), lambda b, i: (b, i, 0)),
        out_shape=jax.ShapeDtypeStruct((G, T, D), F32),
        compiler_params=_cparams(("parallel", "parallel")),
        name="out_proj_residual",
    )(*parts, w, x, gate, g.reshape(1, D))


HALO = BF16_ROWS


def _ffn_kernel(x_ref, xp_ref, xn_ref, g2_ref, sh_ref, sc_ref, wa_ref, wv_ref, cwa_ref, cwv_ref,
                cba_ref, cbv_ref, wd_ref, gate_ref, g3_ref, o_ref, h_ref, acc_ref, ua_ref, uv_ref, *, tm):
    i = pl.program_id(1)
    f = pl.program_id(2)

    def normmod(x):
        h = _rms(x) * g2_ref[...]
        return h * (1.0 + sc_ref[0]) + sh_ref[0]

    @pl.when(f == 0)
    def _():
        zeros = jnp.zeros((HALO - SUBLANES, x_ref.shape[-1]), F32)
        hp = jnp.where(i == 0, 0.0, normmod(xp_ref[0]))
        hn = jnp.where(i == pl.num_programs(1) - 1, 0.0, normmod(xn_ref[0]))
        h_ref[0:HALO] = jnp.concatenate([zeros, hp], axis=0).astype(BF16)
        h_ref[HALO:HALO + tm] = normmod(x_ref[0]).astype(BF16)
        h_ref[HALO + tm:] = jnp.concatenate([hn, zeros], axis=0).astype(BF16)
        acc_ref[...] = jnp.zeros_like(acc_ref)

    h = h_ref[...]
    ua_ref[...] = jnp.dot(h, wa_ref[...], preferred_element_type=F32)
    uv_ref[...] = jnp.dot(h, wv_ref[...], preferred_element_type=F32)

    def conv(u_ref, cw_ref, cb_ref):
        return (u_ref[HALO - 1:HALO - 1 + tm] * cw_ref[0:1, :] + u_ref[HALO:HALO + tm] * cw_ref[1:2, :]
                + u_ref[HALO + 1:HALO + 1 + tm] * cw_ref[2:3, :] + cb_ref[...])

    a = conv(ua_ref, cwa_ref, cba_ref)
    v = conv(uv_ref, cwv_ref, cbv_ref)
    gated = (a * jax.nn.sigmoid(a) * v).astype(BF16)
    acc_ref[...] += jnp.dot(gated, wd_ref[...], preferred_element_type=F32)

    @pl.when(f == pl.num_programs(2) - 1)
    def _():
        o_ref[0] = x_ref[0] + gate_ref[0] * (_rms(acc_ref[...]) * g3_ref[...])


def _conv_ffn(x, g2, shift, scale, w_up, conv_w, conv_b, w_down, gate, g3, *, tm, tf):
    G, T, D = x.shape
    F = w_down.shape[0]
    tm = min(tm, T)
    nf = F // tf
    n_i = T // tm
    rb = tm // SUBLANES
    last_rb = T // SUBLANES - 1
    conv_b = conv_b.reshape(1, 2 * F)
    return pl.pallas_call(
        functools.partial(_ffn_kernel, tm=tm),
        grid=(G, n_i, nf),
        in_specs=[
            pl.BlockSpec((1, tm, D), lambda b, i, f: (b, i, 0)),
            pl.BlockSpec((1, SUBLANES, D), lambda b, i, f: (b, jnp.maximum(i * rb - 1, 0), 0)),
            pl.BlockSpec((1, SUBLANES, D), lambda b, i, f: (b, jnp.minimum((i + 1) * rb, last_rb), 0)),
            pl.BlockSpec((1, D), lambda b, i, f: (0, 0)),
            pl.BlockSpec((1, 1, D), _vec_map(shift, 3)),
            pl.BlockSpec((1, 1, D), _vec_map(scale, 3)),
            pl.BlockSpec((D, tf), lambda b, i, f: (0, f)),
            pl.BlockSpec((D, tf), lambda b, i, f: (0, nf + f)),
            pl.BlockSpec((3, tf), lambda b, i, f: (0, f)),
            pl.BlockSpec((3, tf), lambda b, i, f: (0, nf + f)),
            pl.BlockSpec((1, tf), lambda b, i, f: (0, f)),
            pl.BlockSpec((1, tf), lambda b, i, f: (0, nf + f)),
            pl.BlockSpec((tf, D), lambda b, i, f: (f, 0)),
            pl.BlockSpec((1, 1, D), _vec_map(gate, 3)),
            pl.BlockSpec((1, D), lambda b, i, f: (0, 0)),
        ],
        out_specs=pl.BlockSpec((1, tm, D), lambda b, i, f: (b, i, 0)),
        out_shape=jax.ShapeDtypeStruct((G, T, D), F32),
        scratch_shapes=[
            pltpu.VMEM((tm + 2 * HALO, D), BF16),
            pltpu.VMEM((tm, D), F32),
            pltpu.VMEM((tm + 2 * HALO, tf), F32),
            pltpu.VMEM((tm + 2 * HALO, tf), F32),
        ],
        compiler_params=_cparams(("parallel", "parallel", "arbitrary")),
        name="conv_ffn",
    )(x, x, x, g2.reshape(1, D), shift, scale, w_up, w_up, conv_w, conv_w, conv_b, conv_b, w_down,
      gate, g3.reshape(1, D))


def _gla_chunk(z, v, q, st, lb, rev, masks):
    T = HGRN_CHUNK
    rpos, level_masks, diag_mask = masks
    nt = (((1,), (1,)), ((), ()))

    def prev(x, k):
        return pltpu.roll(x, (T - k) if rev else k, 0)

    def nxt(x, k):
        return pltpu.roll(x, k if rev else (T - k), 0)

    def first_of_block(x, lo, hi):
        k = lo
        while k < hi:
            x = jnp.where((rpos & (2 * k - 1)) >= k, prev(x, k), x)
            k *= 2
        return x

    f = lb + (1.0 - lb) * jax.nn.sigmoid(z)
    kk = 1.0 - f
    lf = jnp.log(f)
    b = lf
    k = 1
    while k < T:
        b = b + jnp.where(rpos >= k, prev(b, k), 0.0)
        k *= 2
    last = 0 if rev else T - 1
    btot = b[last:last + 1, :]

    kdec = (kk * jnp.exp(btot - b)).astype(BF16)
    st_new = st * jnp.exp(btot) + lax.dot_general(v, kdec, (((0,), (0,)), ((), ())),
                                                  preferred_element_type=F32)
    if q is None:
        return None, st_new

    bs = jnp.where(rpos >= 1, prev(b, 1), 0.0)
    fb4 = first_of_block(bs, 1, 4)
    fb16 = first_of_block(fb4, 4, 16)
    kk_b = kk
    scores = jnp.where(diag_mask, jnp.sum(q * kk, axis=-1, keepdims=True), 0.0)
    for m, bref in ((16, fb16), (4, fb4), (1, bs)):
        ql = (q * jnp.exp(b - bref)).astype(BF16)
        ks = []
        for j in (1, 2, 3):
            if m == 16:
                row = (T - 1 - 16 * j) if rev else 16 * j
                kb = bs[row:row + 1, :]
            elif m == 4:
                kb = first_of_block(nxt(fb4, 4 * j), 4, 16)
            else:
                kb = first_of_block(nxt(bs, j), 1, 4)
            ks.append((kk_b * jnp.exp(jnp.minimum(kb - b, 0.0))).astype(BF16))
        r = lax.dot_general(ql, jnp.concatenate(ks, axis=0), nt, preferred_element_type=F32)
        for j in (1, 2, 3):
            scores = scores + jnp.where(level_masks[m][j - 1], r[:, (j - 1) * T:j * T], 0.0)

    o = lax.dot_general((q * jnp.exp(b)).astype(BF16), st.astype(BF16), nt, preferred_element_type=F32)
    o = o + jnp.dot(scores.astype(BF16), v, preferred_element_type=F32)
    return o, st_new


def _gla_masks(rev):
    T = HGRN_CHUNK
    row = lax.broadcasted_iota(jnp.int32, (T, T), 0)
    col = lax.broadcasted_iota(jnp.int32, (T, T), 1)
    rpos = lax.broadcasted_iota(jnp.int32, (T, HGRN_DK), 0)
    if rev:
        row, col, rpos = T - 1 - row, T - 1 - col, T - 1 - rpos
    level = {}
    for m in (16, 4, 1):
        sh = m.bit_length() - 1
        same_parent = (row >> (sh + 2)) == (col >> (sh + 2))
        jt = (row >> sh) & 3
        js = (col >> sh) & 3
        level[m] = [same_parent & (jt == j) & (js < j) for j in (1, 2, 3)]
    return rpos, level, row == col


def _gla_kernel(q_ref, zf_ref, zb_ref, v_ref, g_ref, czf_ref, czb_ref, cv_ref, lb_ref, ng_ref,
                o_ref, oacc_ref, *, n_lat, n_ctx):
    T = HGRN_CHUNK
    zero_state = jnp.zeros((v_ref.shape[-1], HGRN_DK), F32)
    for rev, z_ref, cz_ref in ((False, zf_ref, czf_ref), (True, zb_ref, czb_ref)):
        masks = _gla_masks(rev)
        lb = lb_ref[1:2, :] if rev else lb_ref[0:1, :]

        def ctx_body(i, st, rev=rev, cz_ref=cz_ref, lb=lb, masks=masks):
            c = (n_ctx - 1 - i) if rev else i
            r = pl.multiple_of(c * T, T)
            _, st = _gla_chunk(cz_ref[0, pl.ds(r, T), :].astype(F32), cv_ref[0, pl.ds(r, T), :],
                               None, st, lb, rev, masks)
            return st

        def lat_body(i, st, rev=rev, z_ref=z_ref, lb=lb, masks=masks):
            c = (n_lat - 1 - i) if rev else i
            r = pl.multiple_of(c * T, T)
            qz = q_ref[0, pl.ds(r, T), :].astype(F32)
            q = qz * jax.nn.sigmoid(qz)
            o, st = _gla_chunk(z_ref[0, pl.ds(r, T), :].astype(F32), v_ref[0, pl.ds(r, T), :],
                               q, st, lb, rev, masks)
            if rev:
                o = o + oacc_ref[pl.ds(r, T), :]
                gz = g_ref[0, pl.ds(r, T), :].astype(F32)
                o = _rms(o) * ng_ref[...] * (gz * jax.nn.sigmoid(gz))
                o_ref[0, pl.ds(r, T), :] = o.astype(o_ref.dtype)
            else:
                oacc_ref[pl.ds(r, T), :] = o
            return st

        st = lax.fori_loop(0, n_ctx, ctx_body, zero_state)
        lax.fori_loop(0, n_lat, lat_body, st)


def _gla(p_lat, p_ctx, lb, norm_g, n_heads):
    B, L, _ = p_lat.shape
    Lc = p_ctx.shape[1]
    H = n_heads
    dv = HGRN_DK

    def col(group):
        return lambda b, h: (b, 0, group * H + h)

    lat_specs = [pl.BlockSpec((1, L, dv), col(gidx)) for gidx in range(5)]
    ctx_specs = [pl.BlockSpec((1, Lc, dv), col(gidx)) for gidx in (1, 2, 3)]
    return pl.pallas_call(
        functools.partial(_gla_kernel, n_lat=L // HGRN_CHUNK, n_ctx=Lc // HGRN_CHUNK),
        grid=(B, H),
        in_specs=lat_specs + ctx_specs + [
            pl.BlockSpec((2, HGRN_DK), lambda b, h: (0, h)),
            pl.BlockSpec((1, dv), lambda b, h: (0, 0)),
        ],
        out_specs=pl.BlockSpec((1, L, dv), lambda b, h: (b, 0, h)),
        out_shape=jax.ShapeDtypeStruct((B, L, H * dv), BF16),
        scratch_shapes=[pltpu.VMEM((L, dv), F32)],
        compiler_params=_cparams(("parallel", "parallel")),
        name="hgrn2_bidirectional",
    )(*([p_lat] * 5), *([p_ctx] * 3), lb, norm_g.reshape(1, dv))


def _rope_tables(n_tokens):
    rows = n_tokens // GRID_W
    row = jnp.repeat(jnp.arange(rows, dtype=F32), GRID_W)
    colp = jnp.tile(jnp.arange(GRID_W, dtype=F32), rows)
    inv = ROPE_THETA ** (-jnp.arange(AXIS_FREQS, dtype=F32) / AXIS_FREQS)
    ang = jnp.stack([row[:, None] * inv, colp[:, None] * inv], axis=1)
    cos, sin = jnp.cos(ang), jnp.sin(ang)
    zero = jnp.zeros_like(sin)
    full = lambda a, b: jnp.stack([a, b], axis=2).reshape(n_tokens, HEAD_DIM)
    return full(cos, cos), full(-sin, zero), full(zero, sin)


def _identity_rope(n_tokens):
    return (jnp.ones((n_tokens, HEAD_DIM), F32), jnp.zeros((n_tokens, HEAD_DIM), F32),
            jnp.zeros((n_tokens, HEAD_DIM), F32))


def kernel(x, c, ctx, c_ctx, mod_w, mod_b, norm_g, ab_w_in, ab_w_out, attn_q_norm, attn_k_norm,
           s5_lam_re, s5_lam_im, s5_log_dt, s5_b_re, s5_b_im, s5_c_re, s5_c_im, s5_d,
           s5_glu_w, s5_glu_b, c_w_in, c_w_out, hgrn_lb_logits, hgrn_norm,
           ffn_w_up, ffn_conv_w, ffn_conv_b, ffn_w_down):
    B, L, D = x.shape
    Lc = ctx.shape[1]
    depth = mod_w.shape[0]
    attn_w = ATTN_HEADS * HEAD_DIM
    kv_w = ATTN_KV_HEADS * HEAD_DIM
    qk_w = attn_w + kv_w

    lb_all = jnp.cumsum(jax.nn.softmax(hgrn_lb_logits.astype(F32), axis=0), axis=0)
    lb_all = lb_all - lb_all[:1]

    cvec = jnp.concatenate([c, c_ctx[None], jnp.zeros((SUBLANES - (B + 1) % SUBLANES, D), F32)], axis=0)
    mods = _modulation(cvec, mod_w, mod_b)

    rope_lat = _rope_tables(L)
    rope_ctx = _identity_rope(Lc)

    for l in range(depth):
        last = l == depth - 1
        m_lat = [mods[l, :B, k * D:(k + 1) * D][:, None, :] for k in range(6)]
        m_ctx = [mods[l, B:B + 1, k * D:(k + 1) * D][:, None, :] for k in range(6)]
        g = norm_g[l]
        w_up = ffn_w_up[l].astype(BF16)
        w_down = ffn_w_down[l].astype(BF16)
        if l % 2 == 0:
            e = l // 2
            w_in = ab_w_in[e].astype(BF16)
            gain = jnp.concatenate([jnp.tile(attn_q_norm[e] * (HEAD_DIM ** -0.5), ATTN_HEADS),
                                    jnp.tile(attn_k_norm[e], ATTN_KV_HEADS)]).reshape(1, qk_w)
            p_lat = _proj(x, g[0], m_lat[0], m_lat[1], w_in, tm=512, tn=qk_w, qk=(gain,) + rope_lat)
            p_ctx = _proj(ctx, g[0], m_ctx[0], m_ctx[1], w_in, tm=512, tn=qk_w, qk=(gain,) + rope_ctx)
            a_lat = _attention(p_lat, p_ctx, tq=128)
            tables = _s5_tables(s5_lam_re[e], s5_lam_im[e], s5_log_dt[e], s5_b_re[e], s5_b_im[e],
                                s5_c_re[e], s5_c_im[e], s5_d[e])
            u0 = qk_w + kv_w
            y_ctx, y_lat = _s5(p_ctx[..., u0:], p_lat[..., u0:], tables)
            glu_w = s5_glu_w[e].astype(BF16)
            w_out = ab_w_out[e].astype(BF16)
            s_lat = _glu(y_lat, glu_w, s5_glu_b[e], tm=512)
            x = _out_res([a_lat, s_lat], w_out, x, m_lat[2], g[1], tm=512)
            if not last:
                a_ctx = _attention(p_ctx, None, tq=128)
                s_ctx = _glu(y_ctx, glu_w, s5_glu_b[e], tm=512)
                ctx = _out_res([a_ctx, s_ctx], w_out, ctx, m_ctx[2], g[1], tm=512)
        else:
            o_idx = l // 2
            w_in = c_w_in[o_idx].astype(BF16)
            n_heads = c_w_out.shape[1] // HGRN_DK
            p_lat = _proj(x, g[0], m_lat[0], m_lat[1], w_in, tm=512, tn=1024)
            p_ctx = _proj(ctx, g[0], m_ctx[0], m_ctx[1], w_in, tm=512, tn=1024)
            o_lat = _gla(p_lat, p_ctx, lb_all[l], hgrn_norm[o_idx], n_heads)
            x = _out_res([o_lat], c_w_out[o_idx].astype(BF16), x, m_lat[2], g[1], tm=512)
            assert last, "context outputs of the HGRN2 mixer are only needed by a following layer"
        x = _conv_ffn(x, g[2], m_lat[3], m_lat[4], w_up, ffn_conv_w[l], ffn_conv_b[l], w_down,
                      m_lat[5], g[3], tm=512, tf=512)
        if not last:
            ctx = _conv_ffn(ctx, g[2], m_ctx[3], m_ctx[4], w_up, ffn_conv_w[l], ffn_conv_b[l], w_down,
                            m_ctx[5], g[3], tm=512, tf=512)
    return x
```

```python
import functools
import math

import jax
import jax.numpy as jnp
import numpy as np
from jax import lax
from jax.experimental import pallas as pl
from jax.experimental.pallas import tpu as pltpu

F32 = jnp.float32
BF16 = jnp.bfloat16
EPS = 1e-6

LANES = 128
SUBLANES = 8
BF16_ROWS = 16

HEAD_DIM = 128
ATTN_HEADS = 8
ATTN_KV_HEADS = 2
ATTN_GROUP = ATTN_HEADS // ATTN_KV_HEADS
GRID_W = 64
ROPE_THETA = 10000.0
AXIS_FREQS = HEAD_DIM // 4

S5_GROUP = 16
S5_STATE = 64
S5_CHUNK = 16

HGRN_DK = 128
HGRN_CHUNK = 64

VMEM_LIMIT = 56 * 1024 * 1024


def _cparams(sem):
    return pltpu.CompilerParams(dimension_semantics=sem, vmem_limit_bytes=VMEM_LIMIT)


def _vec_map(vec, nd_grid):
    shared = vec.shape[0] == 1
    if nd_grid == 2:
        return lambda g, i: (0 if shared else g, 0, 0)
    return lambda g, i, j: (0 if shared else g, 0, 0)


def _rms(x):
    return x * lax.rsqrt(jnp.mean(x * x, axis=-1, keepdims=True) + EPS)


def _mod_kernel(s_ref, w_ref, b_ref, o_ref):
    s = s_ref[...]
    a = s * jax.nn.sigmoid(s)
    o_ref[0] = jnp.dot(a.astype(BF16), w_ref[0].astype(BF16), preferred_element_type=F32) + b_ref[0]


def _modulation(cvec, mod_w, mod_b):
    depth, d, n = mod_w.shape
    rows = cvec.shape[0]
    tn = 1024
    return pl.pallas_call(
        _mod_kernel,
        grid=(depth, n // tn),
        in_specs=[
            pl.BlockSpec((rows, d), lambda l, j: (0, 0)),
            pl.BlockSpec((1, d, tn), lambda l, j: (l, 0, j)),
            pl.BlockSpec((1, 1, tn), lambda l, j: (l, 0, j)),
        ],
        out_specs=pl.BlockSpec((1, rows, tn), lambda l, j: (l, 0, j)),
        out_shape=jax.ShapeDtypeStruct((depth, rows, n), F32),
        compiler_params=_cparams(("parallel", "parallel")),
        name="modulation",
    )(cvec, mod_w, mod_b.reshape(depth, 1, n))


def _proj_kernel(x_ref, g_ref, sh_ref, sc_ref, w_ref, *rest, qk_heads):
    if qk_heads:
        gain_ref, cos_ref, s1_ref, s2_ref, o_ref, h_ref = rest
    else:
        o_ref, h_ref = rest
    j = pl.program_id(2)

    @pl.when(j == 0)
    def _():
        h = _rms(x_ref[0]) * g_ref[...]
        h = h * (1.0 + sc_ref[0]) + sh_ref[0]
        h_ref[...] = h.astype(BF16)

    acc = jnp.dot(h_ref[...], w_ref[...], preferred_element_type=F32)

    if not qk_heads:
        o_ref[0] = acc.astype(o_ref.dtype)
        return

    @pl.when(j == 0)
    def _():
        cos, s1, s2 = cos_ref[...], s1_ref[...], s2_ref[...]
        for hh in range(qk_heads):
            sl = slice(hh * HEAD_DIM, (hh + 1) * HEAD_DIM)
            y = _rms(acc[:, sl]) * gain_ref[:, sl]
            y = y * cos + pltpu.roll(y, HEAD_DIM - AXIS_FREQS, 1) * s1 + pltpu.roll(y, AXIS_FREQS, 1) * s2
            o_ref[0, :, sl] = y.astype(o_ref.dtype)

    @pl.when(j != 0)
    def _():
        o_ref[0] = acc.astype(o_ref.dtype)


def _proj(x, g, shift, scale, w, *, tm, tn, qk=None, out_dtype=BF16):
    G, T, D = x.shape
    N = w.shape[1]
    tm = min(tm, T)
    in_specs = [
        pl.BlockSpec((1, tm, D), lambda b, i, j: (b, i, 0)),
        pl.BlockSpec((1, D), lambda b, i, j: (0, 0)),
        pl.BlockSpec((1, 1, D), _vec_map(shift, 3)),
        pl.BlockSpec((1, 1, D), _vec_map(scale, 3)),
        pl.BlockSpec((D, tn), lambda b, i, j: (0, j)),
    ]
    args = [x, g.reshape(1, D), shift, scale, w]
    qk_heads = 0
    if qk is not None:
        gain, cos, s1, s2 = qk
        qk_heads = gain.shape[1] // HEAD_DIM
        assert qk_heads * HEAD_DIM == tn
        in_specs += [pl.BlockSpec((1, tn), lambda b, i, j: (0, 0))]
        in_specs += [pl.BlockSpec((tm, HEAD_DIM), lambda b, i, j: (i, 0))] * 3
        args += [gain, cos, s1, s2]
    return pl.pallas_call(
        functools.partial(_proj_kernel, qk_heads=qk_heads),
        grid=(G, T // tm, N // tn),
        in_specs=in_specs,
        out_specs=pl.BlockSpec((1, tm, tn), lambda b, i, j: (b, i, j)),
        out_shape=jax.ShapeDtypeStruct((G, T, N), out_dtype),
        scratch_shapes=[pltpu.VMEM((tm, D), BF16)],
        compiler_params=_cparams(("parallel", "parallel", "arbitrary")),
        name="norm_mod_proj",
    )(*args)


def _attn_kernel(q_ref, k_ref, v_ref, *rest, tq, has_prefix):
    if has_prefix:
        kp_ref, vp_ref, o_ref = rest
    else:
        (o_ref,) = rest
    nt = (((1,), (1,)), ((), ()))
    q = jnp.concatenate(
        [q_ref[0, :, g * HEAD_DIM:(g + 1) * HEAD_DIM] for g in range(ATTN_GROUP)], axis=0)
    s = lax.dot_general(q, k_ref[0], nt, preferred_element_type=F32)
    m = jnp.max(s, axis=-1, keepdims=True)
    if has_prefix:
        sp = lax.dot_general(q, kp_ref[0], nt, preferred_element_type=F32)
        m = jnp.maximum(m, jnp.max(sp, axis=-1, keepdims=True))
    p = jnp.exp(s - m)
    l = jnp.sum(p, axis=-1, keepdims=True)
    o = jnp.dot(p.astype(BF16), v_ref[0], preferred_element_type=F32)
    if has_prefix:
        pp = jnp.exp(sp - m)
        l = l + jnp.sum(pp, axis=-1, keepdims=True)
        o = o + jnp.dot(pp.astype(BF16), vp_ref[0], preferred_element_type=F32)
    o = o * (1.0 / l)
    for g in range(ATTN_GROUP):
        o_ref[0, :, g * HEAD_DIM:(g + 1) * HEAD_DIM] = o[g * tq:(g + 1) * tq].astype(o_ref.dtype)


def _attention(qkvu, prefix, *, tq):
    B, L, _ = qkvu.shape
    tq = min(tq, L)
    gw = ATTN_GROUP * HEAD_DIM
    k_blk = ATTN_HEADS
    v_blk = ATTN_HEADS + ATTN_KV_HEADS
    in_specs = [
        pl.BlockSpec((1, tq, gw), lambda b, h, i: (b, i, h)),
        pl.BlockSpec((1, L, HEAD_DIM), lambda b, h, i: (b, 0, k_blk + h)),
        pl.BlockSpec((1, L, HEAD_DIM), lambda b, h, i: (b, 0, v_blk + h)),
    ]
    args = [qkvu, qkvu, qkvu]
    if prefix is not None:
        Lp = prefix.shape[1]
        in_specs += [
            pl.BlockSpec((1, Lp, HEAD_DIM), lambda b, h, i: (b, 0, k_blk + h)),
            pl.BlockSpec((1, Lp, HEAD_DIM), lambda b, h, i: (b, 0, v_blk + h)),
        ]
        args += [prefix, prefix]
    return pl.pallas_call(
        functools.partial(_attn_kernel, tq=tq, has_prefix=prefix is not None),
        grid=(B, ATTN_KV_HEADS, L // tq),
        in_specs=in_specs,
        out_specs=pl.BlockSpec((1, tq, gw), lambda b, h, i: (b, i, h)),
        out_shape=jax.ShapeDtypeStruct((B, L, ATTN_HEADS * HEAD_DIM), BF16),
        compiler_params=_cparams(("parallel", "parallel", "arbitrary")),
        name="gqa_attention",
    )(*args)


S5_GB = 4


def _s5_end_kernel(u_ref, w_ref, ef_ref, eb_ref):
    half = 2 * S5_STATE
    for i in range(S5_GB):
        e = jnp.dot(u_ref[i], w_ref[i], preferred_element_type=F32)
        ef_ref[:, i * half:(i + 1) * half] = e[:, :half]
        eb_ref[:, i * half:(i + 1) * half] = e[:, half:]


def _s5_scan_kernel(ef_ref, eb_ref, cf_ref, cb_ref, hf_ref, hb_ref, *, batch, nc_ctx, nc_all):
    cps = SUBLANES // batch
    width = ef_ref.shape[1]

    def advance(h, e, c_ref):
        return (c_ref[0:1, :] * h + c_ref[1:2, :] * pltpu.roll(h, width - S5_STATE, 1)
                + c_ref[2:3, :] * pltpu.roll(h, S5_STATE, 1) + e)

    def fwd_body(i, h):
        r = pl.multiple_of(i * SUBLANES, SUBLANES)
        e8 = ef_ref[pl.ds(r, SUBLANES), :]
        outs = []
        for c in range(cps):
            outs.append(h)
            h = advance(h, e8[c * batch:(c + 1) * batch], cf_ref)
        hf_ref[pl.ds(r, SUBLANES), :] = jnp.concatenate(outs, axis=0)
        return h

    lax.fori_loop(0, nc_all // cps, fwd_body, jnp.zeros((batch, width), F32))

    def bwd_body(lo, i, h):
        r = pl.multiple_of((lo - 1 - i) * SUBLANES, SUBLANES)
        e8 = eb_ref[pl.ds(r, SUBLANES), :]
        outs = [None] * cps
        for c in reversed(range(cps)):
            outs[c] = h
            h = advance(h, e8[c * batch:(c + 1) * batch], cb_ref)
        hb_ref[pl.ds(r, SUBLANES), :] = jnp.concatenate(outs, axis=0)
        return h

    h = jnp.zeros((batch, width), F32)
    h = lax.fori_loop(0, nc_ctx // cps, functools.partial(bwd_body, nc_ctx // cps), h)
    lax.fori_loop(0, (nc_all - nc_ctx) // cps, functools.partial(bwd_body, nc_all // cps), h)


def _s5_out_kernel(u_ref, tz_ref, hf_ref, hb_ref, wf_ref, wb_ref, d_ref, y_ref):
    half = 2 * S5_STATE
    for i in range(S5_GB):
        u = u_ref[i]
        y = jnp.dot(u, tz_ref[i], preferred_element_type=F32)
        y += jnp.dot(hf_ref[:, i * half:(i + 1) * half].astype(BF16), wf_ref[i], preferred_element_type=F32)
        y += jnp.dot(hb_ref[:, i * half:(i + 1) * half].astype(BF16), wb_ref[i], preferred_element_type=F32)
        y += u.astype(F32) * d_ref[i]
        y_ref[i] = y.astype(y_ref.dtype)


def _s5_tables(lam_re, lam_im, log_dt, b_re, b_im, c_re, c_im, d_skip):
    T, C, P = S5_CHUNK, S5_GROUP, S5_STATE
    G = lam_re.shape[1]
    hi = lax.Precision.HIGHEST
    lam = lax.complex(jnp.minimum(lam_re.astype(F32), -1e-4), lam_im.astype(F32))
    dt = jnp.exp(log_dt.astype(F32))[..., None]
    lam_dt = lam * dt
    lam_bar = jnp.exp(lam_dt)
    bmat = lax.complex(b_re.astype(F32), b_im.astype(F32))
    b_bar = ((lam_bar - 1.0) / lam)[..., None] * bmat
    cmat = lax.complex(c_re.astype(F32), c_im.astype(F32))
    steps = jnp.arange(T + 1, dtype=F32)
    pw = jnp.exp(lam_dt[:, None] * steps[None, :, None, None])

    kern = jnp.real(jnp.einsum('zgcp,zjgp,zgpd->zjgcd', cmat, pw[:, :T], b_bar, precision=hi))
    s_idx = jnp.arange(T)[:, None]
    t_idx = jnp.arange(T)[None, :]

    def toeplitz(k, lag):
        m = jnp.where((lag >= 0)[:, :, None, None, None], k[jnp.clip(lag, 0, T - 1)], 0.0)
        return m.transpose(2, 0, 4, 1, 3).reshape(G, T * C, T * C)

    eye = jnp.eye(T * C, dtype=F32)
    tz = toeplitz(kern[0], t_idx - s_idx) + toeplitz(kern[1], s_idx - t_idx)

    def end_w(p_sel, bb):
        w = p_sel[..., None] * bb[None]
        w = jnp.concatenate([jnp.real(w), jnp.imag(w)], axis=2)
        return w.transpose(1, 0, 3, 2).reshape(G, T * C, 2 * P)

    w_end = jnp.concatenate([end_w(pw[0, :T][::-1], b_bar[0]), end_w(pw[1, :T], b_bar[1])], axis=-1)

    def in_w(p_sel, cm):
        z = cm[None] * p_sel[:, :, None, :]
        w = jnp.concatenate([jnp.real(z), -jnp.imag(z)], axis=-1)
        return w.transpose(1, 3, 0, 2).reshape(G, 2 * P, T * C)

    w_in_f = in_w(pw[0, 1:], cmat[0])
    w_in_b = in_w(pw[1, 1:][::-1], cmat[1])

    def coef(z):
        zero = jnp.zeros_like(jnp.real(z))
        rows = [[jnp.real(z), jnp.real(z)], [-jnp.imag(z), zero], [zero, jnp.imag(z)]]
        return jnp.concatenate(
            [jnp.concatenate(r, axis=-1).reshape(1, G * 2 * P) for r in rows], axis=0)

    d_tile = jnp.tile(d_skip.astype(F32).reshape(G, 1, C), (1, T, 1)).reshape(G, 1, T * C)
    del eye
    return (tz.astype(BF16), w_end.astype(BF16), w_in_f.astype(BF16), w_in_b.astype(BF16),
            coef(pw[0, T]), coef(pw[1, T]), d_tile)


def _s5(u_ctx, u_lat, tables):
    tz, w_end, w_in_f, w_in_b, cf, cb, d_tile = tables
    T, C, P = S5_CHUNK, S5_GROUP, S5_STATE
    B, Lc, W = u_ctx.shape
    L = u_lat.shape[1]
    G = W // C
    nc_ctx, nc_all = Lc // T, (Lc + L) // T
    cps = SUBLANES // B
    assert SUBLANES % B == 0 and nc_ctx % cps == 0 and (nc_all - nc_ctx) % cps == 0
    M = nc_all * B
    half = 2 * P
    u = jnp.concatenate([u_ctx, u_lat], axis=1)
    u = u.reshape(B, nc_all, T, G, C).transpose(3, 1, 0, 2, 4).reshape(G, M, T * C)

    ef, eb = pl.pallas_call(
        _s5_end_kernel,
        grid=(G // S5_GB,),
        in_specs=[
            pl.BlockSpec((S5_GB, M, T * C), lambda g: (g, 0, 0)),
            pl.BlockSpec((S5_GB, T * C, 2 * half), lambda g: (g, 0, 0)),
        ],
        out_specs=[pl.BlockSpec((M, S5_GB * half), lambda g: (0, g))] * 2,
        out_shape=[jax.ShapeDtypeStruct((M, G * half), F32)] * 2,
        compiler_params=_cparams(("parallel",)),
        name="s5_chunk_end_states",
    )(u, w_end)

    cw = 512
    hf, hb = pl.pallas_call(
        functools.partial(_s5_scan_kernel, batch=B, nc_ctx=nc_ctx, nc_all=nc_all),
        grid=(G * half // cw,),
        in_specs=[pl.BlockSpec((M, cw), lambda j: (0, j))] * 2 + [pl.BlockSpec((3, cw), lambda j: (0, j))] * 2,
        out_specs=[pl.BlockSpec((M, cw), lambda j: (0, j))] * 2,
        out_shape=[jax.ShapeDtypeStruct((M, G * half), F32)] * 2,
        compiler_params=_cparams(("parallel",)),
        name="s5_chunk_scan",
    )(ef, eb, cf, cb)

    y = pl.pallas_call(
        _s5_out_kernel,
        grid=(G // S5_GB,),
        in_specs=[
            pl.BlockSpec((S5_GB, M, T * C), lambda g: (g, 0, 0)),
            pl.BlockSpec((S5_GB, T * C, T * C), lambda g: (g, 0, 0)),
            pl.BlockSpec((M, S5_GB * half), lambda g: (0, g)),
            pl.BlockSpec((M, S5_GB * half), lambda g: (0, g)),
            pl.BlockSpec((S5_GB, half, T * C), lambda g: (g, 0, 0)),
            pl.BlockSpec((S5_GB, half, T * C), lambda g: (g, 0, 0)),
            pl.BlockSpec((S5_GB, 1, T * C), lambda g: (g, 0, 0)),
        ],
        out_specs=pl.BlockSpec((S5_GB, M, T * C), lambda g: (g, 0, 0)),
        out_shape=jax.ShapeDtypeStruct((G, M, T * C), BF16),
        compiler_params=_cparams(("parallel",)),
        name="s5_chunk_outputs",
    )(u, tz, hf, hb, w_in_f, w_in_b, d_tile)

    y = y.reshape(G, nc_all, B, T, C).transpose(2, 1, 3, 0, 4).reshape(B, Lc + L, W)
    return y[:, :Lc], y[:, Lc:]


def _glu_kernel(y_ref, w_ref, b_ref, o_ref):
    y = y_ref[0].astype(F32)
    z = jnp.dot(jax.nn.gelu(y).astype(BF16), w_ref[...], preferred_element_type=F32) + b_ref[...]
    n = o_ref.shape[-1]
    o_ref[0] = (z[:, :n] * jax.nn.sigmoid(z[:, n:])).astype(o_ref.dtype)


def _glu(y, w, b, *, tm):
    G, T, W = y.shape
    tm = min(tm, T)
    return pl.pallas_call(
        _glu_kernel,
        grid=(G, T // tm),
        in_specs=[
            pl.BlockSpec((1, tm, W), lambda g, i: (g, i, 0)),
            pl.BlockSpec((W, 2 * W), lambda g, i: (0, 0)),
            pl.BlockSpec((1, 2 * W), lambda g, i: (0, 0)),
        ],
        out_specs=pl.BlockSpec((1, tm, W), lambda g, i: (g, i, 0)),
        out_shape=jax.ShapeDtypeStruct((G, T, W), BF16),
        compiler_params=_cparams(("parallel", "parallel")),
        name="s5_glu",
    )(y, w, b.reshape(1, 2 * W))


def _out_res_kernel(*refs, widths):
    n = len(widths)
    lhs = refs[:n]
    w_ref, x_ref, gate_ref, g_ref, o_ref = refs[n:]
    y = None
    off = 0
    for r, k in zip(lhs, widths):
        part = jnp.dot(r[0], w_ref[off:off + k, :], preferred_element_type=F32)
        y = part if y is None else y + part
        off += k
    o_ref[0] = x_ref[0] + gate_ref[0] * (_rms(y) * g_ref[...])


def _out_res(parts, w, x, gate, g, *, tm):
    G, T, D = x.shape
    tm = min(tm, T)
    widths = tuple(p.shape[-1] for p in parts)
    K = sum(widths)
    in_specs = [pl.BlockSpec((1, tm, k), lambda b, i: (b, i, 0)) for k in widths]
    in_specs += [
        pl.BlockSpec((K, D), lambda b, i: (0, 0)),
        pl.BlockSpec((1, tm, D), lambda b, i: (b, i, 0)),
        pl.BlockSpec((1, 1, D), _vec_map(gate, 2)),
        pl.BlockSpec((1, D), lambda b, i: (0, 0)),
    ]
    return pl.pallas_call(
        functools.partial(_out_res_kernel, widths=widths),
        grid=(G, T // tm),
        in_specs=in_specs,
        out_specs=pl.BlockSpec((1, tm, D), lambda b, i: (b, i, 0)),
        out_shape=jax.ShapeDtypeStruct((G, T, D), F32),
        compiler_params=_cparams(("parallel", "parallel")),
        name="out_proj_residual",
    )(*parts, w, x, gate, g.reshape(1, D))


HALO = BF16_ROWS


def _ffn_kernel(x_ref, xp_ref, xn_ref, g2_ref, sh_ref, sc_ref, wa_ref, wv_ref, cwa_ref, cwv_ref,
                cba_ref, cbv_ref, wd_ref, gate_ref, g3_ref, o_ref, h_ref, acc_ref, ua_ref, uv_ref, *, tm):
    i = pl.program_id(1)
    f = pl.program_id(2)

    def normmod(x):
        h = _rms(x) * g2_ref[...]
        return h * (1.0 + sc_ref[0]) + sh_ref[0]

    @pl.when(f == 0)
    def _():
        zeros = jnp.zeros((HALO - SUBLANES, x_ref.shape[-1]), F32)
        hp = jnp.where(i == 0, 0.0, normmod(xp_ref[0]))
        hn = jnp.where(i == pl.num_programs(1) - 1, 0.0, normmod(xn_ref[0]))
        h_ref[0:HALO] = jnp.concatenate([zeros, hp], axis=0).astype(BF16)
        h_ref[HALO:HALO + tm] = normmod(x_ref[0]).astype(BF16)
        h_ref[HALO + tm:] = jnp.concatenate([hn, zeros], axis=0).astype(BF16)
        acc_ref[...] = jnp.zeros_like(acc_ref)

    h = h_ref[...]
    ua_ref[...] = jnp.dot(h, wa_ref[...], preferred_element_type=F32)
    uv_ref[...] = jnp.dot(h, wv_ref[...], preferred_element_type=F32)

    def conv(u_ref, cw_ref, cb_ref):
        return (u_ref[HALO - 1:HALO - 1 + tm] * cw_ref[0:1, :] + u_ref[HALO:HALO + tm] * cw_ref[1:2, :]
                + u_ref[HALO + 1:HALO + 1 + tm] * cw_ref[2:3, :] + cb_ref[...])

    a = conv(ua_ref, cwa_ref, cba_ref)
    v = conv(uv_ref, cwv_ref, cbv_ref)
    gated = (a * jax.nn.sigmoid(a) * v).astype(BF16)
    acc_ref[...] += jnp.dot(gated, wd_ref[...], preferred_element_type=F32)

    @pl.when(f == pl.num_programs(2) - 1)
    def _():
        o_ref[0] = x_ref[0] + gate_ref[0] * (_rms(acc_ref[...]) * g3_ref[...])


def _conv_ffn(x, g2, shift, scale, w_up, conv_w, conv_b, w_down, gate, g3, *, tm, tf):
    G, T, D = x.shape
    F = w_down.shape[0]
    tm = min(tm, T)
    nf = F // tf
    n_i = T // tm
    rb = tm // SUBLANES
    last_rb = T // SUBLANES - 1
    conv_b = conv_b.reshape(1, 2 * F)
    return pl.pallas_call(
        functools.partial(_ffn_kernel, tm=tm),
        grid=(G, n_i, nf),
        in_specs=[
            pl.BlockSpec((1, tm, D), lambda b, i, f: (b, i, 0)),
            pl.BlockSpec((1, SUBLANES, D), lambda b, i, f: (b, jnp.maximum(i * rb - 1, 0), 0)),
            pl.BlockSpec((1, SUBLANES, D), lambda b, i, f: (b, jnp.minimum((i + 1) * rb, last_rb), 0)),
            pl.BlockSpec((1, D), lambda b, i, f: (0, 0)),
            pl.BlockSpec((1, 1, D), _vec_map(shift, 3)),
            pl.BlockSpec((1, 1, D), _vec_map(scale, 3)),
            pl.BlockSpec((D, tf), lambda b, i, f: (0, f)),
            pl.BlockSpec((D, tf), lambda b, i, f: (0, nf + f)),
            pl.BlockSpec((3, tf), lambda b, i, f: (0, f)),
            pl.BlockSpec((3, tf), lambda b, i, f: (0, nf + f)),
            pl.BlockSpec((1, tf), lambda b, i, f: (0, f)),
            pl.BlockSpec((1, tf), lambda b, i, f: (0, nf + f)),
            pl.BlockSpec((tf, D), lambda b, i, f: (f, 0)),
            pl.BlockSpec((1, 1, D), _vec_map(gate, 3)),
            pl.BlockSpec((1, D), lambda b, i, f: (0, 0)),
        ],
        out_specs=pl.BlockSpec((1, tm, D), lambda b, i, f: (b, i, 0)),
        out_shape=jax.ShapeDtypeStruct((G, T, D), F32),
        scratch_shapes=[
            pltpu.VMEM((tm + 2 * HALO, D), BF16),
            pltpu.VMEM((tm, D), F32),
            pltpu.VMEM((tm + 2 * HALO, tf), F32),
            pltpu.VMEM((tm + 2 * HALO, tf), F32),
        ],
        compiler_params=_cparams(("parallel", "parallel", "arbitrary")),
        name="conv_ffn",
    )(x, x, x, g2.reshape(1, D), shift, scale, w_up, w_up, conv_w, conv_w, conv_b, conv_b, w_down,
      gate, g3.reshape(1, D))


GLA_BLOCKS = HGRN_CHUNK // SUBLANES


def _gla_operands(z, v, q, lb, rev):
    nb, rb = GLA_BLOCKS, SUBLANES
    order = list(range(nb))[::-1] if rev else list(range(nb))
    scan_of = {b: j for j, b in enumerate(order)}
    pos = lax.broadcasted_iota(jnp.int32, (rb, HGRN_DK), 0)
    if rev:
        pos = rb - 1 - pos

    def prev(x, k):
        return pltpu.roll(x, (rb - k) if rev else k, 0)

    def nxt(x, k):
        return pltpu.roll(x, k if rev else (rb - k), 0)

    def blocks(x):
        return [x[rb * b:rb * (b + 1)] for b in range(nb)]

    def rows(bl):
        return jnp.concatenate(bl, axis=0)

    f = lb + (1.0 - lb) * jax.nn.sigmoid(z)
    fb = blocks(f)
    kb = blocks(1.0 - f)

    p8, s8, bt = [], [], []
    first = rb - 1 if rev else 0
    for b in range(nb):
        x = y = fb[b]
        for k in (1, 2, 4):
            x = x * jnp.where(pos >= k, prev(x, k), 1.0)
            y = y * jnp.where(pos <= rb - 1 - k, nxt(y, k), 1.0)
        p8.append(x)
        s8.append(jnp.where(pos <= rb - 2, nxt(y, 1), 1.0))
        bt.append(y[first:first + 1, :])
    bts = [bt[order[j]] for j in range(nb)]
    ones = jnp.ones_like(bts[0])
    before = [ones]
    for j in range(1, nb):
        before.append(before[j - 1] * bts[j - 1])
    after = [ones] * nb
    for j in range(nb - 2, -1, -1):
        after[j] = after[j + 1] * bts[j + 1]
    ftot = before[nb - 1] * bts[nb - 1]

    kbase = [kb[b] * s8[b] for b in range(nb)]
    ops = {"v": v, "ftot": ftot,
           "kdec": rows([kbase[b] * after[scan_of[b]] for b in range(nb)]).astype(BF16)}
    if q is None:
        return ops

    qb = blocks(q)
    q8 = [qb[b] * p8[b] for b in range(nb)]
    ops["q_state"] = rows([q8[b] * before[scan_of[b]] for b in range(nb)]).astype(BF16)
    ops["q_far"] = rows(q8).astype(BF16)
    ops["q_near"] = q.astype(BF16)

    vb = blocks(v.astype(F32))
    kcols, vcols = [], []
    chain = {}
    for d in range(1, nb):
        for j in range(nb - d):
            chain[j] = kbase[order[j]] if d == 1 else chain[j] * bts[j + d - 1]
            kcols.append(chain[j])
            vcols.append(vb[order[j]])
    ops["k_far"] = rows(kcols).astype(BF16)
    ops["v_far"] = rows(vcols).astype(BF16)

    kvar = []
    cur = kb
    for d in range(rb):
        if d:
            cur = [cur[b] * nxt(fb[b], d) for b in range(nb)]
        kvar.append(rows(cur).astype(BF16))
    ops["k_near"] = jnp.concatenate(kvar, axis=0)
    ops["v_near"] = jnp.concatenate([v] * rb, axis=0)
    return ops


def _gla_first_dots(ops, st):
    nt = (((1,), (1,)), ((), ()))
    st_new = st * ops["ftot"] + lax.dot_general(ops["v"], ops["kdec"], (((0,), (0,)), ((), ())),
                                                preferred_element_type=F32)
    if "q_near" not in ops:
        return st_new, None
    o = lax.dot_general(ops["q_state"], st.astype(BF16), nt, preferred_element_type=F32)
    far = lax.dot_general(ops["q_far"], ops["k_far"], nt, preferred_element_type=F32)
    near = lax.dot_general(ops["q_near"], ops["k_near"], nt, preferred_element_type=F32)
    return st_new, (o, far, near)


def _gla_second_dots(ops, scores, masks):
    o, far, near = scores
    ma, mb = masks
    o = o + jnp.dot((far * ma).astype(BF16), ops["v_far"], preferred_element_type=F32)
    return o + jnp.dot((near * mb).astype(BF16), ops["v_near"], preferred_element_type=F32)


def _gla_mask_tables():
    T, nb, rb = HGRN_CHUNK, GLA_BLOCKS, SUBLANES
    t = np.arange(T)
    mas, mbs = [], []
    for rev in (False, True):
        sblk = (nb - 1 - t // rb) if rev else t // rb
        spos = (rb - 1 - t % rb) if rev else t % rb
        cols = [(d, j) for d in range(1, nb) for j in range(nb - d)]
        ma = np.zeros((T, len(cols) * rb), np.float32)
        for c, (d, j) in enumerate(cols):
            ma[sblk == j + d, c * rb:(c + 1) * rb] = 1.0
        mb = np.zeros((T, rb * T), np.float32)
        for d in range(rb):
            ok = (t[:, None] // rb == t[None, :] // rb) & (spos[:, None] - spos[None, :] == d)
            mb[:, d * T:(d + 1) * T] = ok
        mas.append(ma)
        mbs.append(mb)
    return jnp.asarray(np.stack(mas)), jnp.asarray(np.stack(mbs))


def _gla_kernel(q_ref, zf_ref, zb_ref, v_ref, g_ref, czf_ref, czb_ref, cv_ref, lb_ref, ng_ref,
                ma_ref, mb_ref, o_ref, oacc_ref, *, n_lat, n_ctx, hp):
    T, dk = HGRN_CHUNK, HGRN_DK
    jobs = [(h, rev) for h in range(hp) for rev in (False, True)]
    zero_state = jnp.zeros((dk, dk), F32)

    def cols(h):
        return slice(h * dk, (h + 1) * dk)

    def start(rev, i, n):
        return pl.multiple_of(((n - 1 - i) if rev else i) * T, T)

    def lb_row(h, rev):
        return lb_ref[int(rev):int(rev) + 1, cols(h)]

    def ctx_body(i, sts):
        ops = []
        for h, rev in jobs:
            r = start(rev, i, n_ctx)
            z = (czb_ref if rev else czf_ref)[0, pl.ds(r, T), cols(h)].astype(F32)
            ops.append(_gla_operands(z, cv_ref[0, pl.ds(r, T), cols(h)], None, lb_row(h, rev), rev))
        return tuple(_gla_first_dots(o, st)[0] for o, st in zip(ops, sts))

    def lat_body(i, sts, second_half):
        rs = [start(rev, i, n_lat) for _, rev in jobs]
        ops = []
        for (h, rev), r in zip(jobs, rs):
            qz = q_ref[0, pl.ds(r, T), cols(h)].astype(F32)
            z = (zb_ref if rev else zf_ref)[0, pl.ds(r, T), cols(h)].astype(F32)
            ops.append(_gla_operands(z, v_ref[0, pl.ds(r, T), cols(h)], qz * jax.nn.sigmoid(qz),
                                     lb_row(h, rev), rev))
        firsts = [_gla_first_dots(o, st) for o, st in zip(ops, sts)]
        outs = [_gla_second_dots(o, f[1], (ma_ref[int(rev)], mb_ref[int(rev)]))
                for o, f, (_, rev) in zip(ops, firsts, jobs)]
        for (h, _), r, o in zip(jobs, rs, outs):
            if second_half:
                o = o + oacc_ref[pl.ds(r, T), cols(h)]
                gz = g_ref[0, pl.ds(r, T), cols(h)].astype(F32)
                o = _rms(o) * ng_ref[...] * (gz * jax.nn.sigmoid(gz))
                o_ref[0, pl.ds(r, T), cols(h)] = o.astype(o_ref.dtype)
            else:
                oacc_ref[pl.ds(r, T), cols(h)] = o
        return tuple(f[0] for f in firsts)

    sts = lax.fori_loop(0, n_ctx, ctx_body, (zero_state,) * len(jobs))
    sts = lax.fori_loop(0, n_lat // 2, functools.partial(lat_body, second_half=False), sts)
    lax.fori_loop(n_lat // 2, n_lat, functools.partial(lat_body, second_half=True), sts)


GLA_HEADS_PER_STEP = 2


def _gla(p_lat, p_ctx, lb, norm_g, n_heads):
    B, L, _ = p_lat.shape
    Lc = p_ctx.shape[1]
    hp = GLA_HEADS_PER_STEP
    steps = n_heads // hp
    dv = HGRN_DK
    w = hp * dv

    def col(group):
        return lambda b, h: (b, 0, group * steps + h)

    lat_specs = [pl.BlockSpec((1, L, w), col(gidx)) for gidx in range(5)]
    ctx_specs = [pl.BlockSpec((1, Lc, w), col(gidx)) for gidx in (1, 2, 3)]
    ma, mb = _gla_mask_tables()
    n_lat = L // HGRN_CHUNK
    assert n_lat % 2 == 0 and n_heads % hp == 0
    return pl.pallas_call(
        functools.partial(_gla_kernel, n_lat=n_lat, n_ctx=Lc // HGRN_CHUNK, hp=hp),
        grid=(B, steps),
        in_specs=lat_specs + ctx_specs + [
            pl.BlockSpec((2, w), lambda b, h: (0, h)),
            pl.BlockSpec((1, dv), lambda b, h: (0, 0)),
            pl.BlockSpec(ma.shape, lambda b, h: (0, 0, 0)),
            pl.BlockSpec(mb.shape, lambda b, h: (0, 0, 0)),
        ],
        out_specs=pl.BlockSpec((1, L, w), lambda b, h: (b, 0, h)),
        out_shape=jax.ShapeDtypeStruct((B, L, n_heads * dv), BF16),
        scratch_shapes=[pltpu.VMEM((L, w), F32)],
        compiler_params=_cparams(("parallel", "parallel")),
        name="hgrn2_bidirectional",
    )(*([p_lat] * 5), *([p_ctx] * 3), lb, norm_g.reshape(1, dv), ma, mb)


def _rope_tables(n_tokens):
    rows = n_tokens // GRID_W
    row = jnp.repeat(jnp.arange(rows, dtype=F32), GRID_W)
    colp = jnp.tile(jnp.arange(GRID_W, dtype=F32), rows)
    inv = ROPE_THETA ** (-jnp.arange(AXIS_FREQS, dtype=F32) / AXIS_FREQS)
    ang = jnp.stack([row[:, None] * inv, colp[:, None] * inv], axis=1)
    cos, sin = jnp.cos(ang), jnp.sin(ang)
    zero = jnp.zeros_like(sin)
    full = lambda a, b: jnp.stack([a, b], axis=2).reshape(n_tokens, HEAD_DIM)
    return full(cos, cos), full(-sin, zero), full(zero, sin)


def _identity_rope(n_tokens):
    return (jnp.ones((n_tokens, HEAD_DIM), F32), jnp.zeros((n_tokens, HEAD_DIM), F32),
            jnp.zeros((n_tokens, HEAD_DIM), F32))


def kernel(x, c, ctx, c_ctx, mod_w, mod_b, norm_g, ab_w_in, ab_w_out, attn_q_norm, attn_k_norm,
           s5_lam_re, s5_lam_im, s5_log_dt, s5_b_re, s5_b_im, s5_c_re, s5_c_im, s5_d,
           s5_glu_w, s5_glu_b, c_w_in, c_w_out, hgrn_lb_logits, hgrn_norm,
           ffn_w_up, ffn_conv_w, ffn_conv_b, ffn_w_down):
    B, L, D = x.shape
    Lc = ctx.shape[1]
    depth = mod_w.shape[0]
    attn_w = ATTN_HEADS * HEAD_DIM
    kv_w = ATTN_KV_HEADS * HEAD_DIM
    qk_w = attn_w + kv_w

    lb_all = jnp.cumsum(jax.nn.softmax(hgrn_lb_logits.astype(F32), axis=0), axis=0)
    lb_all = lb_all - lb_all[:1]

    cvec = jnp.concatenate([c, c_ctx[None], jnp.zeros((SUBLANES - (B + 1) % SUBLANES, D), F32)], axis=0)
    mods = _modulation(cvec, mod_w, mod_b)

    rope_lat = _rope_tables(L)
    rope_ctx = _identity_rope(Lc)

    for l in range(depth):
        last = l == depth - 1
        m_lat = [mods[l, :B, k * D:(k + 1) * D][:, None, :] for k in range(6)]
        m_ctx = [mods[l, B:B + 1, k * D:(k + 1) * D][:, None, :] for k in range(6)]
        g = norm_g[l]
        w_up = ffn_w_up[l].astype(BF16)
        w_down = ffn_w_down[l].astype(BF16)
        if l % 2 == 0:
            e = l // 2
            w_in = ab_w_in[e].astype(BF16)
            gain = jnp.concatenate([jnp.tile(attn_q_norm[e] * (HEAD_DIM ** -0.5), ATTN_HEADS),
                                    jnp.tile(attn_k_norm[e], ATTN_KV_HEADS)]).reshape(1, qk_w)
            p_lat = _proj(x, g[0], m_lat[0], m_lat[1], w_in, tm=512, tn=qk_w, qk=(gain,) + rope_lat)
            p_ctx = _proj(ctx, g[0], m_ctx[0], m_ctx[1], w_in, tm=512, tn=qk_w, qk=(gain,) + rope_ctx)
            a_lat = _attention(p_lat, p_ctx, tq=128)
            tables = _s5_tables(s5_lam_re[e], s5_lam_im[e], s5_log_dt[e], s5_b_re[e], s5_b_im[e],
                                s5_c_re[e], s5_c_im[e], s5_d[e])
            u0 = qk_w + kv_w
            y_ctx, y_lat = _s5(p_ctx[..., u0:], p_lat[..., u0:], tables)
            glu_w = s5_glu_w[e].astype(BF16)
            w_out = ab_w_out[e].astype(BF16)
            s_lat = _glu(y_lat, glu_w, s5_glu_b[e], tm=512)
            x = _out_res([a_lat, s_lat], w_out, x, m_lat[2], g[1], tm=512)
            if not last:
                a_ctx = _attention(p_ctx, None, tq=128)
                s_ctx = _glu(y_ctx, glu_w, s5_glu_b[e], tm=512)
                ctx = _out_res([a_ctx, s_ctx], w_out, ctx, m_ctx[2], g[1], tm=512)
        else:
            o_idx = l // 2
            w_in = c_w_in[o_idx].astype(BF16)
            n_heads = c_w_out.shape[1] // HGRN_DK
            p_lat = _proj(x, g[0], m_lat[0], m_lat[1], w_in, tm=512, tn=1024)
            p_ctx = _proj(ctx, g[0], m_ctx[0], m_ctx[1], w_in, tm=512, tn=1024)
            o_lat = _gla(p_lat, p_ctx, lb_all[l], hgrn_norm[o_idx], n_heads)
            x = _out_res([o_lat], c_w_out[o_idx].astype(BF16), x, m_lat[2], g[1], tm=512)
            assert last, "context outputs of the HGRN2 mixer are only needed by a following layer"
        x = _conv_ffn(x, g[2], m_lat[3], m_lat[4], w_up, ffn_conv_w[l], ffn_conv_b[l], w_down,
                      m_lat[5], g[3], tm=512, tf=512)
        if not last:
            ctx = _conv_ffn(ctx, g[2], m_ctx[3], m_ctx[4], w_up, ffn_conv_w[l], ffn_conv_b[l], w_down,
                            m_ctx[5], g[3], tm=512, tf=512)
    return x
```

```python
import functools
import math

import jax
import jax.numpy as jnp
import numpy as np
from jax import lax
from jax.experimental import pallas as pl
from jax.experimental.pallas import tpu as pltpu

F32 = jnp.float32
BF16 = jnp.bfloat16
EPS = 1e-6

LANES = 128
SUBLANES = 8
BF16_ROWS = 16

HEAD_DIM = 128
ATTN_HEADS = 8
ATTN_KV_HEADS = 2
ATTN_GROUP = ATTN_HEADS // ATTN_KV_HEADS
GRID_W = 64
ROPE_THETA = 10000.0
AXIS_FREQS = HEAD_DIM // 4

S5_GROUP = 16
S5_STATE = 64
S5_CHUNK = 16

HGRN_DK = 128
HGRN_CHUNK = 64

VMEM_LIMIT = 56 * 1024 * 1024


def _cparams(sem):
    return pltpu.CompilerParams(dimension_semantics=sem, vmem_limit_bytes=VMEM_LIMIT)


def _vec_map(vec, nd_grid):
    shared = vec.shape[0] == 1
    if nd_grid == 2:
        return lambda g, i: (0 if shared else g, 0, 0)
    return lambda g, i, j: (0 if shared else g, 0, 0)


def _rms(x):
    return x * lax.rsqrt(jnp.mean(x * x, axis=-1, keepdims=True) + EPS)


def _mod_kernel(s_ref, w_ref, b_ref, o_ref):
    s = s_ref[...]
    a = s * jax.nn.sigmoid(s)
    o_ref[0] = jnp.dot(a.astype(BF16), w_ref[0].astype(BF16), preferred_element_type=F32) + b_ref[0]


def _modulation(cvec, mod_w, mod_b):
    depth, d, n = mod_w.shape
    rows = cvec.shape[0]
    tn = 1024
    return pl.pallas_call(
        _mod_kernel,
        grid=(depth, n // tn),
        in_specs=[
            pl.BlockSpec((rows, d), lambda l, j: (0, 0)),
            pl.BlockSpec((1, d, tn), lambda l, j: (l, 0, j)),
            pl.BlockSpec((1, 1, tn), lambda l, j: (l, 0, j)),
        ],
        out_specs=pl.BlockSpec((1, rows, tn), lambda l, j: (l, 0, j)),
        out_shape=jax.ShapeDtypeStruct((depth, rows, n), F32),
        compiler_params=_cparams(("parallel", "parallel")),
        name="modulation",
    )(cvec, mod_w, mod_b.reshape(depth, 1, n))


def _proj_kernel(x_ref, g_ref, sh_ref, sc_ref, w_ref, *rest, qk_heads):
    if qk_heads:
        gain_ref, cos_ref, s1_ref, s2_ref, o_ref, h_ref = rest
    else:
        o_ref, h_ref = rest
    j = pl.program_id(2)

    @pl.when(j == 0)
    def _():
        h = _rms(x_ref[0]) * g_ref[...]
        h = h * (1.0 + sc_ref[0]) + sh_ref[0]
        h_ref[...] = h.astype(BF16)

    acc = jnp.dot(h_ref[...], w_ref[...], preferred_element_type=F32)

    if not qk_heads:
        o_ref[0] = acc.astype(o_ref.dtype)
        return

    @pl.when(j == 0)
    def _():
        cos, s1, s2 = cos_ref[...], s1_ref[...], s2_ref[...]
        for hh in range(qk_heads):
            sl = slice(hh * HEAD_DIM, (hh + 1) * HEAD_DIM)
            y = _rms(acc[:, sl]) * gain_ref[:, sl]
            y = y * cos + pltpu.roll(y, HEAD_DIM - AXIS_FREQS, 1) * s1 + pltpu.roll(y, AXIS_FREQS, 1) * s2
            o_ref[0, :, sl] = y.astype(o_ref.dtype)

    @pl.when(j != 0)
    def _():
        o_ref[0] = acc.astype(o_ref.dtype)


def _proj(x, g, shift, scale, w, *, tm, tn, qk=None, out_dtype=BF16):
    G, T, D = x.shape
    N = w.shape[1]
    tm = min(tm, T)
    in_specs = [
        pl.BlockSpec((1, tm, D), lambda b, i, j: (b, i, 0)),
        pl.BlockSpec((1, D), lambda b, i, j: (0, 0)),
        pl.BlockSpec((1, 1, D), _vec_map(shift, 3)),
        pl.BlockSpec((1, 1, D), _vec_map(scale, 3)),
        pl.BlockSpec((D, tn), lambda b, i, j: (0, j)),
    ]
    args = [x, g.reshape(1, D), shift, scale, w]
    qk_heads = 0
    if qk is not None:
        gain, cos, s1, s2 = qk
        qk_heads = gain.shape[1] // HEAD_DIM
        assert qk_heads * HEAD_DIM == tn
        in_specs += [pl.BlockSpec((1, tn), lambda b, i, j: (0, 0))]
        in_specs += [pl.BlockSpec((tm, HEAD_DIM), lambda b, i, j: (i, 0))] * 3
        args += [gain, cos, s1, s2]
    return pl.pallas_call(
        functools.partial(_proj_kernel, qk_heads=qk_heads),
        grid=(G, T // tm, N // tn),
        in_specs=in_specs,
        out_specs=pl.BlockSpec((1, tm, tn), lambda b, i, j: (b, i, j)),
        out_shape=jax.ShapeDtypeStruct((G, T, N), out_dtype),
        scratch_shapes=[pltpu.VMEM((tm, D), BF16)],
        compiler_params=_cparams(("parallel", "parallel", "arbitrary")),
        name="norm_mod_proj",
    )(*args)


ATTN_KEY_CHUNK = 512


def _attn_kernel(q_ref, k_ref, v_ref, *rest, tq, has_prefix):
    if has_prefix:
        kp_ref, vp_ref, o_ref = rest
    else:
        (o_ref,) = rest
    nt = (((1,), (1,)), ((), ()))
    q = jnp.concatenate(
        [q_ref[0, :, g * HEAD_DIM:(g + 1) * HEAD_DIM] for g in range(ATTN_GROUP)], axis=0)
    n_keys = k_ref.shape[1]
    kc = min(ATTN_KEY_CHUNK, n_keys)
    chunks = [(kp_ref, vp_ref, 0, kp_ref.shape[1])] if has_prefix else []
    chunks += [(k_ref, v_ref, c * kc, kc) for c in range(n_keys // kc)]

    def scores(chunk):
        kr, _, start, size = chunk
        return lax.dot_general(q, kr[0, start:start + size, :], nt, preferred_element_type=F32)

    m = jnp.full((q.shape[0], 1), -1e30, F32)
    l = jnp.zeros((q.shape[0], 1), F32)
    o = jnp.zeros((q.shape[0], HEAD_DIM), F32)
    s_next = scores(chunks[0])
    for ci, (_, vr, start, size) in enumerate(chunks):
        s = s_next
        if ci + 1 < len(chunks):
            s_next = scores(chunks[ci + 1])
        m_new = jnp.maximum(m, jnp.max(s, axis=-1, keepdims=True))
        alpha = jnp.exp(m - m_new)
        p = jnp.exp(s - m_new)
        l = alpha * l + jnp.sum(p, axis=-1, keepdims=True)
        o = alpha * o + jnp.dot(p.astype(BF16), vr[0, start:start + size, :], preferred_element_type=F32)
        m = m_new
    o = o * (1.0 / l)
    for g in range(ATTN_GROUP):
        o_ref[0, :, g * HEAD_DIM:(g + 1) * HEAD_DIM] = o[g * tq:(g + 1) * tq].astype(o_ref.dtype)


def _attention(qkvu, prefix, *, tq):
    B, L, _ = qkvu.shape
    tq = min(tq, L)
    gw = ATTN_GROUP * HEAD_DIM
    k_blk = ATTN_HEADS
    v_blk = ATTN_HEADS + ATTN_KV_HEADS
    in_specs = [
        pl.BlockSpec((1, tq, gw), lambda b, h, i: (b, i, h)),
        pl.BlockSpec((1, L, HEAD_DIM), lambda b, h, i: (b, 0, k_blk + h)),
        pl.BlockSpec((1, L, HEAD_DIM), lambda b, h, i: (b, 0, v_blk + h)),
    ]
    args = [qkvu, qkvu, qkvu]
    if prefix is not None:
        Lp = prefix.shape[1]
        in_specs += [
            pl.BlockSpec((1, Lp, HEAD_DIM), lambda b, h, i: (b, 0, k_blk + h)),
            pl.BlockSpec((1, Lp, HEAD_DIM), lambda b, h, i: (b, 0, v_blk + h)),
        ]
        args += [prefix, prefix]
    return pl.pallas_call(
        functools.partial(_attn_kernel, tq=tq, has_prefix=prefix is not None),
        grid=(B, ATTN_KV_HEADS, L // tq),
        in_specs=in_specs,
        out_specs=pl.BlockSpec((1, tq, gw), lambda b, h, i: (b, i, h)),
        out_shape=jax.ShapeDtypeStruct((B, L, ATTN_HEADS * HEAD_DIM), BF16),
        compiler_params=_cparams(("parallel", "parallel", "arbitrary")),
        name="gqa_attention",
    )(*args)


S5_SLAB_GROUPS = LANES // S5_GROUP
S5_KD = S5_CHUNK * LANES
S5_PAIR_COLS = 4 * LANES


def _s5_end_kernel(u_ref, w_ref, e_ref):
    e_ref[...] = jnp.dot(u_ref[0], w_ref[0], preferred_element_type=F32)


def _s5_scan_kernel(e_ref, cf_ref, cb_ref, h_ref, *, batch, nc_ctx, nc_all):
    cps = SUBLANES // batch
    g_ctx, g_all = nc_ctx // cps, nc_all // cps

    def advance(hr, hi, er, ei, c_ref):
        ar, ai = c_ref[0:1, :], c_ref[1:2, :]
        return ar * hr - ai * hi + er, ar * hi + ai * hr + ei

    def sweep(r, state, col0, c_ref, order):
        e8 = e_ref[pl.ds(r, SUBLANES), col0:col0 + 2 * LANES]
        hr, hi = state
        out_r, out_i = [None] * cps, [None] * cps
        for c in order:
            out_r[c], out_i[c] = hr, hi
            rows = slice(c * batch, (c + 1) * batch)
            hr, hi = advance(hr, hi, e8[rows, :LANES], e8[rows, LANES:], c_ref)
        h_ref[pl.ds(r, SUBLANES), col0:col0 + LANES] = jnp.concatenate(out_r, axis=0).astype(h_ref.dtype)
        h_ref[pl.ds(r, SUBLANES), col0 + LANES:col0 + 2 * LANES] = (
            jnp.concatenate(out_i, axis=0).astype(h_ref.dtype))
        return hr, hi

    def body(i, carry):
        fwd, bwd = carry
        fwd = sweep(pl.multiple_of(i * SUBLANES, SUBLANES), fwd, 0, cf_ref, range(cps))
        gi = jnp.where(i < g_ctx, g_ctx - 1 - i, g_all - 1 - (i - g_ctx))
        bwd = sweep(pl.multiple_of(gi * SUBLANES, SUBLANES), bwd, 2 * LANES, cb_ref, reversed(range(cps)))
        return fwd, bwd

    zero = jnp.zeros((batch, LANES), F32)
    lax.fori_loop(0, g_all, body, ((zero, zero), (zero, zero)))


def _s5_out_kernel(u_ref, us_ref, tz_ref, h_ref, win_ref, d_ref, y_ref):
    y = jnp.dot(u_ref[0], tz_ref[0], preferred_element_type=F32)
    y += jnp.dot(h_ref[...].astype(BF16), win_ref[0], preferred_element_type=F32)
    y += us_ref[0].astype(F32) * d_ref[0]
    y_ref[0] = y.astype(y_ref.dtype)


def _s5_tables(lam_re, lam_im, log_dt, b_re, b_im, c_re, c_im, d_skip):
    T, C, P = S5_CHUNK, S5_GROUP, S5_STATE
    G = lam_re.shape[1]
    hi = lax.Precision.HIGHEST
    lam = lax.complex(jnp.minimum(lam_re.astype(F32), -1e-4), lam_im.astype(F32))
    dt = jnp.exp(log_dt.astype(F32))[..., None]
    lam_dt = lam * dt
    lam_bar = jnp.exp(lam_dt)
    bmat = lax.complex(b_re.astype(F32), b_im.astype(F32))
    b_bar = ((lam_bar - 1.0) / lam)[..., None] * bmat
    cmat = lax.complex(c_re.astype(F32), c_im.astype(F32))
    steps = jnp.arange(T + 1, dtype=F32)
    pw = jnp.exp(lam_dt[:, None] * steps[None, :, None, None])

    kern = jnp.real(jnp.einsum('zgcp,zjgp,zgpd->zjgcd', cmat, pw[:, :T], b_bar, precision=hi))
    s_idx = jnp.arange(T)[:, None]
    t_idx = jnp.arange(T)[None, :]

    def toeplitz(k, lag):
        m = jnp.where((lag >= 0)[:, :, None, None, None], k[jnp.clip(lag, 0, T - 1)], 0.0)
        return m.transpose(2, 0, 4, 1, 3).reshape(G, T * C, T * C)

    tz = toeplitz(kern[0], t_idx - s_idx) + toeplitz(kern[1], s_idx - t_idx)

    def end_w(p_sel, bb):
        w = p_sel[..., None] * bb[None]
        w = jnp.concatenate([jnp.real(w), jnp.imag(w)], axis=2)
        return w.transpose(1, 0, 3, 2).reshape(G, T * C, 2 * P)

    w_end = jnp.concatenate([end_w(pw[0, :T][::-1], b_bar[0]), end_w(pw[1, :T], b_bar[1])], axis=-1)

    def in_w(p_sel, cm):
        z = cm[None] * p_sel[:, :, None, :]
        w = jnp.concatenate([jnp.real(z), -jnp.imag(z)], axis=-1)
        return w.transpose(1, 3, 0, 2).reshape(G, 2 * P, T * C)

    w_in_f = in_w(pw[0, 1:], cmat[0])
    w_in_b = in_w(pw[1, 1:][::-1], cmat[1])

    def coef(z):
        return jnp.stack([jnp.real(z).reshape(G * P), jnp.imag(z).reshape(G * P)])

    sg = S5_SLAB_GROUPS
    ngb, npair = G // sg, sg // 2
    eye_g, eye_q, eye_h = (jnp.eye(n, dtype=F32) for n in (sg, npair, 2))
    tz_slab = jnp.einsum('agsdtc,gh->asgdthc', tz.reshape(ngb, sg, T, C, T, C), eye_g)
    tz_slab = tz_slab.reshape(ngb, S5_KD, S5_KD)
    end_slab = jnp.einsum('aqhsdcp,qr,hk->asqhdrckp', w_end.reshape(ngb, npair, 2, T, C, 4, P), eye_q, eye_h)
    end_slab = end_slab.reshape(ngb, S5_KD, npair * S5_PAIR_COLS)
    w_in = jnp.concatenate([w_in_f.reshape(G, 2, P, T, C), w_in_b.reshape(G, 2, P, T, C)], axis=1)
    in_slab = jnp.einsum('aqhcptd,qr,hk->aqchptrkd', w_in.reshape(ngb, npair, 2, 4, P, T, C), eye_q, eye_h)
    in_slab = in_slab.reshape(ngb, npair * S5_PAIR_COLS, S5_KD)
    d_slab = jnp.tile(d_skip.astype(F32).reshape(ngb, 1, LANES), (1, 1, T))
    return (tz_slab.astype(BF16), end_slab.astype(BF16), in_slab.astype(BF16),
            coef(pw[0, T]), coef(pw[1, T]), d_slab)


def _s5(u_ctx, u_lat, tables):
    tz, w_end, w_in, cf, cb, d_slab = tables
    T = S5_CHUNK
    B, Lc, W = u_ctx.shape
    L = u_lat.shape[1]
    ngb = W // LANES
    nc_ctx, nc_all = Lc // T, (Lc + L) // T
    cps = SUBLANES // B
    assert SUBLANES % B == 0 and nc_ctx % cps == 0 and (nc_all - nc_ctx) % cps == 0
    M = nc_all * B
    kd, sw = S5_KD, w_end.shape[-1]
    n_half = 4
    hw = kd // n_half
    u = jnp.concatenate([u_ctx, u_lat], axis=1)
    u = u.reshape(B, nc_all, T, ngb, LANES).transpose(3, 1, 0, 2, 4).reshape(ngb, M, kd)

    e = pl.pallas_call(
        _s5_end_kernel,
        grid=(ngb, n_half),
        in_specs=[
            pl.BlockSpec((1, M, kd), lambda g, j: (g, 0, 0)),
            pl.BlockSpec((1, kd, sw // n_half), lambda g, j: (g, 0, j)),
        ],
        out_specs=pl.BlockSpec((M, sw // n_half), lambda g, j: (0, g * n_half + j)),
        out_shape=jax.ShapeDtypeStruct((M, ngb * sw), F32),
        compiler_params=_cparams(("parallel", "parallel")),
        name="s5_chunk_end_states",
    )(u, w_end)

    h = pl.pallas_call(
        functools.partial(_s5_scan_kernel, batch=B, nc_ctx=nc_ctx, nc_all=nc_all),
        grid=(ngb * sw // S5_PAIR_COLS,),
        in_specs=[pl.BlockSpec((M, S5_PAIR_COLS), lambda j: (0, j))]
        + [pl.BlockSpec((2, LANES), lambda j: (0, j))] * 2,
        out_specs=pl.BlockSpec((M, S5_PAIR_COLS), lambda j: (0, j)),
        out_shape=jax.ShapeDtypeStruct((M, ngb * sw), F32),
        compiler_params=_cparams(("parallel",)),
        name="s5_chunk_scan",
    )(e, cf, cb)

    y = pl.pallas_call(
        _s5_out_kernel,
        grid=(ngb, n_half),
        in_specs=[
            pl.BlockSpec((1, M, kd), lambda g, j: (g, 0, 0)),
            pl.BlockSpec((1, M, hw), lambda g, j: (g, 0, j)),
            pl.BlockSpec((1, kd, hw), lambda g, j: (g, 0, j)),
            pl.BlockSpec((M, sw), lambda g, j: (0, g)),
            pl.BlockSpec((1, sw, hw), lambda g, j: (g, 0, j)),
            pl.BlockSpec((1, 1, hw), lambda g, j: (g, 0, j)),
        ],
        out_specs=pl.BlockSpec((1, M, hw), lambda g, j: (g, 0, j)),
        out_shape=jax.ShapeDtypeStruct((ngb, M, kd), BF16),
        compiler_params=_cparams(("parallel", "parallel")),
        name="s5_chunk_outputs",
    )(u, u, tz, h, w_in, d_slab)

    y = y.reshape(ngb, nc_all, B, T, LANES).transpose(2, 1, 3, 0, 4).reshape(B, Lc + L, W)
    return y[:, :Lc], y[:, Lc:]


def _glu_kernel(y_ref, w_ref, b_ref, o_ref):
    y = y_ref[0].astype(F32)
    z = jnp.dot(jax.nn.gelu(y).astype(BF16), w_ref[...], preferred_element_type=F32) + b_ref[...]
    n = o_ref.shape[-1]
    o_ref[0] = (z[:, :n] * jax.nn.sigmoid(z[:, n:])).astype(o_ref.dtype)


def _glu(y, w, b, *, tm):
    G, T, W = y.shape
    tm = min(tm, T)
    return pl.pallas_call(
        _glu_kernel,
        grid=(G, T // tm),
        in_specs=[
            pl.BlockSpec((1, tm, W), lambda g, i: (g, i, 0)),
            pl.BlockSpec((W, 2 * W), lambda g, i: (0, 0)),
            pl.BlockSpec((1, 2 * W), lambda g, i: (0, 0)),
        ],
        out_specs=pl.BlockSpec((1, tm, W), lambda g, i: (g, i, 0)),
        out_shape=jax.ShapeDtypeStruct((G, T, W), BF16),
        compiler_params=_cparams(("parallel", "parallel")),
        name="s5_glu",
    )(y, w, b.reshape(1, 2 * W))


def _out_res_kernel(*refs, widths):
    n = len(widths)
    lhs = refs[:n]
    w_ref, x_ref, gate_ref, g_ref, o_ref = refs[n:]
    y = None
    off = 0
    for r, k in zip(lhs, widths):
        part = jnp.dot(r[0], w_ref[off:off + k, :], preferred_element_type=F32)
        y = part if y is None else y + part
        off += k
    o_ref[0] = x_ref[0] + gate_ref[0] * (_rms(y) * g_ref[...])


def _out_res(parts, w, x, gate, g, *, tm):
    G, T, D = x.shape
    tm = min(tm, T)
    widths = tuple(p.shape[-1] for p in parts)
    K = sum(widths)
    in_specs = [pl.BlockSpec((1, tm, k), lambda b, i: (b, i, 0)) for k in widths]
    in_specs += [
        pl.BlockSpec((K, D), lambda b, i: (0, 0)),
        pl.BlockSpec((1, tm, D), lambda b, i: (b, i, 0)),
        pl.BlockSpec((1, 1, D), _vec_map(gate, 2)),
        pl.BlockSpec((1, D), lambda b, i: (0, 0)),
    ]
    return pl.pallas_call(
        functools.partial(_out_res_kernel, widths=widths),
        grid=(G, T // tm),
        in_specs=in_specs,
        out_specs=pl.BlockSpec((1, tm, D), lambda b, i: (b, i, 0)),
        out_shape=jax.ShapeDtypeStruct((G, T, D), F32),
        compiler_params=_cparams(("parallel", "parallel")),
        name="out_proj_residual",
    )(*parts, w, x, gate, g.reshape(1, D))


HALO = BF16_ROWS


def _ffn_kernel(x_ref, xp_ref, xn_ref, g2_ref, sh_ref, sc_ref, wa_ref, wv_ref, cwa_ref, cwv_ref,
                cba_ref, cbv_ref, wd_ref, gate_ref, g3_ref, o_ref, h_ref, acc_ref, ua_ref, uv_ref, *, tm):
    i = pl.program_id(1)
    f = pl.program_id(2)

    def normmod(x):
        h = _rms(x) * g2_ref[...]
        return h * (1.0 + sc_ref[0]) + sh_ref[0]

    @pl.when(f == 0)
    def _():
        zeros = jnp.zeros((HALO - SUBLANES, x_ref.shape[-1]), F32)
        hp = jnp.where(i == 0, 0.0, normmod(xp_ref[0]))
        hn = jnp.where(i == pl.num_programs(1) - 1, 0.0, normmod(xn_ref[0]))
        h_ref[0:HALO] = jnp.concatenate([zeros, hp], axis=0).astype(BF16)
        h_ref[HALO:HALO + tm] = normmod(x_ref[0]).astype(BF16)
        h_ref[HALO + tm:] = jnp.concatenate([hn, zeros], axis=0).astype(BF16)
        acc_ref[...] = jnp.zeros_like(acc_ref)

    h = h_ref[...]
    ua_ref[...] = jnp.dot(h, wa_ref[...], preferred_element_type=F32)
    uv_ref[...] = jnp.dot(h, wv_ref[...], preferred_element_type=F32)

    def conv(u_ref, cw_ref, cb_ref):
        return (u_ref[HALO - 1:HALO - 1 + tm] * cw_ref[0:1, :] + u_ref[HALO:HALO + tm] * cw_ref[1:2, :]
                + u_ref[HALO + 1:HALO + 1 + tm] * cw_ref[2:3, :] + cb_ref[...])

    a = conv(ua_ref, cwa_ref, cba_ref)
    v = conv(uv_ref, cwv_ref, cbv_ref)
    gated = (a * jax.nn.sigmoid(a) * v).astype(BF16)
    acc_ref[...] += jnp.dot(gated, wd_ref[...], preferred_element_type=F32)

    @pl.when(f == pl.num_programs(2) - 1)
    def _():
        o_ref[0] = x_ref[0] + gate_ref[0] * (_rms(acc_ref[...]) * g3_ref[...])


def _conv_ffn(x, g2, shift, scale, w_up, conv_w, conv_b, w_down, gate, g3, *, tm, tf):
    G, T, D = x.shape
    F = w_down.shape[0]
    tm = min(tm, T)
    nf = F // tf
    n_i = T // tm
    rb = tm // SUBLANES
    last_rb = T // SUBLANES - 1
    conv_b = conv_b.reshape(1, 2 * F)
    return pl.pallas_call(
        functools.partial(_ffn_kernel, tm=tm),
        grid=(G, n_i, nf),
        in_specs=[
            pl.BlockSpec((1, tm, D), lambda b, i, f: (b, i, 0)),
            pl.BlockSpec((1, SUBLANES, D), lambda b, i, f: (b, jnp.maximum(i * rb - 1, 0), 0)),
            pl.BlockSpec((1, SUBLANES, D), lambda b, i, f: (b, jnp.minimum((i + 1) * rb, last_rb), 0)),
            pl.BlockSpec((1, D), lambda b, i, f: (0, 0)),
            pl.BlockSpec((1, 1, D), _vec_map(shift, 3)),
            pl.BlockSpec((1, 1, D), _vec_map(scale, 3)),
            pl.BlockSpec((D, tf), lambda b, i, f: (0, f)),
            pl.BlockSpec((D, tf), lambda b, i, f: (0, nf + f)),
            pl.BlockSpec((3, tf), lambda b, i, f: (0, f)),
            pl.BlockSpec((3, tf), lambda b, i, f: (0, nf + f)),
            pl.BlockSpec((1, tf), lambda b, i, f: (0, f)),
            pl.BlockSpec((1, tf), lambda b, i, f: (0, nf + f)),
            pl.BlockSpec((tf, D), lambda b, i, f: (f, 0)),
            pl.BlockSpec((1, 1, D), _vec_map(gate, 3)),
            pl.BlockSpec((1, D), lambda b, i, f: (0, 0)),
        ],
        out_specs=pl.BlockSpec((1, tm, D), lambda b, i, f: (b, i, 0)),
        out_shape=jax.ShapeDtypeStruct((G, T, D), F32),
        scratch_shapes=[
            pltpu.VMEM((tm + 2 * HALO, D), BF16),
            pltpu.VMEM((tm, D), F32),
            pltpu.VMEM((tm + 2 * HALO, tf), F32),
            pltpu.VMEM((tm + 2 * HALO, tf), F32),
        ],
        compiler_params=_cparams(("parallel", "parallel", "arbitrary")),
        name="conv_ffn",
    )(x, x, x, g2.reshape(1, D), shift, scale, w_up, w_up, conv_w, conv_w, conv_b, conv_b, w_down,
      gate, g3.reshape(1, D))


GLA_BLOCKS = HGRN_CHUNK // SUBLANES


def _gla_operands(z, v, q, lb, rev):
    nb, rb = GLA_BLOCKS, SUBLANES
    order = list(range(nb))[::-1] if rev else list(range(nb))
    scan_of = {b: j for j, b in enumerate(order)}
    pos = lax.broadcasted_iota(jnp.int32, (rb, HGRN_DK), 0)
    if rev:
        pos = rb - 1 - pos

    def prev(x, k):
        return pltpu.roll(x, (rb - k) if rev else k, 0)

    def nxt(x, k):
        return pltpu.roll(x, k if rev else (rb - k), 0)

    def blocks(x):
        return [x[rb * b:rb * (b + 1)] for b in range(nb)]

    def rows(bl):
        return jnp.concatenate(bl, axis=0)

    f = lb + (1.0 - lb) * jax.nn.sigmoid(z)
    fb = blocks(f)
    kb = blocks(1.0 - f)

    p8, s8, bt = [], [], []
    first = rb - 1 if rev else 0
    for b in range(nb):
        x = y = fb[b]
        for k in (1, 2, 4):
            x = x * jnp.where(pos >= k, prev(x, k), 1.0)
            y = y * jnp.where(pos <= rb - 1 - k, nxt(y, k), 1.0)
        p8.append(x)
        s8.append(jnp.where(pos <= rb - 2, nxt(y, 1), 1.0))
        bt.append(y[first:first + 1, :])
    bts = [bt[order[j]] for j in range(nb)]
    ones = jnp.ones_like(bts[0])
    before = [ones]
    for j in range(1, nb):
        before.append(before[j - 1] * bts[j - 1])
    after = [ones] * nb
    for j in range(nb - 2, -1, -1):
        after[j] = after[j + 1] * bts[j + 1]
    ftot = before[nb - 1] * bts[nb - 1]

    kbase = [kb[b] * s8[b] for b in range(nb)]
    ops = {"v": v, "ftot": ftot,
           "kdec": rows([kbase[b] * after[scan_of[b]] for b in range(nb)]).astype(BF16)}
    if q is None:
        return ops

    qb = blocks(q)
    q8 = [qb[b] * p8[b] for b in range(nb)]
    ops["q_state"] = rows([q8[b] * before[scan_of[b]] for b in range(nb)]).astype(BF16)
    ops["q_far"] = rows(q8).astype(BF16)
    ops["q_near"] = q.astype(BF16)

    vb = blocks(v.astype(F32))
    kcols, vcols = [], []
    chain = {}
    for d in range(1, nb):
        for j in range(nb - d):
            chain[j] = kbase[order[j]] if d == 1 else chain[j] * bts[j + d - 1]
            kcols.append(chain[j])
            vcols.append(vb[order[j]])
    ops["k_far"] = rows(kcols).astype(BF16)
    ops["v_far"] = rows(vcols).astype(BF16)

    kvar = []
    cur = kb
    for d in range(rb):
        if d:
            cur = [cur[b] * nxt(fb[b], d) for b in range(nb)]
        kvar.append(rows(cur).astype(BF16))
    ops["k_near"] = jnp.concatenate(kvar, axis=0)
    ops["v_near"] = jnp.concatenate([v] * rb, axis=0)
    return ops


def _gla_first_dots(ops, st):
    nt = (((1,), (1,)), ((), ()))
    st_new = st * ops["ftot"] + lax.dot_general(ops["v"], ops["kdec"], (((0,), (0,)), ((), ())),
                                                preferred_element_type=F32)
    if "q_near" not in ops:
        return st_new, None
    o = lax.dot_general(ops["q_state"], st.astype(BF16), nt, preferred_element_type=F32)
    far = lax.dot_general(ops["q_far"], ops["k_far"], nt, preferred_element_type=F32)
    near = lax.dot_general(ops["q_near"], ops["k_near"], nt, preferred_element_type=F32)
    return st_new, (o, far, near)


def _gla_second_dots(ops, scores, masks):
    o, far, near = scores
    ma, mb = masks
    o = o + jnp.dot((far * ma).astype(BF16), ops["v_far"], preferred_element_type=F32)
    return o + jnp.dot((near * mb).astype(BF16), ops["v_near"], preferred_element_type=F32)


def _gla_mask_tables():
    T, nb, rb = HGRN_CHUNK, GLA_BLOCKS, SUBLANES
    t = np.arange(T)
    mas, mbs = [], []
    for rev in (False, True):
        sblk = (nb - 1 - t // rb) if rev else t // rb
        spos = (rb - 1 - t % rb) if rev else t % rb
        cols = [(d, j) for d in range(1, nb) for j in range(nb - d)]
        ma = np.zeros((T, len(cols) * rb), np.float32)
        for c, (d, j) in enumerate(cols):
            ma[sblk == j + d, c * rb:(c + 1) * rb] = 1.0
        mb = np.zeros((T, rb * T), np.float32)
        for d in range(rb):
            ok = (t[:, None] // rb == t[None, :] // rb) & (spos[:, None] - spos[None, :] == d)
            mb[:, d * T:(d + 1) * T] = ok
        mas.append(ma)
        mbs.append(mb)
    return jnp.asarray(np.stack(mas)), jnp.asarray(np.stack(mbs))


def _gla_kernel(q_ref, zf_ref, zb_ref, v_ref, g_ref, czf_ref, czb_ref, cv_ref, lb_ref, ng_ref,
                ma_ref, mb_ref, o_ref, oacc_ref, *, n_lat, n_ctx, hp):
    T, dk = HGRN_CHUNK, HGRN_DK
    jobs = [(h, rev) for h in range(hp) for rev in (False, True)]
    zero_state = jnp.zeros((dk, dk), F32)

    def cols(h):
        return slice(h * dk, (h + 1) * dk)

    def start(rev, i, n):
        return pl.multiple_of(((n - 1 - i) if rev else i) * T, T)

    def lb_row(h, rev):
        return lb_ref[int(rev):int(rev) + 1, cols(h)]

    def ctx_body(i, sts):
        ops = []
        for h, rev in jobs:
            r = start(rev, i, n_ctx)
            z = (czb_ref if rev else czf_ref)[0, pl.ds(r, T), cols(h)].astype(F32)
            ops.append(_gla_operands(z, cv_ref[0, pl.ds(r, T), cols(h)], None, lb_row(h, rev), rev))
        return tuple(_gla_first_dots(o, st)[0] for o, st in zip(ops, sts))

    def lat_body(i, sts, second_half):
        rs = [start(rev, i, n_lat) for _, rev in jobs]
        ops = []
        for (h, rev), r in zip(jobs, rs):
            qz = q_ref[0, pl.ds(r, T), cols(h)].astype(F32)
            z = (zb_ref if rev else zf_ref)[0, pl.ds(r, T), cols(h)].astype(F32)
            ops.append(_gla_operands(z, v_ref[0, pl.ds(r, T), cols(h)], qz * jax.nn.sigmoid(qz),
                                     lb_row(h, rev), rev))
        firsts = [_gla_first_dots(o, st) for o, st in zip(ops, sts)]
        outs = [_gla_second_dots(o, f[1], (ma_ref[int(rev)], mb_ref[int(rev)]))
                for o, f, (_, rev) in zip(ops, firsts, jobs)]
        for (h, _), r, o in zip(jobs, rs, outs):
            if second_half:
                o = o + oacc_ref[pl.ds(r, T), cols(h)]
                gz = g_ref[0, pl.ds(r, T), cols(h)].astype(F32)
                o = _rms(o) * ng_ref[...] * (gz * jax.nn.sigmoid(gz))
                o_ref[0, pl.ds(r, T), cols(h)] = o.astype(o_ref.dtype)
            else:
                oacc_ref[pl.ds(r, T), cols(h)] = o
        return tuple(f[0] for f in firsts)

    sts = lax.fori_loop(0, n_ctx, ctx_body, (zero_state,) * len(jobs))
    sts = lax.fori_loop(0, n_lat // 2, functools.partial(lat_body, second_half=False), sts)
    lax.fori_loop(n_lat // 2, n_lat, functools.partial(lat_body, second_half=True), sts)


GLA_HEADS_PER_STEP = 2


def _gla(p_lat, p_ctx, lb, norm_g, n_heads):
    B, L, _ = p_lat.shape
    Lc = p_ctx.shape[1]
    hp = GLA_HEADS_PER_STEP
    steps = n_heads // hp
    dv = HGRN_DK
    w = hp * dv

    def col(group):
        return lambda b, h: (b, 0, group * steps + h)

    lat_specs = [pl.BlockSpec((1, L, w), col(gidx)) for gidx in range(5)]
    ctx_specs = [pl.BlockSpec((1, Lc, w), col(gidx)) for gidx in (1, 2, 3)]
    ma, mb = _gla_mask_tables()
    n_lat = L // HGRN_CHUNK
    assert n_lat % 2 == 0 and n_heads % hp == 0
    return pl.pallas_call(
        functools.partial(_gla_kernel, n_lat=n_lat, n_ctx=Lc // HGRN_CHUNK, hp=hp),
        grid=(B, steps),
        in_specs=lat_specs + ctx_specs + [
            pl.BlockSpec((2, w), lambda b, h: (0, h)),
            pl.BlockSpec((1, dv), lambda b, h: (0, 0)),
            pl.BlockSpec(ma.shape, lambda b, h: (0, 0, 0)),
            pl.BlockSpec(mb.shape, lambda b, h: (0, 0, 0)),
        ],
        out_specs=pl.BlockSpec((1, L, w), lambda b, h: (b, 0, h)),
        out_shape=jax.ShapeDtypeStruct((B, L, n_heads * dv), BF16),
        scratch_shapes=[pltpu.VMEM((L, w), F32)],
        compiler_params=_cparams(("parallel", "parallel")),
        name="hgrn2_bidirectional",
    )(*([p_lat] * 5), *([p_ctx] * 3), lb, norm_g.reshape(1, dv), ma, mb)


def _rope_tables(n_tokens):
    rows = n_tokens // GRID_W
    row = jnp.repeat(jnp.arange(rows, dtype=F32), GRID_W)
    colp = jnp.tile(jnp.arange(GRID_W, dtype=F32), rows)
    inv = ROPE_THETA ** (-jnp.arange(AXIS_FREQS, dtype=F32) / AXIS_FREQS)
    ang = jnp.stack([row[:, None] * inv, colp[:, None] * inv], axis=1)
    cos, sin = jnp.cos(ang), jnp.sin(ang)
    zero = jnp.zeros_like(sin)
    full = lambda a, b: jnp.stack([a, b], axis=2).reshape(n_tokens, HEAD_DIM)
    return full(cos, cos), full(-sin, zero), full(zero, sin)


def _identity_rope(n_tokens):
    return (jnp.ones((n_tokens, HEAD_DIM), F32), jnp.zeros((n_tokens, HEAD_DIM), F32),
            jnp.zeros((n_tokens, HEAD_DIM), F32))


def kernel(x, c, ctx, c_ctx, mod_w, mod_b, norm_g, ab_w_in, ab_w_out, attn_q_norm, attn_k_norm,
           s5_lam_re, s5_lam_im, s5_log_dt, s5_b_re, s5_b_im, s5_c_re, s5_c_im, s5_d,
           s5_glu_w, s5_glu_b, c_w_in, c_w_out, hgrn_lb_logits, hgrn_norm,
           ffn_w_up, ffn_conv_w, ffn_conv_b, ffn_w_down):
    B, L, D = x.shape
    Lc = ctx.shape[1]
    depth = mod_w.shape[0]
    attn_w = ATTN_HEADS * HEAD_DIM
    kv_w = ATTN_KV_HEADS * HEAD_DIM
    qk_w = attn_w + kv_w

    lb_all = jnp.cumsum(jax.nn.softmax(hgrn_lb_logits.astype(F32), axis=0), axis=0)
    lb_all = lb_all - lb_all[:1]

    cvec = jnp.concatenate([c, c_ctx[None], jnp.zeros((SUBLANES - (B + 1) % SUBLANES, D), F32)], axis=0)
    mods = _modulation(cvec, mod_w, mod_b)

    rope_lat = _rope_tables(L)
    rope_ctx = _identity_rope(Lc)

    for l in range(depth):
        last = l == depth - 1
        m_lat = [mods[l, :B, k * D:(k + 1) * D][:, None, :] for k in range(6)]
        m_ctx = [mods[l, B:B + 1, k * D:(k + 1) * D][:, None, :] for k in range(6)]
        g = norm_g[l]
        w_up = ffn_w_up[l].astype(BF16)
        w_down = ffn_w_down[l].astype(BF16)
        if l % 2 == 0:
            e = l // 2
            w_in = ab_w_in[e].astype(BF16)
            gain = jnp.concatenate([jnp.tile(attn_q_norm[e] * (HEAD_DIM ** -0.5), ATTN_HEADS),
                                    jnp.tile(attn_k_norm[e], ATTN_KV_HEADS)]).reshape(1, qk_w)
            p_lat = _proj(x, g[0], m_lat[0], m_lat[1], w_in, tm=512, tn=qk_w, qk=(gain,) + rope_lat)
            p_ctx = _proj(ctx, g[0], m_ctx[0], m_ctx[1], w_in, tm=512, tn=qk_w, qk=(gain,) + rope_ctx)
            a_lat = _attention(p_lat, p_ctx, tq=128)
            tables = _s5_tables(s5_lam_re[e], s5_lam_im[e], s5_log_dt[e], s5_b_re[e], s5_b_im[e],
                                s5_c_re[e], s5_c_im[e], s5_d[e])
            u0 = qk_w + kv_w
            y_ctx, y_lat = _s5(p_ctx[..., u0:], p_lat[..., u0:], tables)
            glu_w = s5_glu_w[e].astype(BF16)
            w_out = ab_w_out[e].astype(BF16)
            s_lat = _glu(y_lat, glu_w, s5_glu_b[e], tm=512)
            x = _out_res([a_lat, s_lat], w_out, x, m_lat[2], g[1], tm=512)
            if not last:
                a_ctx = _attention(p_ctx, None, tq=128)
                s_ctx = _glu(y_ctx, glu_w, s5_glu_b[e], tm=512)
                ctx = _out_res([a_ctx, s_ctx], w_out, ctx, m_ctx[2], g[1], tm=512)
        else:
            o_idx = l // 2
            w_in = c_w_in[o_idx].astype(BF16)
            n_heads = c_w_out.shape[1] // HGRN_DK
            p_lat = _proj(x, g[0], m_lat[0], m_lat[1], w_in, tm=512, tn=1024)
            p_ctx = _proj(ctx, g[0], m_ctx[0], m_ctx[1], w_in, tm=512, tn=1024)
            o_lat = _gla(p_lat, p_ctx, lb_all[l], hgrn_norm[o_idx], n_heads)
            x = _out_res([o_lat], c_w_out[o_idx].astype(BF16), x, m_lat[2], g[1], tm=512)
            assert last, "context outputs of the HGRN2 mixer are only needed by a following layer"
        x = _conv_ffn(x, g[2], m_lat[3], m_lat[4], w_up, ffn_conv_w[l], ffn_conv_b[l], w_down,
                      m_lat[5], g[3], tm=512, tf=512)
        if not last:
            ctx = _conv_ffn(ctx, g[2], m_ctx[3], m_ctx[4], w_up, ffn_conv_w[l], ffn_conv_b[l], w_down,
                            m_ctx[5], g[3], tm=512, tf=512)
    return x
```

```python
import functools
import math

import jax
import jax.numpy as jnp
import numpy as np
from jax import lax
from jax.experimental import pallas as pl
from jax.experimental.pallas import tpu as pltpu

F32 = jnp.float32
BF16 = jnp.bfloat16
EPS = 1e-6

LANES = 128
SUBLANES = 8
BF16_ROWS = 16

HEAD_DIM = 128
ATTN_HEADS = 8
ATTN_KV_HEADS = 2
ATTN_GROUP = ATTN_HEADS // ATTN_KV_HEADS
GRID_W = 64
ROPE_THETA = 10000.0
AXIS_FREQS = HEAD_DIM // 4

S5_GROUP = 16
S5_STATE = 64
S5_CHUNK = 16

HGRN_DK = 128
HGRN_CHUNK = 64

VMEM_LIMIT = 56 * 1024 * 1024


def _cparams(sem):
    return pltpu.CompilerParams(dimension_semantics=sem, vmem_limit_bytes=VMEM_LIMIT)


def _vec_map(vec, nd_grid):
    shared = vec.shape[0] == 1
    if nd_grid == 2:
        return lambda g, i: (0 if shared else g, 0, 0)
    return lambda g, i, j: (0 if shared else g, 0, 0)


def _rms(x):
    return x * lax.rsqrt(jnp.mean(x * x, axis=-1, keepdims=True) + EPS)


def _mod_kernel(s_ref, w_ref, b_ref, o_ref):
    s = s_ref[...]
    a = s * jax.nn.sigmoid(s)
    o_ref[0] = jnp.dot(a.astype(BF16), w_ref[0].astype(BF16), preferred_element_type=F32) + b_ref[0]


def _modulation(cvec, mod_w, mod_b):
    depth, d, n = mod_w.shape
    rows = cvec.shape[0]
    tn = 1024
    return pl.pallas_call(
        _mod_kernel,
        grid=(depth, n // tn),
        in_specs=[
            pl.BlockSpec((rows, d), lambda l, j: (0, 0)),
            pl.BlockSpec((1, d, tn), lambda l, j: (l, 0, j)),
            pl.BlockSpec((1, 1, tn), lambda l, j: (l, 0, j)),
        ],
        out_specs=pl.BlockSpec((1, rows, tn), lambda l, j: (l, 0, j)),
        out_shape=jax.ShapeDtypeStruct((depth, rows, n), F32),
        compiler_params=_cparams(("parallel", "parallel")),
        name="modulation",
    )(cvec, mod_w, mod_b.reshape(depth, 1, n))


def _proj_kernel(x_ref, g_ref, sh_ref, sc_ref, w_ref, *rest, qk_heads):
    if qk_heads:
        gain_ref, cos_ref, s1_ref, s2_ref, o_ref, h_ref = rest
    else:
        o_ref, h_ref = rest
    j = pl.program_id(2)

    @pl.when(j == 0)
    def _():
        h = _rms(x_ref[0]) * g_ref[...]
        h = h * (1.0 + sc_ref[0]) + sh_ref[0]
        h_ref[...] = h.astype(BF16)

    acc = jnp.dot(h_ref[...], w_ref[...], preferred_element_type=F32)

    if not qk_heads:
        o_ref[0] = acc.astype(o_ref.dtype)
        return

    @pl.when(j == 0)
    def _():
        cos, s1, s2 = cos_ref[...], s1_ref[...], s2_ref[...]
        for hh in range(qk_heads):
            sl = slice(hh * HEAD_DIM, (hh + 1) * HEAD_DIM)
            y = _rms(acc[:, sl]) * gain_ref[:, sl]
            y = y * cos + pltpu.roll(y, HEAD_DIM - AXIS_FREQS, 1) * s1 + pltpu.roll(y, AXIS_FREQS, 1) * s2
            o_ref[0, :, sl] = y.astype(o_ref.dtype)

    @pl.when(j != 0)
    def _():
        o_ref[0] = acc.astype(o_ref.dtype)


def _proj(x, g, shift, scale, w, *, tm, tn, qk=None, out_dtype=BF16):
    G, T, D = x.shape
    N = w.shape[1]
    tm = min(tm, T)
    in_specs = [
        pl.BlockSpec((1, tm, D), lambda b, i, j: (b, i, 0)),
        pl.BlockSpec((1, D), lambda b, i, j: (0, 0)),
        pl.BlockSpec((1, 1, D), _vec_map(shift, 3)),
        pl.BlockSpec((1, 1, D), _vec_map(scale, 3)),
        pl.BlockSpec((D, tn), lambda b, i, j: (0, j)),
    ]
    args = [x, g.reshape(1, D), shift, scale, w]
    qk_heads = 0
    if qk is not None:
        gain, cos, s1, s2 = qk
        qk_heads = gain.shape[1] // HEAD_DIM
        assert qk_heads * HEAD_DIM == tn
        in_specs += [pl.BlockSpec((1, tn), lambda b, i, j: (0, 0))]
        in_specs += [pl.BlockSpec((tm, HEAD_DIM), lambda b, i, j: (i, 0))] * 3
        args += [gain, cos, s1, s2]
    return pl.pallas_call(
        functools.partial(_proj_kernel, qk_heads=qk_heads),
        grid=(G, T // tm, N // tn),
        in_specs=in_specs,
        out_specs=pl.BlockSpec((1, tm, tn), lambda b, i, j: (b, i, j)),
        out_shape=jax.ShapeDtypeStruct((G, T, N), out_dtype),
        scratch_shapes=[pltpu.VMEM((tm, D), BF16)],
        compiler_params=_cparams(("parallel", "parallel", "arbitrary")),
        name="norm_mod_proj",
    )(*args)


ATTN_KEY_CHUNK = 512


def _attn_kernel(q_ref, k_ref, v_ref, *rest, tq, has_prefix):
    if has_prefix:
        kp_ref, vp_ref, o_ref = rest
    else:
        (o_ref,) = rest
    nt = (((1,), (1,)), ((), ()))
    q = jnp.concatenate(
        [q_ref[0, :, g * HEAD_DIM:(g + 1) * HEAD_DIM] for g in range(ATTN_GROUP)], axis=0)
    n_keys = k_ref.shape[1]
    kc = min(ATTN_KEY_CHUNK, n_keys)
    chunks = [(kp_ref, vp_ref, 0, kp_ref.shape[1])] if has_prefix else []
    chunks += [(k_ref, v_ref, c * kc, kc) for c in range(n_keys // kc)]

    def scores(chunk):
        kr, _, start, size = chunk
        return lax.dot_general(q, kr[0, start:start + size, :], nt, preferred_element_type=F32)

    m = jnp.full((q.shape[0], 1), -1e30, F32)
    l = jnp.zeros((q.shape[0], 1), F32)
    o = jnp.zeros((q.shape[0], HEAD_DIM), F32)
    s_next = scores(chunks[0])
    for ci, (_, vr, start, size) in enumerate(chunks):
        s = s_next
        if ci + 1 < len(chunks):
            s_next = scores(chunks[ci + 1])
        m_new = jnp.maximum(m, jnp.max(s, axis=-1, keepdims=True))
        alpha = jnp.exp(m - m_new)
        p = jnp.exp(s - m_new)
        l = alpha * l + jnp.sum(p, axis=-1, keepdims=True)
        o = alpha * o + jnp.dot(p.astype(BF16), vr[0, start:start + size, :], preferred_element_type=F32)
        m = m_new
    o = o * (1.0 / l)
    for g in range(ATTN_GROUP):
        o_ref[0, :, g * HEAD_DIM:(g + 1) * HEAD_DIM] = o[g * tq:(g + 1) * tq].astype(o_ref.dtype)


def _attention(qkvu, prefix, *, tq):
    B, L, _ = qkvu.shape
    tq = min(tq, L)
    gw = ATTN_GROUP * HEAD_DIM
    k_blk = ATTN_HEADS
    v_blk = ATTN_HEADS + ATTN_KV_HEADS
    in_specs = [
        pl.BlockSpec((1, tq, gw), lambda b, h, i: (b, i, h)),
        pl.BlockSpec((1, L, HEAD_DIM), lambda b, h, i: (b, 0, k_blk + h)),
        pl.BlockSpec((1, L, HEAD_DIM), lambda b, h, i: (b, 0, v_blk + h)),
    ]
    args = [qkvu, qkvu, qkvu]
    if prefix is not None:
        Lp = prefix.shape[1]
        in_specs += [
            pl.BlockSpec((1, Lp, HEAD_DIM), lambda b, h, i: (b, 0, k_blk + h)),
            pl.BlockSpec((1, Lp, HEAD_DIM), lambda b, h, i: (b, 0, v_blk + h)),
        ]
        args += [prefix, prefix]
    return pl.pallas_call(
        functools.partial(_attn_kernel, tq=tq, has_prefix=prefix is not None),
        grid=(B, ATTN_KV_HEADS, L // tq),
        in_specs=in_specs,
        out_specs=pl.BlockSpec((1, tq, gw), lambda b, h, i: (b, i, h)),
        out_shape=jax.ShapeDtypeStruct((B, L, ATTN_HEADS * HEAD_DIM), BF16),
        compiler_params=_cparams(("parallel", "parallel", "arbitrary")),
        name="gqa_attention",
    )(*args)


S5_SLAB_GROUPS = LANES // S5_GROUP
S5_KD = S5_CHUNK * LANES
S5_PAIR_COLS = 4 * LANES


def _s5_end_kernel(u_ref, w_ref, e_ref):
    e_ref[...] = jnp.dot(u_ref[0], w_ref[0], preferred_element_type=F32)


def _s5_scan_kernel(e_ref, cf_ref, cb_ref, h_ref, *, batch, nc_ctx, nc_all):
    cps = SUBLANES // batch
    g_ctx, g_all = nc_ctx // cps, nc_all // cps

    def advance(hr, hi, er, ei, c_ref):
        ar, ai = c_ref[0:1, :], c_ref[1:2, :]
        return ar * hr - ai * hi + er, ar * hi + ai * hr + ei

    def sweep(r, state, col0, c_ref, order):
        e8 = e_ref[pl.ds(r, SUBLANES), col0:col0 + 2 * LANES]
        hr, hi = state
        out_r, out_i = [None] * cps, [None] * cps
        for c in order:
            out_r[c], out_i[c] = hr, hi
            rows = slice(c * batch, (c + 1) * batch)
            hr, hi = advance(hr, hi, e8[rows, :LANES], e8[rows, LANES:], c_ref)
        h_ref[pl.ds(r, SUBLANES), col0:col0 + LANES] = jnp.concatenate(out_r, axis=0).astype(h_ref.dtype)
        h_ref[pl.ds(r, SUBLANES), col0 + LANES:col0 + 2 * LANES] = (
            jnp.concatenate(out_i, axis=0).astype(h_ref.dtype))
        return hr, hi

    def body(i, carry):
        fwd, bwd = carry
        fwd = sweep(pl.multiple_of(i * SUBLANES, SUBLANES), fwd, 0, cf_ref, range(cps))
        gi = jnp.where(i < g_ctx, g_ctx - 1 - i, g_all - 1 - (i - g_ctx))
        bwd = sweep(pl.multiple_of(gi * SUBLANES, SUBLANES), bwd, 2 * LANES, cb_ref, reversed(range(cps)))
        return fwd, bwd

    zero = jnp.zeros((batch, LANES), F32)
    lax.fori_loop(0, g_all, body, ((zero, zero), (zero, zero)))


def _s5_out_kernel(u_ref, us_ref, lag_ref, h_ref, win_ref, d_ref, y_ref, tz_ref):
    T = S5_CHUNK
    t_blocks = tz_ref.shape[1] // LANES
    t0 = pl.program_id(1) * t_blocks
    for tt in range(t_blocks):
        for s in range(T):
            tz_ref[s * LANES:(s + 1) * LANES, tt * LANES:(tt + 1) * LANES] = lag_ref[0, t0 + tt - s + (T - 1)]
    y = jnp.dot(u_ref[0], tz_ref[...], preferred_element_type=F32)
    y += jnp.dot(h_ref[...].astype(BF16), win_ref[0], preferred_element_type=F32)
    y += us_ref[0].astype(F32) * d_ref[0]
    y_ref[0] = y.astype(y_ref.dtype)


def _s5_tables(lam_re, lam_im, log_dt, b_re, b_im, c_re, c_im, d_skip):
    T, C, P = S5_CHUNK, S5_GROUP, S5_STATE
    G = lam_re.shape[1]
    hi = lax.Precision.HIGHEST
    lam = lax.complex(jnp.minimum(lam_re.astype(F32), -1e-4), lam_im.astype(F32))
    dt = jnp.exp(log_dt.astype(F32))[..., None]
    lam_dt = lam * dt
    lam_bar = jnp.exp(lam_dt)
    bmat = lax.complex(b_re.astype(F32), b_im.astype(F32))
    b_bar = ((lam_bar - 1.0) / lam)[..., None] * bmat
    cmat = lax.complex(c_re.astype(F32), c_im.astype(F32))
    steps = jnp.arange(T + 1, dtype=F32)
    pw = jnp.exp(lam_dt[:, None] * steps[None, :, None, None])

    kern = jnp.real(jnp.einsum('zgcp,zjgp,zgpd->zjgcd', cmat, pw[:, :T], b_bar, precision=hi))

    def end_w(p_sel, bb):
        w = p_sel[..., None] * bb[None]
        w = jnp.concatenate([jnp.real(w), jnp.imag(w)], axis=2)
        return w.transpose(1, 0, 3, 2).reshape(G, T * C, 2 * P)

    w_end = jnp.concatenate([end_w(pw[0, :T][::-1], b_bar[0]), end_w(pw[1, :T], b_bar[1])], axis=-1)

    def in_w(p_sel, cm):
        z = cm[None] * p_sel[:, :, None, :]
        w = jnp.concatenate([jnp.real(z), -jnp.imag(z)], axis=-1)
        return w.transpose(1, 3, 0, 2).reshape(G, 2 * P, T * C)

    w_in_f = in_w(pw[0, 1:], cmat[0])
    w_in_b = in_w(pw[1, 1:][::-1], cmat[1])

    def coef(z):
        return jnp.stack([jnp.real(z).reshape(G * P), jnp.imag(z).reshape(G * P)])

    sg = S5_SLAB_GROUPS
    ngb, npair = G // sg, sg // 2
    grp = np.arange(sg)
    tok_cols_grp = np.tile(np.repeat(grp, C), T)
    state_cols_grp = (2 * np.arange(npair)[:, None, None, None] + np.arange(2)[None, None, :, None]
                      + np.zeros((1, 4, 1, P), np.int64)).reshape(-1)
    rows_gd = np.repeat(grp, C)

    lag_k = jnp.concatenate([kern[1][1:][::-1], (kern[0][0] + kern[1][0])[None], kern[0][1:]], axis=0)
    lag_src = lag_k.reshape(2 * T - 1, ngb, sg, C, C).transpose(1, 0, 2, 4, 3).reshape(ngb * (2 * T - 1), LANES, C)
    rep_c = np.tile(np.eye(C, dtype=np.float32), (1, sg))
    mask_c = (rows_gd[:, None] == np.repeat(grp, C)[None, :]).astype(np.float32)[None]
    lag_tiles = _spread(lag_src, rep_c, mask_c, lambda n: 0).reshape(ngb, 2 * T - 1, LANES, LANES)

    end_src = w_end.reshape(ngb, sg, T, C, 4 * P).transpose(0, 2, 1, 3, 4).reshape(ngb * T, LANES, 4 * P)
    rep_state = np.tile(np.eye(4 * P, dtype=np.float32).reshape(4 * P, 1, 4, 1, P), (1, npair, 1, 2, 1))
    rep_state = rep_state.reshape(4 * P, npair * S5_PAIR_COLS)
    mask_state = (rows_gd[:, None] == state_cols_grp[None, :]).astype(np.float32)[None]
    end_slab = _spread(end_src, rep_state, mask_state, lambda n: 0).reshape(ngb, S5_KD, npair * S5_PAIR_COLS)

    w_in = jnp.concatenate([w_in_f.reshape(G, 2, P, T * C), w_in_b.reshape(G, 2, P, T * C)], axis=1)
    in_src = w_in.reshape(ngb, npair, 2, 4, P, T * C).transpose(0, 1, 3, 2, 4, 5)
    in_src = in_src.reshape(ngb * npair * 4, 2 * P, T * C)
    rep_tok = np.tile(np.eye(T * C, dtype=np.float32).reshape(T * C, T, 1, C), (1, 1, sg, 1)).reshape(T * C, S5_KD)
    row_grp = 2 * np.arange(npair)[:, None, None] + np.repeat(np.arange(2), P)[None, :, None]
    mask_tok = (row_grp == tok_cols_grp[None, None, :]).astype(np.float32)
    in_slab = _spread(in_src, rep_tok, mask_tok, lambda n: (n // 4) % npair)
    in_slab = in_slab.reshape(ngb, npair * S5_PAIR_COLS, S5_KD)

    d_slab = jnp.tile(d_skip.astype(F32).reshape(ngb, 1, LANES), (1, 1, T))
    return lag_tiles, end_slab, in_slab, coef(pw[0, T]), coef(pw[1, T]), d_slab


def _spread_kernel(src_ref, rep_ref, mask_ref, o_ref):
    o_ref[0] = (jnp.dot(src_ref[0].astype(BF16), rep_ref[...], preferred_element_type=F32)
                * mask_ref[0]).astype(o_ref.dtype)


def _spread(src, rep, mask, mask_index):
    n, rows, k = src.shape
    width = rep.shape[1]
    return pl.pallas_call(
        _spread_kernel,
        grid=(n,),
        in_specs=[
            pl.BlockSpec((1, rows, k), lambda i: (i, 0, 0)),
            pl.BlockSpec((k, width), lambda i: (0, 0)),
            pl.BlockSpec((1, rows, width), lambda i: (mask_index(i), 0, 0)),
        ],
        out_specs=pl.BlockSpec((1, rows, width), lambda i: (i, 0, 0)),
        out_shape=jax.ShapeDtypeStruct((n, rows, width), BF16),
        compiler_params=_cparams(("parallel",)),
        name="s5_spread_operator",
    )(src, jnp.asarray(rep, BF16), jnp.asarray(mask, F32))


def _s5(u_ctx, u_lat, tables):
    tz, w_end, w_in, cf, cb, d_slab = tables
    T = S5_CHUNK
    B, Lc, W = u_ctx.shape
    L = u_lat.shape[1]
    ngb = W // LANES
    nc_ctx, nc_all = Lc // T, (Lc + L) // T
    cps = SUBLANES // B
    assert SUBLANES % B == 0 and nc_ctx % cps == 0 and (nc_all - nc_ctx) % cps == 0
    M = nc_all * B
    kd, sw = S5_KD, w_end.shape[-1]
    n_half = 4
    hw = kd // n_half
    u = jnp.concatenate([u_ctx, u_lat], axis=1)
    u = u.reshape(B, nc_all, T, ngb, LANES).transpose(3, 1, 0, 2, 4).reshape(ngb, M, kd)

    e = pl.pallas_call(
        _s5_end_kernel,
        grid=(ngb, n_half),
        in_specs=[
            pl.BlockSpec((1, M, kd), lambda g, j: (g, 0, 0)),
            pl.BlockSpec((1, kd, sw // n_half), lambda g, j: (g, 0, j)),
        ],
        out_specs=pl.BlockSpec((M, sw // n_half), lambda g, j: (0, g * n_half + j)),
        out_shape=jax.ShapeDtypeStruct((M, ngb * sw), F32),
        compiler_params=_cparams(("parallel", "parallel")),
        name="s5_chunk_end_states",
    )(u, w_end)

    h = pl.pallas_call(
        functools.partial(_s5_scan_kernel, batch=B, nc_ctx=nc_ctx, nc_all=nc_all),
        grid=(ngb * sw // S5_PAIR_COLS,),
        in_specs=[pl.BlockSpec((M, S5_PAIR_COLS), lambda j: (0, j))]
        + [pl.BlockSpec((2, LANES), lambda j: (0, j))] * 2,
        out_specs=pl.BlockSpec((M, S5_PAIR_COLS), lambda j: (0, j)),
        out_shape=jax.ShapeDtypeStruct((M, ngb * sw), F32),
        compiler_params=_cparams(("parallel",)),
        name="s5_chunk_scan",
    )(e, cf, cb)

    y = pl.pallas_call(
        _s5_out_kernel,
        grid=(ngb, n_half),
        in_specs=[
            pl.BlockSpec((1, M, kd), lambda g, j: (g, 0, 0)),
            pl.BlockSpec((1, M, hw), lambda g, j: (g, 0, j)),
            pl.BlockSpec((1,) + tz.shape[1:], lambda g, j: (g, 0, 0, 0)),
            pl.BlockSpec((M, sw), lambda g, j: (0, g)),
            pl.BlockSpec((1, sw, hw), lambda g, j: (g, 0, j)),
            pl.BlockSpec((1, 1, hw), lambda g, j: (g, 0, j)),
        ],
        out_specs=pl.BlockSpec((1, M, hw), lambda g, j: (g, 0, j)),
        out_shape=jax.ShapeDtypeStruct((ngb, M, kd), BF16),
        scratch_shapes=[pltpu.VMEM((kd, hw), BF16)],
        compiler_params=_cparams(("parallel", "parallel")),
        name="s5_chunk_outputs",
    )(u, u, tz, h, w_in, d_slab)

    y = y.reshape(ngb, nc_all, B, T, LANES).transpose(2, 1, 3, 0, 4).reshape(B, Lc + L, W)
    return y[:, :Lc], y[:, Lc:]


def _glu_kernel(y_ref, w_ref, b_ref, o_ref):
    y = y_ref[0].astype(F32)
    z = jnp.dot(jax.nn.gelu(y).astype(BF16), w_ref[...], preferred_element_type=F32) + b_ref[...]
    n = o_ref.shape[-1]
    o_ref[0] = (z[:, :n] * jax.nn.sigmoid(z[:, n:])).astype(o_ref.dtype)


def _glu(y, w, b, *, tm):
    G, T, W = y.shape
    tm = min(tm, T)
    return pl.pallas_call(
        _glu_kernel,
        grid=(G, T // tm),
        in_specs=[
            pl.BlockSpec((1, tm, W), lambda g, i: (g, i, 0)),
            pl.BlockSpec((W, 2 * W), lambda g, i: (0, 0)),
            pl.BlockSpec((1, 2 * W), lambda g, i: (0, 0)),
        ],
        out_specs=pl.BlockSpec((1, tm, W), lambda g, i: (g, i, 0)),
        out_shape=jax.ShapeDtypeStruct((G, T, W), BF16),
        compiler_params=_cparams(("parallel", "parallel")),
        name="s5_glu",
    )(y, w, b.reshape(1, 2 * W))


def _out_res_kernel(*refs, widths):
    n = len(widths)
    lhs = refs[:n]
    w_ref, x_ref, gate_ref, g_ref, o_ref = refs[n:]
    y = None
    off = 0
    for r, k in zip(lhs, widths):
        part = jnp.dot(r[0], w_ref[off:off + k, :], preferred_element_type=F32)
        y = part if y is None else y + part
        off += k
    o_ref[0] = x_ref[0] + gate_ref[0] * (_rms(y) * g_ref[...])


def _out_res(parts, w, x, gate, g, *, tm):
    G, T, D = x.shape
    tm = min(tm, T)
    widths = tuple(p.shape[-1] for p in parts)
    K = sum(widths)
    in_specs = [pl.BlockSpec((1, tm, k), lambda b, i: (b, i, 0)) for k in widths]
    in_specs += [
        pl.BlockSpec((K, D), lambda b, i: (0, 0)),
        pl.BlockSpec((1, tm, D), lambda b, i: (b, i, 0)),
        pl.BlockSpec((1, 1, D), _vec_map(gate, 2)),
        pl.BlockSpec((1, D), lambda b, i: (0, 0)),
    ]
    return pl.pallas_call(
        functools.partial(_out_res_kernel, widths=widths),
        grid=(G, T // tm),
        in_specs=in_specs,
        out_specs=pl.BlockSpec((1, tm, D), lambda b, i: (b, i, 0)),
        out_shape=jax.ShapeDtypeStruct((G, T, D), F32),
        compiler_params=_cparams(("parallel", "parallel")),
        name="out_proj_residual",
    )(*parts, w, x, gate, g.reshape(1, D))


HALO = BF16_ROWS


def _ffn_kernel(x_ref, xp_ref, xn_ref, g2_ref, sh_ref, sc_ref, wa_ref, wv_ref, cwa_ref, cwv_ref,
                cba_ref, cbv_ref, wd_ref, gate_ref, g3_ref, o_ref, h_ref, acc_ref, ua_ref, uv_ref, *, tm):
    i = pl.program_id(1)
    f = pl.program_id(2)

    def normmod(x):
        h = _rms(x) * g2_ref[...]
        return h * (1.0 + sc_ref[0]) + sh_ref[0]

    @pl.when(f == 0)
    def _():
        zeros = jnp.zeros((HALO - SUBLANES, x_ref.shape[-1]), F32)
        hp = jnp.where(i == 0, 0.0, normmod(xp_ref[0]))
        hn = jnp.where(i == pl.num_programs(1) - 1, 0.0, normmod(xn_ref[0]))
        h_ref[0:HALO] = jnp.concatenate([zeros, hp], axis=0).astype(BF16)
        h_ref[HALO:HALO + tm] = normmod(x_ref[0]).astype(BF16)
        h_ref[HALO + tm:] = jnp.concatenate([hn, zeros], axis=0).astype(BF16)
        acc_ref[...] = jnp.zeros_like(acc_ref)

    h = h_ref[...]
    ua_ref[...] = jnp.dot(h, wa_ref[...], preferred_element_type=F32)
    uv_ref[...] = jnp.dot(h, wv_ref[...], preferred_element_type=F32)

    def conv(u_ref, cw_ref, cb_ref):
        return (u_ref[HALO - 1:HALO - 1 + tm] * cw_ref[0:1, :] + u_ref[HALO:HALO + tm] * cw_ref[1:2, :]
                + u_ref[HALO + 1:HALO + 1 + tm] * cw_ref[2:3, :] + cb_ref[...])

    a = conv(ua_ref, cwa_ref, cba_ref)
    v = conv(uv_ref, cwv_ref, cbv_ref)
    gated = (a * jax.nn.sigmoid(a) * v).astype(BF16)
    acc_ref[...] += jnp.dot(gated, wd_ref[...], preferred_element_type=F32)

    @pl.when(f == pl.num_programs(2) - 1)
    def _():
        o_ref[0] = x_ref[0] + gate_ref[0] * (_rms(acc_ref[...]) * g3_ref[...])


def _conv_ffn(x, g2, shift, scale, w_up, conv_w, conv_b, w_down, gate, g3, *, tm, tf):
    G, T, D = x.shape
    F = w_down.shape[0]
    tm = min(tm, T)
    nf = F // tf
    n_i = T // tm
    rb = tm // SUBLANES
    last_rb = T // SUBLANES - 1
    conv_b = conv_b.reshape(1, 2 * F)
    return pl.pallas_call(
        functools.partial(_ffn_kernel, tm=tm),
        grid=(G, n_i, nf),
        in_specs=[
            pl.BlockSpec((1, tm, D), lambda b, i, f: (b, i, 0)),
            pl.BlockSpec((1, SUBLANES, D), lambda b, i, f: (b, jnp.maximum(i * rb - 1, 0), 0)),
            pl.BlockSpec((1, SUBLANES, D), lambda b, i, f: (b, jnp.minimum((i + 1) * rb, last_rb), 0)),
            pl.BlockSpec((1, D), lambda b, i, f: (0, 0)),
            pl.BlockSpec((1, 1, D), _vec_map(shift, 3)),
            pl.BlockSpec((1, 1, D), _vec_map(scale, 3)),
            pl.BlockSpec((D, tf), lambda b, i, f: (0, f)),
            pl.BlockSpec((D, tf), lambda b, i, f: (0, nf + f)),
            pl.BlockSpec((3, tf), lambda b, i, f: (0, f)),
            pl.BlockSpec((3, tf), lambda b, i, f: (0, nf + f)),
            pl.BlockSpec((1, tf), lambda b, i, f: (0, f)),
            pl.BlockSpec((1, tf), lambda b, i, f: (0, nf + f)),
            pl.BlockSpec((tf, D), lambda b, i, f: (f, 0)),
            pl.BlockSpec((1, 1, D), _vec_map(gate, 3)),
            pl.BlockSpec((1, D), lambda b, i, f: (0, 0)),
        ],
        out_specs=pl.BlockSpec((1, tm, D), lambda b, i, f: (b, i, 0)),
        out_shape=jax.ShapeDtypeStruct((G, T, D), F32),
        scratch_shapes=[
            pltpu.VMEM((tm + 2 * HALO, D), BF16),
            pltpu.VMEM((tm, D), F32),
            pltpu.VMEM((tm + 2 * HALO, tf), F32),
            pltpu.VMEM((tm + 2 * HALO, tf), F32),
        ],
        compiler_params=_cparams(("parallel", "parallel", "arbitrary")),
        name="conv_ffn",
    )(x, x, x, g2.reshape(1, D), shift, scale, w_up, w_up, conv_w, conv_w, conv_b, conv_b, w_down,
      gate, g3.reshape(1, D))


GLA_BLOCKS = HGRN_CHUNK // SUBLANES


def _gla_operands(z, v, q, lb, rev):
    nb, rb = GLA_BLOCKS, SUBLANES
    order = list(range(nb))[::-1] if rev else list(range(nb))
    scan_of = {b: j for j, b in enumerate(order)}
    pos = lax.broadcasted_iota(jnp.int32, (rb, HGRN_DK), 0)
    if rev:
        pos = rb - 1 - pos

    def prev(x, k):
        return pltpu.roll(x, (rb - k) if rev else k, 0)

    def nxt(x, k):
        return pltpu.roll(x, k if rev else (rb - k), 0)

    def blocks(x):
        return [x[rb * b:rb * (b + 1)] for b in range(nb)]

    def rows(bl):
        return jnp.concatenate(bl, axis=0)

    f = lb + (1.0 - lb) * jax.nn.sigmoid(z)
    fb = blocks(f)
    kb = blocks(1.0 - f)

    p8, s8, bt = [], [], []
    first = rb - 1 if rev else 0
    for b in range(nb):
        x = y = fb[b]
        for k in (1, 2, 4):
            x = x * jnp.where(pos >= k, prev(x, k), 1.0)
            y = y * jnp.where(pos <= rb - 1 - k, nxt(y, k), 1.0)
        p8.append(x)
        s8.append(jnp.where(pos <= rb - 2, nxt(y, 1), 1.0))
        bt.append(y[first:first + 1, :])
    bts = [bt[order[j]] for j in range(nb)]
    ones = jnp.ones_like(bts[0])
    before = [ones]
    for j in range(1, nb):
        before.append(before[j - 1] * bts[j - 1])
    after = [ones] * nb
    for j in range(nb - 2, -1, -1):
        after[j] = after[j + 1] * bts[j + 1]
    ftot = before[nb - 1] * bts[nb - 1]

    kbase = [kb[b] * s8[b] for b in range(nb)]
    ops = {"v": v, "ftot": ftot,
           "kdec": rows([kbase[b] * after[scan_of[b]] for b in range(nb)]).astype(BF16)}
    if q is None:
        return ops

    qb = blocks(q)
    q8 = [qb[b] * p8[b] for b in range(nb)]
    ops["q_state"] = rows([q8[b] * before[scan_of[b]] for b in range(nb)]).astype(BF16)
    ops["q_far"] = rows(q8).astype(BF16)
    ops["q_near"] = q.astype(BF16)

    vb = blocks(v.astype(F32))
    kcols, vcols = [], []
    chain = {}
    for d in range(1, nb):
        for j in range(nb - d):
            chain[j] = kbase[order[j]] if d == 1 else chain[j] * bts[j + d - 1]
            kcols.append(chain[j])
            vcols.append(vb[order[j]])
    ops["k_far"] = rows(kcols).astype(BF16)
    ops["v_far"] = rows(vcols).astype(BF16)

    kvar = []
    cur = kb
    for d in range(rb):
        if d:
            cur = [cur[b] * nxt(fb[b], d) for b in range(nb)]
        kvar.append(rows(cur).astype(BF16))
    ops["k_near"] = jnp.concatenate(kvar, axis=0)
    ops["v_near"] = jnp.concatenate([v] * rb, axis=0)
    return ops


def _gla_first_dots(ops, st):
    nt = (((1,), (1,)), ((), ()))
    st_new = st * ops["ftot"] + lax.dot_general(ops["v"], ops["kdec"], (((0,), (0,)), ((), ())),
                                                preferred_element_type=F32)
    if "q_near" not in ops:
        return st_new, None
    o = lax.dot_general(ops["q_state"], st.astype(BF16), nt, preferred_element_type=F32)
    far = lax.dot_general(ops["q_far"], ops["k_far"], nt, preferred_element_type=F32)
    near = lax.dot_general(ops["q_near"], ops["k_near"], nt, preferred_element_type=F32)
    return st_new, (o, far, near)


def _gla_second_dots(ops, scores, masks):
    o, far, near = scores
    ma, mb = masks
    o = o + jnp.dot((far * ma).astype(BF16), ops["v_far"], preferred_element_type=F32)
    return o + jnp.dot((near * mb).astype(BF16), ops["v_near"], preferred_element_type=F32)


def _gla_mask_tables():
    T, nb, rb = HGRN_CHUNK, GLA_BLOCKS, SUBLANES
    t = np.arange(T)
    mas, mbs = [], []
    for rev in (False, True):
        sblk = (nb - 1 - t // rb) if rev else t // rb
        spos = (rb - 1 - t % rb) if rev else t % rb
        cols = [(d, j) for d in range(1, nb) for j in range(nb - d)]
        ma = np.zeros((T, len(cols) * rb), np.float32)
        for c, (d, j) in enumerate(cols):
            ma[sblk == j + d, c * rb:(c + 1) * rb] = 1.0
        mb = np.zeros((T, rb * T), np.float32)
        for d in range(rb):
            ok = (t[:, None] // rb == t[None, :] // rb) & (spos[:, None] - spos[None, :] == d)
            mb[:, d * T:(d + 1) * T] = ok
        mas.append(ma)
        mbs.append(mb)
    return jnp.asarray(np.stack(mas)), jnp.asarray(np.stack(mbs))


def _gla_kernel(q_ref, zf_ref, zb_ref, v_ref, g_ref, czf_ref, czb_ref, cv_ref, lb_ref, ng_ref,
                ma_ref, mb_ref, o_ref, oacc_ref, *, n_lat, n_ctx, hp):
    T, dk = HGRN_CHUNK, HGRN_DK
    jobs = [(h, rev) for h in range(hp) for rev in (False, True)]
    zero_state = jnp.zeros((dk, dk), F32)

    def cols(h):
        return slice(h * dk, (h + 1) * dk)

    def start(rev, i, n):
        return pl.multiple_of(((n - 1 - i) if rev else i) * T, T)

    def lb_row(h, rev):
        return lb_ref[int(rev):int(rev) + 1, cols(h)]

    def ctx_body(i, sts):
        ops = []
        for h, rev in jobs:
            r = start(rev, i, n_ctx)
            z = (czb_ref if rev else czf_ref)[0, pl.ds(r, T), cols(h)].astype(F32)
            ops.append(_gla_operands(z, cv_ref[0, pl.ds(r, T), cols(h)], None, lb_row(h, rev), rev))
        return tuple(_gla_first_dots(o, st)[0] for o, st in zip(ops, sts))

    def lat_body(i, sts, second_half):
        rs = [start(rev, i, n_lat) for _, rev in jobs]
        ops = []
        for (h, rev), r in zip(jobs, rs):
            qz = q_ref[0, pl.ds(r, T), cols(h)].astype(F32)
            z = (zb_ref if rev else zf_ref)[0, pl.ds(r, T), cols(h)].astype(F32)
            ops.append(_gla_operands(z, v_ref[0, pl.ds(r, T), cols(h)], qz * jax.nn.sigmoid(qz),
                                     lb_row(h, rev), rev))
        firsts = [_gla_first_dots(o, st) for o, st in zip(ops, sts)]
        outs = [_gla_second_dots(o, f[1], (ma_ref[int(rev)], mb_ref[int(rev)]))
                for o, f, (_, rev) in zip(ops, firsts, jobs)]
        for (h, _), r, o in zip(jobs, rs, outs):
            if second_half:
                o = o + oacc_ref[pl.ds(r, T), cols(h)]
                gz = g_ref[0, pl.ds(r, T), cols(h)].astype(F32)
                o = _rms(o) * ng_ref[...] * (gz * jax.nn.sigmoid(gz))
                o_ref[0, pl.ds(r, T), cols(h)] = o.astype(o_ref.dtype)
            else:
                oacc_ref[pl.ds(r, T), cols(h)] = o
        return tuple(f[0] for f in firsts)

    sts = lax.fori_loop(0, n_ctx, ctx_body, (zero_state,) * len(jobs))
    sts = lax.fori_loop(0, n_lat // 2, functools.partial(lat_body, second_half=False), sts)
    lax.fori_loop(n_lat // 2, n_lat, functools.partial(lat_body, second_half=True), sts)


GLA_HEADS_PER_STEP = 2


def _gla(p_lat, p_ctx, lb, norm_g, n_heads):
    B, L, _ = p_lat.shape
    Lc = p_ctx.shape[1]
    hp = GLA_HEADS_PER_STEP
    steps = n_heads // hp
    dv = HGRN_DK
    w = hp * dv

    def col(group):
        return lambda b, h: (b, 0, group * steps + h)

    lat_specs = [pl.BlockSpec((1, L, w), col(gidx)) for gidx in range(5)]
    ctx_specs = [pl.BlockSpec((1, Lc, w), col(gidx)) for gidx in (1, 2, 3)]
    ma, mb = _gla_mask_tables()
    n_lat = L // HGRN_CHUNK
    assert n_lat % 2 == 0 and n_heads % hp == 0
    return pl.pallas_call(
        functools.partial(_gla_kernel, n_lat=n_lat, n_ctx=Lc // HGRN_CHUNK, hp=hp),
        grid=(B, steps),
        in_specs=lat_specs + ctx_specs + [
            pl.BlockSpec((2, w), lambda b, h: (0, h)),
            pl.BlockSpec((1, dv), lambda b, h: (0, 0)),
            pl.BlockSpec(ma.shape, lambda b, h: (0, 0, 0)),
            pl.BlockSpec(mb.shape, lambda b, h: (0, 0, 0)),
        ],
        out_specs=pl.BlockSpec((1, L, w), lambda b, h: (b, 0, h)),
        out_shape=jax.ShapeDtypeStruct((B, L, n_heads * dv), BF16),
        scratch_shapes=[pltpu.VMEM((L, w), F32)],
        compiler_params=_cparams(("parallel", "parallel")),
        name="hgrn2_bidirectional",
    )(*([p_lat] * 5), *([p_ctx] * 3), lb, norm_g.reshape(1, dv), ma, mb)


def _rope_tables(n_tokens):
    rows = n_tokens // GRID_W
    row = jnp.repeat(jnp.arange(rows, dtype=F32), GRID_W)
    colp = jnp.tile(jnp.arange(GRID_W, dtype=F32), rows)
    inv = ROPE_THETA ** (-jnp.arange(AXIS_FREQS, dtype=F32) / AXIS_FREQS)
    ang = jnp.stack([row[:, None] * inv, colp[:, None] * inv], axis=1)
    cos, sin = jnp.cos(ang), jnp.sin(ang)
    zero = jnp.zeros_like(sin)
    full = lambda a, b: jnp.stack([a, b], axis=2).reshape(n_tokens, HEAD_DIM)
    return full(cos, cos), full(-sin, zero), full(zero, sin)


def _identity_rope(n_tokens):
    return (jnp.ones((n_tokens, HEAD_DIM), F32), jnp.zeros((n_tokens, HEAD_DIM), F32),
            jnp.zeros((n_tokens, HEAD_DIM), F32))


def kernel(x, c, ctx, c_ctx, mod_w, mod_b, norm_g, ab_w_in, ab_w_out, attn_q_norm, attn_k_norm,
           s5_lam_re, s5_lam_im, s5_log_dt, s5_b_re, s5_b_im, s5_c_re, s5_c_im, s5_d,
           s5_glu_w, s5_glu_b, c_w_in, c_w_out, hgrn_lb_logits, hgrn_norm,
           ffn_w_up, ffn_conv_w, ffn_conv_b, ffn_w_down):
    B, L, D = x.shape
    Lc = ctx.shape[1]
    depth = mod_w.shape[0]
    attn_w = ATTN_HEADS * HEAD_DIM
    kv_w = ATTN_KV_HEADS * HEAD_DIM
    qk_w = attn_w + kv_w

    lb_all = jnp.cumsum(jax.nn.softmax(hgrn_lb_logits.astype(F32), axis=0), axis=0)
    lb_all = lb_all - lb_all[:1]

    cvec = jnp.concatenate([c, c_ctx[None], jnp.zeros((SUBLANES - (B + 1) % SUBLANES, D), F32)], axis=0)
    mods = _modulation(cvec, mod_w, mod_b)

    rope_lat = _rope_tables(L)
    rope_ctx = _identity_rope(Lc)

    for l in range(depth):
        last = l == depth - 1
        m_lat = [mods[l, :B, k * D:(k + 1) * D][:, None, :] for k in range(6)]
        m_ctx = [mods[l, B:B + 1, k * D:(k + 1) * D][:, None, :] for k in range(6)]
        g = norm_g[l]
        w_up = ffn_w_up[l].astype(BF16)
        w_down = ffn_w_down[l].astype(BF16)
        if l % 2 == 0:
            e = l // 2
            w_in = ab_w_in[e].astype(BF16)
            gain = jnp.concatenate([jnp.tile(attn_q_norm[e] * (HEAD_DIM ** -0.5), ATTN_HEADS),
                                    jnp.tile(attn_k_norm[e], ATTN_KV_HEADS)]).reshape(1, qk_w)
            p_lat = _proj(x, g[0], m_lat[0], m_lat[1], w_in, tm=512, tn=qk_w, qk=(gain,) + rope_lat)
            p_ctx = _proj(ctx, g[0], m_ctx[0], m_ctx[1], w_in, tm=512, tn=qk_w, qk=(gain,) + rope_ctx)
            a_lat = _attention(p_lat, p_ctx, tq=128)
            tables = _s5_tables(s5_lam_re[e], s5_lam_im[e], s5_log_dt[e], s5_b_re[e], s5_b_im[e],
                                s5_c_re[e], s5_c_im[e], s5_d[e])
            u0 = qk_w + kv_w
            y_ctx, y_lat = _s5(p_ctx[..., u0:], p_lat[..., u0:], tables)
            glu_w = s5_glu_w[e].astype(BF16)
            w_out = ab_w_out[e].astype(BF16)
            s_lat = _glu(y_lat, glu_w, s5_glu_b[e], tm=512)
            x = _out_res([a_lat, s_lat], w_out, x, m_lat[2], g[1], tm=512)
            if not last:
                a_ctx = _attention(p_ctx, None, tq=128)
                s_ctx = _glu(y_ctx, glu_w, s5_glu_b[e], tm=512)
                ctx = _out_res([a_ctx, s_ctx], w_out, ctx, m_ctx[2], g[1], tm=512)
        else:
            o_idx = l // 2
            w_in = c_w_in[o_idx].astype(BF16)
            n_heads = c_w_out.shape[1] // HGRN_DK
            p_lat = _proj(x, g[0], m_lat[0], m_lat[1], w_in, tm=512, tn=1024)
            p_ctx = _proj(ctx, g[0], m_ctx[0], m_ctx[1], w_in, tm=512, tn=1024)
            o_lat = _gla(p_lat, p_ctx, lb_all[l], hgrn_norm[o_idx], n_heads)
            x = _out_res([o_lat], c_w_out[o_idx].astype(BF16), x, m_lat[2], g[1], tm=512)
            assert last, "context outputs of the HGRN2 mixer are only needed by a following layer"
        x = _conv_ffn(x, g[2], m_lat[3], m_lat[4], w_up, ffn_conv_w[l], ffn_conv_b[l], w_down,
                      m_lat[5], g[3], tm=512, tf=512)
        if not last:
            ctx = _conv_ffn(ctx, g[2], m_ctx[3], m_ctx[4], w_up, ffn_conv_w[l], ffn_conv_b[l], w_down,
                            m_ctx[5], g[3], tm=512, tf=512)
    return x
```

```python
import functools
import math

import jax
import jax.numpy as jnp
import numpy as np
from jax import lax
from jax.experimental import pallas as pl
from jax.experimental.pallas import tpu as pltpu

F32 = jnp.float32
BF16 = jnp.bfloat16
EPS = 1e-6

LANES = 128
SUBLANES = 8
BF16_ROWS = 16

HEAD_DIM = 128
ATTN_HEADS = 8
ATTN_KV_HEADS = 2
ATTN_GROUP = ATTN_HEADS // ATTN_KV_HEADS
GRID_W = 64
ROPE_THETA = 10000.0
AXIS_FREQS = HEAD_DIM // 4

S5_GROUP = 16
S5_STATE = 64
S5_CHUNK = 16

HGRN_DK = 128
HGRN_CHUNK = 64

VMEM_LIMIT = 56 * 1024 * 1024


def _cparams(sem):
    return pltpu.CompilerParams(dimension_semantics=sem, vmem_limit_bytes=VMEM_LIMIT)


def _vec_map(vec, nd_grid):
    shared = vec.shape[0] == 1
    if nd_grid == 2:
        return lambda g, i: (0 if shared else g, 0, 0)
    return lambda g, i, j: (0 if shared else g, 0, 0)


def _rms(x):
    return x * lax.rsqrt(jnp.mean(x * x, axis=-1, keepdims=True) + EPS)


def _mod_kernel(s_ref, w_ref, b_ref, o_ref):
    s = s_ref[...]
    a = s * jax.nn.sigmoid(s)
    o_ref[0] = jnp.dot(a.astype(BF16), w_ref[0].astype(BF16), preferred_element_type=F32) + b_ref[0]


def _modulation(cvec, mod_w, mod_b):
    depth, d, n = mod_w.shape
    rows = cvec.shape[0]
    tn = 1024
    return pl.pallas_call(
        _mod_kernel,
        grid=(depth, n // tn),
        in_specs=[
            pl.BlockSpec((rows, d), lambda l, j: (0, 0)),
            pl.BlockSpec((1, d, tn), lambda l, j: (l, 0, j)),
            pl.BlockSpec((1, 1, tn), lambda l, j: (l, 0, j)),
        ],
        out_specs=pl.BlockSpec((1, rows, tn), lambda l, j: (l, 0, j)),
        out_shape=jax.ShapeDtypeStruct((depth, rows, n), F32),
        compiler_params=_cparams(("parallel", "parallel")),
        name="modulation",
    )(cvec, mod_w, mod_b.reshape(depth, 1, n))


def _proj_kernel(x_ref, g_ref, sh_ref, sc_ref, w_ref, *rest, qk_heads):
    if qk_heads:
        gain_ref, cos_ref, s1_ref, s2_ref, o_ref, h_ref = rest
    else:
        o_ref, h_ref = rest
    j = pl.program_id(2)

    @pl.when(j == 0)
    def _():
        h = _rms(x_ref[0]) * g_ref[...]
        h = h * (1.0 + sc_ref[0]) + sh_ref[0]
        h_ref[...] = h.astype(BF16)

    acc = jnp.dot(h_ref[...], w_ref[...], preferred_element_type=F32)

    if not qk_heads:
        o_ref[0] = acc.astype(o_ref.dtype)
        return

    @pl.when(j == 0)
    def _():
        cos, s1, s2 = cos_ref[...], s1_ref[...], s2_ref[...]
        for hh in range(qk_heads):
            sl = slice(hh * HEAD_DIM, (hh + 1) * HEAD_DIM)
            y = _rms(acc[:, sl]) * gain_ref[:, sl]
            y = y * cos + pltpu.roll(y, HEAD_DIM - AXIS_FREQS, 1) * s1 + pltpu.roll(y, AXIS_FREQS, 1) * s2
            o_ref[0, :, sl] = y.astype(o_ref.dtype)

    @pl.when(j != 0)
    def _():
        o_ref[0] = acc.astype(o_ref.dtype)


def _proj(x, g, shift, scale, w, *, tm, tn, qk=None, out_dtype=BF16):
    G, T, D = x.shape
    N = w.shape[1]
    tm = min(tm, T)
    in_specs = [
        pl.BlockSpec((1, tm, D), lambda b, i, j: (b, i, 0)),
        pl.BlockSpec((1, D), lambda b, i, j: (0, 0)),
        pl.BlockSpec((1, 1, D), _vec_map(shift, 3)),
        pl.BlockSpec((1, 1, D), _vec_map(scale, 3)),
        pl.BlockSpec((D, tn), lambda b, i, j: (0, j)),
    ]
    args = [x, g.reshape(1, D), shift, scale, w]
    qk_heads = 0
    if qk is not None:
        gain, cos, s1, s2 = qk
        qk_heads = gain.shape[1] // HEAD_DIM
        assert qk_heads * HEAD_DIM == tn
        in_specs += [pl.BlockSpec((1, tn), lambda b, i, j: (0, 0))]
        in_specs += [pl.BlockSpec((tm, HEAD_DIM), lambda b, i, j: (i, 0))] * 3
        args += [gain, cos, s1, s2]
    return pl.pallas_call(
        functools.partial(_proj_kernel, qk_heads=qk_heads),
        grid=(G, T // tm, N // tn),
        in_specs=in_specs,
        out_specs=pl.BlockSpec((1, tm, tn), lambda b, i, j: (b, i, j)),
        out_shape=jax.ShapeDtypeStruct((G, T, N), out_dtype),
        scratch_shapes=[pltpu.VMEM((tm, D), BF16)],
        compiler_params=_cparams(("parallel", "parallel", "arbitrary")),
        name="norm_mod_proj",
    )(*args)


ATTN_KEY_CHUNK = 512


def _attn_kernel(q_ref, k_ref, v_ref, *rest, tq, has_prefix):
    if has_prefix:
        kp_ref, vp_ref, o_ref = rest
    else:
        (o_ref,) = rest
    nt = (((1,), (1,)), ((), ()))
    q = jnp.concatenate(
        [q_ref[0, :, g * HEAD_DIM:(g + 1) * HEAD_DIM] for g in range(ATTN_GROUP)], axis=0)
    n_keys = k_ref.shape[1]
    kc = min(ATTN_KEY_CHUNK, n_keys)
    chunks = [(kp_ref, vp_ref, 0, kp_ref.shape[1])] if has_prefix else []
    chunks += [(k_ref, v_ref, c * kc, kc) for c in range(n_keys // kc)]

    def scores(chunk):
        kr, _, start, size = chunk
        return lax.dot_general(q, kr[0, start:start + size, :], nt, preferred_element_type=F32)

    m = jnp.full((q.shape[0], 1), -1e30, F32)
    l = jnp.zeros((q.shape[0], 1), F32)
    o = jnp.zeros((q.shape[0], HEAD_DIM), F32)
    s_next = scores(chunks[0])
    for ci, (_, vr, start, size) in enumerate(chunks):
        s = s_next
        if ci + 1 < len(chunks):
            s_next = scores(chunks[ci + 1])
        m_new = jnp.maximum(m, jnp.max(s, axis=-1, keepdims=True))
        alpha = jnp.exp(m - m_new)
        p = jnp.exp(s - m_new)
        l = alpha * l + jnp.sum(p, axis=-1, keepdims=True)
        o = alpha * o + jnp.dot(p.astype(BF16), vr[0, start:start + size, :], preferred_element_type=F32)
        m = m_new
    o = o * (1.0 / l)
    for g in range(ATTN_GROUP):
        o_ref[0, :, g * HEAD_DIM:(g + 1) * HEAD_DIM] = o[g * tq:(g + 1) * tq].astype(o_ref.dtype)


def _attention(qkvu, prefix, *, tq):
    B, L, _ = qkvu.shape
    tq = min(tq, L)
    gw = ATTN_GROUP * HEAD_DIM
    k_blk = ATTN_HEADS
    v_blk = ATTN_HEADS + ATTN_KV_HEADS
    in_specs = [
        pl.BlockSpec((1, tq, gw), lambda b, h, i: (b, i, h)),
        pl.BlockSpec((1, L, HEAD_DIM), lambda b, h, i: (b, 0, k_blk + h)),
        pl.BlockSpec((1, L, HEAD_DIM), lambda b, h, i: (b, 0, v_blk + h)),
    ]
    args = [qkvu, qkvu, qkvu]
    if prefix is not None:
        Lp = prefix.shape[1]
        in_specs += [
            pl.BlockSpec((1, Lp, HEAD_DIM), lambda b, h, i: (b, 0, k_blk + h)),
            pl.BlockSpec((1, Lp, HEAD_DIM), lambda b, h, i: (b, 0, v_blk + h)),
        ]
        args += [prefix, prefix]
    return pl.pallas_call(
        functools.partial(_attn_kernel, tq=tq, has_prefix=prefix is not None),
        grid=(B, ATTN_KV_HEADS, L // tq),
        in_specs=in_specs,
        out_specs=pl.BlockSpec((1, tq, gw), lambda b, h, i: (b, i, h)),
        out_shape=jax.ShapeDtypeStruct((B, L, ATTN_HEADS * HEAD_DIM), BF16),
        compiler_params=_cparams(("parallel", "parallel", "arbitrary")),
        name="gqa_attention",
    )(*args)


S5_SLAB_GROUPS = LANES // S5_GROUP
S5_KD = S5_CHUNK * LANES
S5_PAIR_COLS = 4 * LANES


def _s5_end_kernel(u_ref, w_ref, e_ref):
    e_ref[...] = jnp.dot(u_ref[0], w_ref[0], preferred_element_type=F32)


def _s5_scan_kernel(e_ref, cf_ref, cb_ref, h_ref, *, batch, nc_ctx, nc_all):
    cps = SUBLANES // batch
    g_ctx, g_all = nc_ctx // cps, nc_all // cps

    def advance(hr, hi, er, ei, c_ref):
        ar, ai = c_ref[0:1, :], c_ref[1:2, :]
        return ar * hr - ai * hi + er, ar * hi + ai * hr + ei

    def sweep(r, state, col0, c_ref, order):
        e8 = e_ref[pl.ds(r, SUBLANES), col0:col0 + 2 * LANES]
        hr, hi = state
        out_r, out_i = [None] * cps, [None] * cps
        for c in order:
            out_r[c], out_i[c] = hr, hi
            rows = slice(c * batch, (c + 1) * batch)
            hr, hi = advance(hr, hi, e8[rows, :LANES], e8[rows, LANES:], c_ref)
        h_ref[pl.ds(r, SUBLANES), col0:col0 + LANES] = jnp.concatenate(out_r, axis=0).astype(h_ref.dtype)
        h_ref[pl.ds(r, SUBLANES), col0 + LANES:col0 + 2 * LANES] = (
            jnp.concatenate(out_i, axis=0).astype(h_ref.dtype))
        return hr, hi

    def body(i, carry):
        fwd, bwd = carry
        fwd = sweep(pl.multiple_of(i * SUBLANES, SUBLANES), fwd, 0, cf_ref, range(cps))
        gi = jnp.where(i < g_ctx, g_ctx - 1 - i, g_all - 1 - (i - g_ctx))
        bwd = sweep(pl.multiple_of(gi * SUBLANES, SUBLANES), bwd, 2 * LANES, cb_ref, reversed(range(cps)))
        return fwd, bwd

    zero = jnp.zeros((batch, LANES), F32)
    lax.fori_loop(0, g_all, body, ((zero, zero), (zero, zero)))


def _s5_out_kernel(u_ref, us_ref, lag_ref, h_ref, win_ref, d_ref, y_ref, tz_ref):
    T = S5_CHUNK
    t_blocks = tz_ref.shape[1] // LANES
    t0 = pl.program_id(1) * t_blocks
    for tt in range(t_blocks):
        for s in range(T):
            tz_ref[s * LANES:(s + 1) * LANES, tt * LANES:(tt + 1) * LANES] = lag_ref[0, t0 + tt - s + (T - 1)]
    y = jnp.dot(u_ref[0], tz_ref[...], preferred_element_type=F32)
    y += jnp.dot(h_ref[...].astype(BF16), win_ref[0], preferred_element_type=F32)
    y += us_ref[0].astype(F32) * d_ref[0]
    y_ref[0] = y.astype(y_ref.dtype)


def _s5_tables(lam_re, lam_im, log_dt, b_re, b_im, c_re, c_im, d_skip):
    T, C, P = S5_CHUNK, S5_GROUP, S5_STATE
    G = lam_re.shape[1]
    hi = lax.Precision.HIGHEST
    lam = lax.complex(jnp.minimum(lam_re.astype(F32), -1e-4), lam_im.astype(F32))
    dt = jnp.exp(log_dt.astype(F32))[..., None]
    lam_dt = lam * dt
    lam_bar = jnp.exp(lam_dt)
    bmat = lax.complex(b_re.astype(F32), b_im.astype(F32))
    b_bar = ((lam_bar - 1.0) / lam)[..., None] * bmat
    cmat = lax.complex(c_re.astype(F32), c_im.astype(F32))
    steps = jnp.arange(T + 1, dtype=F32)
    pw = jnp.exp(lam_dt[:, None] * steps[None, :, None, None])

    kern = jnp.real(jnp.einsum('zgcp,zjgp,zgpd->zjgcd', cmat, pw[:, :T], b_bar, precision=hi))

    def end_w(p_sel, bb):
        w = p_sel[..., None] * bb[None]
        w = jnp.concatenate([jnp.real(w), jnp.imag(w)], axis=2)
        return w.transpose(1, 0, 3, 2).reshape(G, T * C, 2 * P)

    w_end = jnp.concatenate([end_w(pw[0, :T][::-1], b_bar[0]), end_w(pw[1, :T], b_bar[1])], axis=-1)

    def in_w(p_sel, cm):
        z = cm[None] * p_sel[:, :, None, :]
        w = jnp.concatenate([jnp.real(z), -jnp.imag(z)], axis=-1)
        return w.transpose(1, 3, 0, 2).reshape(G, 2 * P, T * C)

    w_in_f = in_w(pw[0, 1:], cmat[0])
    w_in_b = in_w(pw[1, 1:][::-1], cmat[1])

    def coef(z):
        return jnp.stack([jnp.real(z).reshape(G * P), jnp.imag(z).reshape(G * P)])

    sg = S5_SLAB_GROUPS
    ngb, npair = G // sg, sg // 2
    grp = np.arange(sg)
    tok_cols_grp = np.tile(np.repeat(grp, C), T)
    state_cols_grp = (2 * np.arange(npair)[:, None, None, None] + np.arange(2)[None, None, :, None]
                      + np.zeros((1, 4, 1, P), np.int64)).reshape(-1)
    rows_gd = np.repeat(grp, C)

    lag_k = jnp.concatenate([kern[1][1:][::-1], (kern[0][0] + kern[1][0])[None], kern[0][1:]], axis=0)
    lag_src = lag_k.reshape(2 * T - 1, ngb, sg, C, C).transpose(1, 0, 2, 4, 3).reshape(ngb * (2 * T - 1), LANES, C)
    rep_c = np.tile(np.eye(C, dtype=np.float32), (1, sg))
    mask_c = (rows_gd[:, None] == np.repeat(grp, C)[None, :]).astype(np.float32)[None]
    lag_tiles = _spread(lag_src, rep_c, mask_c, lambda n: 0).reshape(ngb, 2 * T - 1, LANES, LANES)

    end_src = w_end.reshape(ngb, sg, T, C, 4 * P).transpose(0, 2, 1, 3, 4).reshape(ngb * T, LANES, 4 * P)
    rep_state = np.tile(np.eye(4 * P, dtype=np.float32).reshape(4 * P, 1, 4, 1, P), (1, npair, 1, 2, 1))
    rep_state = rep_state.reshape(4 * P, npair * S5_PAIR_COLS)
    mask_state = (rows_gd[:, None] == state_cols_grp[None, :]).astype(np.float32)[None]
    end_slab = _spread(end_src, rep_state, mask_state, lambda n: 0).reshape(ngb, S5_KD, npair * S5_PAIR_COLS)

    w_in = jnp.concatenate([w_in_f.reshape(G, 2, P, T * C), w_in_b.reshape(G, 2, P, T * C)], axis=1)
    in_src = w_in.reshape(ngb, npair, 2, 4, P, T * C).transpose(0, 1, 3, 2, 4, 5)
    in_src = in_src.reshape(ngb * npair * 4, 2 * P, T * C)
    rep_tok = np.tile(np.eye(T * C, dtype=np.float32).reshape(T * C, T, 1, C), (1, 1, sg, 1)).reshape(T * C, S5_KD)
    row_grp = 2 * np.arange(npair)[:, None, None] + np.repeat(np.arange(2), P)[None, :, None]
    mask_tok = (row_grp == tok_cols_grp[None, None, :]).astype(np.float32)
    in_slab = _spread(in_src, rep_tok, mask_tok, lambda n: (n // 4) % npair)
    in_slab = in_slab.reshape(ngb, npair * S5_PAIR_COLS, S5_KD)

    d_slab = jnp.tile(d_skip.astype(F32).reshape(ngb, 1, LANES), (1, 1, T))
    return lag_tiles, end_slab, in_slab, coef(pw[0, T]), coef(pw[1, T]), d_slab


SPREAD_TILES = 4


def _spread_kernel(src_ref, rep_ref, mask_ref, o_ref):
    for i in range(SPREAD_TILES):
        o_ref[i] = (jnp.dot(src_ref[i].astype(BF16), rep_ref[...], preferred_element_type=F32)
                    * mask_ref[0]).astype(o_ref.dtype)


def _spread(src, rep, mask, mask_index):
    n, rows, k = src.shape
    width = rep.shape[1]
    st = SPREAD_TILES
    assert n % st == 0
    return pl.pallas_call(
        _spread_kernel,
        grid=(n // st,),
        in_specs=[
            pl.BlockSpec((st, rows, k), lambda i: (i, 0, 0)),
            pl.BlockSpec((k, width), lambda i: (0, 0)),
            pl.BlockSpec((1, rows, width), lambda i: (mask_index(i * st), 0, 0)),
        ],
        out_specs=pl.BlockSpec((st, rows, width), lambda i: (i, 0, 0)),
        out_shape=jax.ShapeDtypeStruct((n, rows, width), BF16),
        compiler_params=_cparams(("parallel",)),
        name="s5_spread_operator",
    )(src, jnp.asarray(rep, BF16), jnp.asarray(mask, F32))


def _s5(u_ctx, u_lat, tables):
    tz, w_end, w_in, cf, cb, d_slab = tables
    T = S5_CHUNK
    B, Lc, W = u_ctx.shape
    L = u_lat.shape[1]
    ngb = W // LANES
    nc_ctx, nc_all = Lc // T, (Lc + L) // T
    cps = SUBLANES // B
    assert SUBLANES % B == 0 and nc_ctx % cps == 0 and (nc_all - nc_ctx) % cps == 0
    M = nc_all * B
    kd, sw = S5_KD, w_end.shape[-1]
    n_half = 4
    hw = kd // n_half
    u = jnp.concatenate([u_ctx, u_lat], axis=1)
    u = u.reshape(B, nc_all, T, ngb, LANES).transpose(3, 1, 0, 2, 4).reshape(ngb, M, kd)

    e = pl.pallas_call(
        _s5_end_kernel,
        grid=(ngb, n_half),
        in_specs=[
            pl.BlockSpec((1, M, kd), lambda g, j: (g, 0, 0)),
            pl.BlockSpec((1, kd, sw // n_half), lambda g, j: (g, 0, j)),
        ],
        out_specs=pl.BlockSpec((M, sw // n_half), lambda g, j: (0, g * n_half + j)),
        out_shape=jax.ShapeDtypeStruct((M, ngb * sw), F32),
        compiler_params=_cparams(("parallel", "parallel")),
        name="s5_chunk_end_states",
    )(u, w_end)

    h = pl.pallas_call(
        functools.partial(_s5_scan_kernel, batch=B, nc_ctx=nc_ctx, nc_all=nc_all),
        grid=(ngb * sw // S5_PAIR_COLS,),
        in_specs=[pl.BlockSpec((M, S5_PAIR_COLS), lambda j: (0, j))]
        + [pl.BlockSpec((2, LANES), lambda j: (0, j))] * 2,
        out_specs=pl.BlockSpec((M, S5_PAIR_COLS), lambda j: (0, j)),
        out_shape=jax.ShapeDtypeStruct((M, ngb * sw), F32),
        compiler_params=_cparams(("parallel",)),
        name="s5_chunk_scan",
    )(e, cf, cb)

    y = pl.pallas_call(
        _s5_out_kernel,
        grid=(ngb, n_half),
        in_specs=[
            pl.BlockSpec((1, M, kd), lambda g, j: (g, 0, 0)),
            pl.BlockSpec((1, M, hw), lambda g, j: (g, 0, j)),
            pl.BlockSpec((1,) + tz.shape[1:], lambda g, j: (g, 0, 0, 0)),
            pl.BlockSpec((M, sw), lambda g, j: (0, g)),
            pl.BlockSpec((1, sw, hw), lambda g, j: (g, 0, j)),
            pl.BlockSpec((1, 1, hw), lambda g, j: (g, 0, j)),
        ],
        out_specs=pl.BlockSpec((1, M, hw), lambda g, j: (g, 0, j)),
        out_shape=jax.ShapeDtypeStruct((ngb, M, kd), BF16),
        scratch_shapes=[pltpu.VMEM((kd, hw), BF16)],
        compiler_params=_cparams(("parallel", "parallel")),
        name="s5_chunk_outputs",
    )(u, u, tz, h, w_in, d_slab)

    y = y.reshape(ngb, nc_all, B, T, LANES).transpose(2, 1, 3, 0, 4).reshape(B, Lc + L, W)
    return y[:, :Lc], y[:, Lc:]


def _glu_kernel(y_ref, w_ref, b_ref, o_ref):
    y = y_ref[0].astype(F32)
    z = jnp.dot(jax.nn.gelu(y).astype(BF16), w_ref[...], preferred_element_type=F32) + b_ref[...]
    n = o_ref.shape[-1]
    o_ref[0] = (z[:, :n] * jax.nn.sigmoid(z[:, n:])).astype(o_ref.dtype)


def _glu(y, w, b, *, tm):
    G, T, W = y.shape
    tm = min(tm, T)
    return pl.pallas_call(
        _glu_kernel,
        grid=(G, T // tm),
        in_specs=[
            pl.BlockSpec((1, tm, W), lambda g, i: (g, i, 0)),
            pl.BlockSpec((W, 2 * W), lambda g, i: (0, 0)),
            pl.BlockSpec((1, 2 * W), lambda g, i: (0, 0)),
        ],
        out_specs=pl.BlockSpec((1, tm, W), lambda g, i: (g, i, 0)),
        out_shape=jax.ShapeDtypeStruct((G, T, W), BF16),
        compiler_params=_cparams(("parallel", "parallel")),
        name="s5_glu",
    )(y, w, b.reshape(1, 2 * W))


def _out_res_kernel(*refs, widths):
    n = len(widths)
    lhs = refs[:n]
    w_ref, x_ref, gate_ref, g_ref, o_ref = refs[n:]
    y = None
    off = 0
    for r, k in zip(lhs, widths):
        part = jnp.dot(r[0], w_ref[off:off + k, :], preferred_element_type=F32)
        y = part if y is None else y + part
        off += k
    o_ref[0] = x_ref[0] + gate_ref[0] * (_rms(y) * g_ref[...])


def _out_res(parts, w, x, gate, g, *, tm):
    G, T, D = x.shape
    tm = min(tm, T)
    widths = tuple(p.shape[-1] for p in parts)
    K = sum(widths)
    in_specs = [pl.BlockSpec((1, tm, k), lambda b, i: (b, i, 0)) for k in widths]
    in_specs += [
        pl.BlockSpec((K, D), lambda b, i: (0, 0)),
        pl.BlockSpec((1, tm, D), lambda b, i: (b, i, 0)),
        pl.BlockSpec((1, 1, D), _vec_map(gate, 2)),
        pl.BlockSpec((1, D), lambda b, i: (0, 0)),
    ]
    return pl.pallas_call(
        functools.partial(_out_res_kernel, widths=widths),
        grid=(G, T // tm),
        in_specs=in_specs,
        out_specs=pl.BlockSpec((1, tm, D), lambda b, i: (b, i, 0)),
        out_shape=jax.ShapeDtypeStruct((G, T, D), F32),
        compiler_params=_cparams(("parallel", "parallel")),
        name="out_proj_residual",
    )(*parts, w, x, gate, g.reshape(1, D))


HALO = BF16_ROWS


def _ffn_kernel(x_ref, xp_ref, xn_ref, g2_ref, sh_ref, sc_ref, wa_ref, wv_ref, cwa_ref, cwv_ref,
                cba_ref, cbv_ref, wd_ref, gate_ref, g3_ref, o_ref, h_ref, acc_ref, ua_ref, uv_ref, *, tm):
    i = pl.program_id(1)
    f = pl.program_id(2)

    def normmod(x):
        h = _rms(x) * g2_ref[...]
        return h * (1.0 + sc_ref[0]) + sh_ref[0]

    @pl.when(f == 0)
    def _():
        zeros = jnp.zeros((HALO - SUBLANES, x_ref.shape[-1]), F32)
        hp = jnp.where(i == 0, 0.0, normmod(xp_ref[0]))
        hn = jnp.where(i == pl.num_programs(1) - 1, 0.0, normmod(xn_ref[0]))
        h_ref[0:HALO] = jnp.concatenate([zeros, hp], axis=0).astype(BF16)
        h_ref[HALO:HALO + tm] = normmod(x_ref[0]).astype(BF16)
        h_ref[HALO + tm:] = jnp.concatenate([hn, zeros], axis=0).astype(BF16)
        acc_ref[...] = jnp.zeros_like(acc_ref)

    h = h_ref[...]
    ua_ref[...] = jnp.dot(h, wa_ref[...], preferred_element_type=F32)
    uv_ref[...] = jnp.dot(h, wv_ref[...], preferred_element_type=F32)

    def conv(u_ref, cw_ref, cb_ref):
        return (u_ref[HALO - 1:HALO - 1 + tm] * cw_ref[0:1, :] + u_ref[HALO:HALO + tm] * cw_ref[1:2, :]
                + u_ref[HALO + 1:HALO + 1 + tm] * cw_ref[2:3, :] + cb_ref[...])

    a = conv(ua_ref, cwa_ref, cba_ref)
    v = conv(uv_ref, cwv_ref, cbv_ref)
    gated = (a * jax.nn.sigmoid(a) * v).astype(BF16)
    acc_ref[...] += jnp.dot(gated, wd_ref[...], preferred_element_type=F32)

    @pl.when(f == pl.num_programs(2) - 1)
    def _():
        o_ref[0] = x_ref[0] + gate_ref[0] * (_rms(acc_ref[...]) * g3_ref[...])


def _conv_ffn(x, g2, shift, scale, w_up, conv_w, conv_b, w_down, gate, g3, *, tm, tf):
    G, T, D = x.shape
    F = w_down.shape[0]
    tm = min(tm, T)
    nf = F // tf
    n_i = T // tm
    rb = tm // SUBLANES
    last_rb = T // SUBLANES - 1
    conv_b = conv_b.reshape(1, 2 * F)
    return pl.pallas_call(
        functools.partial(_ffn_kernel, tm=tm),
        grid=(G, n_i, nf),
        in_specs=[
            pl.BlockSpec((1, tm, D), lambda b, i, f: (b, i, 0)),
            pl.BlockSpec((1, SUBLANES, D), lambda b, i, f: (b, jnp.maximum(i * rb - 1, 0), 0)),
            pl.BlockSpec((1, SUBLANES, D), lambda b, i, f: (b, jnp.minimum((i + 1) * rb, last_rb), 0)),
            pl.BlockSpec((1, D), lambda b, i, f: (0, 0)),
            pl.BlockSpec((1, 1, D), _vec_map(shift, 3)),
            pl.BlockSpec((1, 1, D), _vec_map(scale, 3)),
            pl.BlockSpec((D, tf), lambda b, i, f: (0, f)),
            pl.BlockSpec((D, tf), lambda b, i, f: (0, nf + f)),
            pl.BlockSpec((3, tf), lambda b, i, f: (0, f)),
            pl.BlockSpec((3, tf), lambda b, i, f: (0, nf + f)),
            pl.BlockSpec((1, tf), lambda b, i, f: (0, f)),
            pl.BlockSpec((1, tf), lambda b, i, f: (0, nf + f)),
            pl.BlockSpec((tf, D), lambda b, i, f: (f, 0)),
            pl.BlockSpec((1, 1, D), _vec_map(gate, 3)),
            pl.BlockSpec((1, D), lambda b, i, f: (0, 0)),
        ],
        out_specs=pl.BlockSpec((1, tm, D), lambda b, i, f: (b, i, 0)),
        out_shape=jax.ShapeDtypeStruct((G, T, D), F32),
        scratch_shapes=[
            pltpu.VMEM((tm + 2 * HALO, D), BF16),
            pltpu.VMEM((tm, D), F32),
            pltpu.VMEM((tm + 2 * HALO, tf), F32),
            pltpu.VMEM((tm + 2 * HALO, tf), F32),
        ],
        compiler_params=_cparams(("parallel", "parallel", "arbitrary")),
        name="conv_ffn",
    )(x, x, x, g2.reshape(1, D), shift, scale, w_up, w_up, conv_w, conv_w, conv_b, conv_b, w_down,
      gate, g3.reshape(1, D))


GLA_BLOCKS = HGRN_CHUNK // SUBLANES


def _gla_operands(z, v, q, lb, rev):
    nb, rb = GLA_BLOCKS, SUBLANES
    order = list(range(nb))[::-1] if rev else list(range(nb))
    scan_of = {b: j for j, b in enumerate(order)}
    pos = lax.broadcasted_iota(jnp.int32, (rb, HGRN_DK), 0)
    if rev:
        pos = rb - 1 - pos

    def prev(x, k):
        return pltpu.roll(x, (rb - k) if rev else k, 0)

    def nxt(x, k):
        return pltpu.roll(x, k if rev else (rb - k), 0)

    def blocks(x):
        return [x[rb * b:rb * (b + 1)] for b in range(nb)]

    def rows(bl):
        return jnp.concatenate(bl, axis=0)

    f = lb + (1.0 - lb) * jax.nn.sigmoid(z)
    fb = blocks(f)
    kb = blocks(1.0 - f)

    hb = rb // 2
    hpos = pos & (hb - 1)
    lo_last = hb if rev else hb - 1
    hi_first = hb - 1 if rev else hb
    p4, s4, p8, s8, bt = [], [], [], [], []
    for b in range(nb):
        x = y = fb[b]
        for k in (1, 2):
            x = x * jnp.where(hpos >= k, prev(x, k), 1.0)
            y = y * jnp.where(hpos <= hb - 1 - k, nxt(y, k), 1.0)
        lo_tot = x[lo_last:lo_last + 1, :]
        hi_tot = y[hi_first:hi_first + 1, :]
        sfx = jnp.where(hpos <= hb - 2, nxt(y, 1), 1.0)
        p4.append(x)
        s4.append(sfx)
        p8.append(x * jnp.where(pos >= hb, lo_tot, 1.0))
        s8.append(sfx * jnp.where(pos < hb, hi_tot, 1.0))
        bt.append(lo_tot * hi_tot)
    bts = [bt[order[j]] for j in range(nb)]
    ones = jnp.ones_like(bts[0])
    before = [ones]
    for j in range(1, nb):
        before.append(before[j - 1] * bts[j - 1])
    after = [ones] * nb
    for j in range(nb - 2, -1, -1):
        after[j] = after[j + 1] * bts[j + 1]
    ftot = before[nb - 1] * bts[nb - 1]

    kbase = [kb[b] * s8[b] for b in range(nb)]
    ops = {"v": v, "ftot": ftot,
           "kdec": rows([kbase[b] * after[scan_of[b]] for b in range(nb)]).astype(BF16)}
    if q is None:
        return ops

    qb = blocks(q)
    q8 = [qb[b] * p8[b] for b in range(nb)]
    ops["q_state"] = rows([q8[b] * before[scan_of[b]] for b in range(nb)]).astype(BF16)
    ops["q_far"] = rows(q8).astype(BF16)
    ops["q_near"] = q.astype(BF16)

    vb = blocks(v.astype(F32))
    kcols, vcols = [], []
    chain = {}
    for d in range(1, nb):
        for j in range(nb - d):
            chain[j] = kbase[order[j]] if d == 1 else chain[j] * bts[j + d - 1]
            kcols.append(chain[j])
            vcols.append(vb[order[j]])
    ops["k_far"] = rows(kcols).astype(BF16)
    ops["v_far"] = rows(vcols).astype(BF16)

    ops["q_cross"] = rows([qb[b] * p4[b] for b in range(nb)]).astype(BF16)
    ops["k_cross"] = rows([kb[b] * s4[b] for b in range(nb)]).astype(BF16)

    kvar = []
    cur = kb
    for d in range(hb):
        if d:
            cur = [cur[b] * nxt(fb[b], d) for b in range(nb)]
        kvar.append(rows(cur).astype(BF16))
    ops["k_near"] = jnp.concatenate(kvar, axis=0)
    return ops


def _gla_first_dots(ops, st):
    nt = (((1,), (1,)), ((), ()))
    st_new = st * ops["ftot"] + lax.dot_general(ops["v"], ops["kdec"], (((0,), (0,)), ((), ())),
                                                preferred_element_type=F32)
    if "q_near" not in ops:
        return st_new, None
    o = lax.dot_general(ops["q_state"], st.astype(BF16), nt, preferred_element_type=F32)
    far = lax.dot_general(ops["q_far"], ops["k_far"], nt, preferred_element_type=F32)
    cross = lax.dot_general(ops["q_cross"], ops["k_cross"], nt, preferred_element_type=F32)
    near = lax.dot_general(ops["q_near"], ops["k_near"], nt, preferred_element_type=F32)
    return st_new, (o, far, cross, near)


def _gla_second_dots(ops, scores, masks):
    o, far, cross, near = scores
    m_far, m_cross, m_near = masks
    o = o + jnp.dot((far * m_far).astype(BF16), ops["v_far"], preferred_element_type=F32)
    o = o + jnp.dot((cross * m_cross).astype(BF16), ops["v"], preferred_element_type=F32)
    v_near = jnp.concatenate([ops["v"]] * (near.shape[1] // HGRN_CHUNK), axis=0)
    return o + jnp.dot((near * m_near).astype(BF16), v_near, preferred_element_type=F32)


def _operand_rows():
    T, nb, rb = HGRN_CHUNK, GLA_BLOCKS, SUBLANES
    far = rb * nb * (nb - 1) // 2
    sizes = [("kdec", T), ("q_state", T), ("q_far", T), ("q_near", T), ("q_cross", T), ("k_cross", T),
             ("v", T), ("k_far", far), ("v_far", far), ("k_near", (rb // 2) * T)]
    rows, off = {}, 0
    for name, n in sizes:
        rows[name] = (off, n)
        off += n
    return rows, off


GLA_OPERAND_ROWS, GLA_OPERAND_TOTAL = _operand_rows()


def _gla_mask_tables():
    T, nb, rb = HGRN_CHUNK, GLA_BLOCKS, SUBLANES
    hb = rb // 2
    t = np.arange(T)
    far, cross, near = [], [], []
    for rev in (False, True):
        sblk = (nb - 1 - t // rb) if rev else t // rb
        spos = (rb - 1 - t % rb) if rev else t % rb
        same_block = t[:, None] // rb == t[None, :] // rb
        cols = [(d, j) for d in range(1, nb) for j in range(nb - d)]
        m_far = np.zeros((T, len(cols) * rb), np.float32)
        for c, (d, j) in enumerate(cols):
            m_far[sblk == j + d, c * rb:(c + 1) * rb] = 1.0
        m_cross = same_block & (spos[:, None] >= hb) & (spos[None, :] < hb)
        same_half = same_block & (spos[:, None] // hb == spos[None, :] // hb)
        m_near = np.zeros((T, hb * T), np.float32)
        for d in range(hb):
            m_near[:, d * T:(d + 1) * T] = same_half & (spos[:, None] - spos[None, :] == d)
        far.append(m_far)
        cross.append(m_cross.astype(np.float32))
        near.append(m_near)
    return tuple(jnp.asarray(np.stack(m)) for m in (far, cross, near))


def _gla_kernel(q_ref, zf_ref, zb_ref, v_ref, g_ref, czf_ref, czb_ref, cv_ref, lb_ref, ng_ref,
                mfar_ref, mcross_ref, mnear_ref, o_ref, oacc_ref, ops_a_ref, ft_a_ref, ops_b_ref, ft_b_ref,
                *, n_lat, n_ctx, hp):
    T, dk = HGRN_CHUNK, HGRN_DK
    jobs = [(h, rev) for h in range(hp) for rev in (False, True)]
    zero_state = jnp.zeros((dk, dk), F32)

    def cols(h):
        return slice(h * dk, (h + 1) * dk)

    def start(rev, i, n):
        r = ((n - 1 - i) if rev else i) * T
        return r if isinstance(r, int) else pl.multiple_of(r, T)

    def lb_row(h, rev):
        return lb_ref[int(rev):int(rev) + 1, cols(h)]

    def prep_ctx(i):
        ops = []
        for h, rev in jobs:
            r = start(rev, i, n_ctx)
            z = (czb_ref if rev else czf_ref)[0, pl.ds(r, T), cols(h)].astype(F32)
            ops.append(_gla_operands(z, cv_ref[0, pl.ds(r, T), cols(h)], None, lb_row(h, rev), rev))
        return tuple(ops)

    def prep_lat(i):
        ops = []
        for h, rev in jobs:
            r = start(rev, i, n_lat)
            qz = q_ref[0, pl.ds(r, T), cols(h)].astype(F32)
            z = (zb_ref if rev else zf_ref)[0, pl.ds(r, T), cols(h)].astype(F32)
            ops.append(_gla_operands(z, v_ref[0, pl.ds(r, T), cols(h)], qz * jax.nn.sigmoid(qz),
                                     lb_row(h, rev), rev))
        return tuple(ops)

    def stash(ops, buf):
        ops_ref, ft_ref = buf
        for j, o in enumerate(ops):
            for name, (off, n) in GLA_OPERAND_ROWS.items():
                ops_ref[j, off:off + n, :] = o[name]
            ft_ref[j] = jnp.broadcast_to(o["ftot"], (SUBLANES, dk))

    def fetch(buf):
        ops_ref, ft_ref = buf
        out = []
        for j in range(len(jobs)):
            o = {name: ops_ref[j, off:off + n, :] for name, (off, n) in GLA_OPERAND_ROWS.items()}
            o["ftot"] = ft_ref[j, 0:1, :]
            out.append(o)
        return out

    def ctx_step(i, sts):
        return tuple(_gla_first_dots(o, st)[0] for o, st in zip(prep_ctx(i), sts))

    def lat_step(i, sts, second_half, cur, nxt):
        ops = fetch(cur)
        firsts = [_gla_first_dots(o, st) for o, st in zip(ops, sts)]
        stash(prep_lat(jnp.minimum(i + 1, n_lat - 1)), nxt)
        outs = [_gla_second_dots(o, f[1], tuple(m[int(rev)] for m in (mfar_ref, mcross_ref, mnear_ref)))
                for o, f, (_, rev) in zip(ops, firsts, jobs)]
        for (h, rev), o in zip(jobs, outs):
            r = start(rev, i, n_lat)
            if second_half:
                o = o + oacc_ref[pl.ds(r, T), cols(h)]
                gz = g_ref[0, pl.ds(r, T), cols(h)].astype(F32)
                o = _rms(o) * ng_ref[...] * (gz * jax.nn.sigmoid(gz))
                o_ref[0, pl.ds(r, T), cols(h)] = o.astype(o_ref.dtype)
            else:
                oacc_ref[pl.ds(r, T), cols(h)] = o
        return tuple(f[0] for f in firsts)

    buf_a, buf_b = (ops_a_ref, ft_a_ref), (ops_b_ref, ft_b_ref)

    def lat_pair(k, sts, second_half):
        sts = lat_step(2 * k, sts, second_half, buf_a, buf_b)
        return lat_step(2 * k + 1, sts, second_half, buf_b, buf_a)

    sts = lax.fori_loop(0, n_ctx, ctx_step, (zero_state,) * len(jobs))
    stash(prep_lat(0), buf_a)
    quarter = n_lat // 4
    sts = lax.fori_loop(0, quarter, functools.partial(lat_pair, second_half=False), sts)
    lax.fori_loop(quarter, 2 * quarter, functools.partial(lat_pair, second_half=True), sts)


GLA_HEADS_PER_STEP = 2


def _gla(p_lat, p_ctx, lb, norm_g, n_heads):
    B, L, _ = p_lat.shape
    Lc = p_ctx.shape[1]
    hp = GLA_HEADS_PER_STEP
    steps = n_heads // hp
    dv = HGRN_DK
    w = hp * dv

    def col(group):
        return lambda b, h: (b, 0, group * steps + h)

    lat_specs = [pl.BlockSpec((1, L, w), col(gidx)) for gidx in range(5)]
    ctx_specs = [pl.BlockSpec((1, Lc, w), col(gidx)) for gidx in (1, 2, 3)]
    masks = _gla_mask_tables()
    n_lat = L // HGRN_CHUNK
    assert n_lat % 4 == 0 and n_heads % hp == 0
    operand_bufs = [pltpu.VMEM((2 * hp, GLA_OPERAND_TOTAL, dv), BF16), pltpu.VMEM((2 * hp, SUBLANES, dv), F32)]
    return pl.pallas_call(
        functools.partial(_gla_kernel, n_lat=n_lat, n_ctx=Lc // HGRN_CHUNK, hp=hp),
        grid=(B, steps),
        in_specs=lat_specs + ctx_specs + [
            pl.BlockSpec((2, w), lambda b, h: (0, h)),
            pl.BlockSpec((1, dv), lambda b, h: (0, 0)),
        ] + [pl.BlockSpec(m.shape, lambda b, h: (0, 0, 0)) for m in masks],
        out_specs=pl.BlockSpec((1, L, w), lambda b, h: (b, 0, h)),
        out_shape=jax.ShapeDtypeStruct((B, L, n_heads * dv), BF16),
        scratch_shapes=[pltpu.VMEM((L, w), F32)] + operand_bufs * 2,
        compiler_params=_cparams(("parallel", "parallel")),
        name="hgrn2_bidirectional",
    )(*([p_lat] * 5), *([p_ctx] * 3), lb, norm_g.reshape(1, dv), *masks)


def _rope_tables(n_tokens):
    rows = n_tokens // GRID_W
    row = jnp.repeat(jnp.arange(rows, dtype=F32), GRID_W)
    colp = jnp.tile(jnp.arange(GRID_W, dtype=F32), rows)
    inv = ROPE_THETA ** (-jnp.arange(AXIS_FREQS, dtype=F32) / AXIS_FREQS)
    ang = jnp.stack([row[:, None] * inv, colp[:, None] * inv], axis=1)
    cos, sin = jnp.cos(ang), jnp.sin(ang)
    zero = jnp.zeros_like(sin)
    full = lambda a, b: jnp.stack([a, b], axis=2).reshape(n_tokens, HEAD_DIM)
    return full(cos, cos), full(-sin, zero), full(zero, sin)


def _identity_rope(n_tokens):
    return (jnp.ones((n_tokens, HEAD_DIM), F32), jnp.zeros((n_tokens, HEAD_DIM), F32),
            jnp.zeros((n_tokens, HEAD_DIM), F32))


def kernel(x, c, ctx, c_ctx, mod_w, mod_b, norm_g, ab_w_in, ab_w_out, attn_q_norm, attn_k_norm,
           s5_lam_re, s5_lam_im, s5_log_dt, s5_b_re, s5_b_im, s5_c_re, s5_c_im, s5_d,
           s5_glu_w, s5_glu_b, c_w_in, c_w_out, hgrn_lb_logits, hgrn_norm,
           ffn_w_up, ffn_conv_w, ffn_conv_b, ffn_w_down):
    B, L, D = x.shape
    Lc = ctx.shape[1]
    depth = mod_w.shape[0]
    attn_w = ATTN_HEADS * HEAD_DIM
    kv_w = ATTN_KV_HEADS * HEAD_DIM
    qk_w = attn_w + kv_w

    lb_all = jnp.cumsum(jax.nn.softmax(hgrn_lb_logits.astype(F32), axis=0), axis=0)
    lb_all = lb_all - lb_all[:1]

    cvec = jnp.concatenate([c, c_ctx[None], jnp.zeros((SUBLANES - (B + 1) % SUBLANES, D), F32)], axis=0)
    mods = _modulation(cvec, mod_w, mod_b)

    rope_lat = _rope_tables(L)
    rope_ctx = _identity_rope(Lc)

    for l in range(depth):
        last = l == depth - 1
        m_lat = [mods[l, :B, k * D:(k + 1) * D][:, None, :] for k in range(6)]
        m_ctx = [mods[l, B:B + 1, k * D:(k + 1) * D][:, None, :] for k in range(6)]
        g = norm_g[l]
        w_up = ffn_w_up[l].astype(BF16)
        w_down = ffn_w_down[l].astype(BF16)
        if l % 2 == 0:
            e = l // 2
            w_in = ab_w_in[e].astype(BF16)
            gain = jnp.concatenate([jnp.tile(attn_q_norm[e] * (HEAD_DIM ** -0.5), ATTN_HEADS),
                                    jnp.tile(attn_k_norm[e], ATTN_KV_HEADS)]).reshape(1, qk_w)
            p_lat = _proj(x, g[0], m_lat[0], m_lat[1], w_in, tm=512, tn=qk_w, qk=(gain,) + rope_lat)
            p_ctx = _proj(ctx, g[0], m_ctx[0], m_ctx[1], w_in, tm=512, tn=qk_w, qk=(gain,) + rope_ctx)
            a_lat = _attention(p_lat, p_ctx, tq=128)
            tables = _s5_tables(s5_lam_re[e], s5_lam_im[e], s5_log_dt[e], s5_b_re[e], s5_b_im[e],
                                s5_c_re[e], s5_c_im[e], s5_d[e])
            u0 = qk_w + kv_w
            y_ctx, y_lat = _s5(p_ctx[..., u0:], p_lat[..., u0:], tables)
            glu_w = s5_glu_w[e].astype(BF16)
            w_out = ab_w_out[e].astype(BF16)
            s_lat = _glu(y_lat, glu_w, s5_glu_b[e], tm=512)
            x = _out_res([a_lat, s_lat], w_out, x, m_lat[2], g[1], tm=512)
            if not last:
                a_ctx = _attention(p_ctx, None, tq=128)
                s_ctx = _glu(y_ctx, glu_w, s5_glu_b[e], tm=512)
                ctx = _out_res([a_ctx, s_ctx], w_out, ctx, m_ctx[2], g[1], tm=512)
        else:
            o_idx = l // 2
            w_in = c_w_in[o_idx].astype(BF16)
            n_heads = c_w_out.shape[1] // HGRN_DK
            p_lat = _proj(x, g[0], m_lat[0], m_lat[1], w_in, tm=512, tn=1024)
            p_ctx = _proj(ctx, g[0], m_ctx[0], m_ctx[1], w_in, tm=512, tn=1024)
            o_lat = _gla(p_lat, p_ctx, lb_all[l], hgrn_norm[o_idx], n_heads)
            x = _out_res([o_lat], c_w_out[o_idx].astype(BF16), x, m_lat[2], g[1], tm=512)
            assert last, "context outputs of the HGRN2 mixer are only needed by a following layer"
        x = _conv_ffn(x, g[2], m_lat[3], m_lat[4], w_up, ffn_conv_w[l], ffn_conv_b[l], w_down,
                      m_lat[5], g[3], tm=512, tf=512)
        if not last:
            ctx = _conv_ffn(ctx, g[2], m_ctx[3], m_ctx[4], w_up, ffn_conv_w[l], ffn_conv_b[l], w_down,
                            m_ctx[5], g[3], tm=512, tf=512)
    return x
```

```python
import functools
import math

import jax
import jax.numpy as jnp
import numpy as np
from jax import lax
from jax.experimental import pallas as pl
from jax.experimental.pallas import tpu as pltpu

F32 = jnp.float32
BF16 = jnp.bfloat16
EPS = 1e-6

LANES = 128
SUBLANES = 8
BF16_ROWS = 16

HEAD_DIM = 128
ATTN_HEADS = 8
ATTN_KV_HEADS = 2
ATTN_GROUP = ATTN_HEADS // ATTN_KV_HEADS
GRID_W = 64
ROPE_THETA = 10000.0
AXIS_FREQS = HEAD_DIM // 4

S5_GROUP = 16
S5_STATE = 64
S5_CHUNK = 16

HGRN_DK = 128
HGRN_CHUNK = 64

VMEM_LIMIT = 56 * 1024 * 1024


def _cparams(sem):
    return pltpu.CompilerParams(dimension_semantics=sem, vmem_limit_bytes=VMEM_LIMIT)


def _vec_map(vec, nd_grid):
    shared = vec.shape[0] == 1
    if nd_grid == 2:
        return lambda g, i: (0 if shared else g, 0, 0)
    return lambda g, i, j: (0 if shared else g, 0, 0)


def _rms(x):
    return x * lax.rsqrt(jnp.mean(x * x, axis=-1, keepdims=True) + EPS)


def _mod_kernel(s_ref, w_ref, b_ref, o_ref):
    s = s_ref[...]
    a = s * jax.nn.sigmoid(s)
    o_ref[0] = jnp.dot(a.astype(BF16), w_ref[0].astype(BF16), preferred_element_type=F32) + b_ref[0]


def _modulation(cvec, mod_w, mod_b):
    depth, d, n = mod_w.shape
    rows = cvec.shape[0]
    tn = 1024
    return pl.pallas_call(
        _mod_kernel,
        grid=(depth, n // tn),
        in_specs=[
            pl.BlockSpec((rows, d), lambda l, j: (0, 0)),
            pl.BlockSpec((1, d, tn), lambda l, j: (l, 0, j)),
            pl.BlockSpec((1, 1, tn), lambda l, j: (l, 0, j)),
        ],
        out_specs=pl.BlockSpec((1, rows, tn), lambda l, j: (l, 0, j)),
        out_shape=jax.ShapeDtypeStruct((depth, rows, n), F32),
        compiler_params=_cparams(("parallel", "parallel")),
        name="modulation",
    )(cvec, mod_w, mod_b.reshape(depth, 1, n))


def _proj_kernel(x_ref, g_ref, sh_ref, sc_ref, w_ref, *rest, qk_heads):
    if qk_heads:
        gain_ref, cos_ref, s1_ref, s2_ref, o_ref, h_ref = rest
    else:
        o_ref, h_ref = rest
    j = pl.program_id(2)

    @pl.when(j == 0)
    def _():
        h = _rms(x_ref[0]) * g_ref[...]
        h = h * (1.0 + sc_ref[0]) + sh_ref[0]
        h_ref[...] = h.astype(BF16)

    acc = jnp.dot(h_ref[...], w_ref[...], preferred_element_type=F32)

    if not qk_heads:
        o_ref[0] = acc.astype(o_ref.dtype)
        return

    @pl.when(j == 0)
    def _():
        cos, s1, s2 = cos_ref[...], s1_ref[...], s2_ref[...]
        for hh in range(qk_heads):
            sl = slice(hh * HEAD_DIM, (hh + 1) * HEAD_DIM)
            y = _rms(acc[:, sl]) * gain_ref[:, sl]
            y = y * cos + pltpu.roll(y, HEAD_DIM - AXIS_FREQS, 1) * s1 + pltpu.roll(y, AXIS_FREQS, 1) * s2
            o_ref[0, :, sl] = y.astype(o_ref.dtype)

    @pl.when(j != 0)
    def _():
        o_ref[0] = acc.astype(o_ref.dtype)


def _proj(x, g, shift, scale, w, *, tm, tn, qk=None, out_dtype=BF16):
    G, T, D = x.shape
    N = w.shape[1]
    tm = min(tm, T)
    in_specs = [
        pl.BlockSpec((1, tm, D), lambda b, i, j: (b, i, 0)),
        pl.BlockSpec((1, D), lambda b, i, j: (0, 0)),
        pl.BlockSpec((1, 1, D), _vec_map(shift, 3)),
        pl.BlockSpec((1, 1, D), _vec_map(scale, 3)),
        pl.BlockSpec((D, tn), lambda b, i, j: (0, j)),
    ]
    args = [x, g.reshape(1, D), shift, scale, w]
    qk_heads = 0
    if qk is not None:
        gain, cos, s1, s2 = qk
        qk_heads = gain.shape[1] // HEAD_DIM
        assert qk_heads * HEAD_DIM == tn
        in_specs += [pl.BlockSpec((1, tn), lambda b, i, j: (0, 0))]
        in_specs += [pl.BlockSpec((tm, HEAD_DIM), lambda b, i, j: (i, 0))] * 3
        args += [gain, cos, s1, s2]
    return pl.pallas_call(
        functools.partial(_proj_kernel, qk_heads=qk_heads),
        grid=(G, T // tm, N // tn),
        in_specs=in_specs,
        out_specs=pl.BlockSpec((1, tm, tn), lambda b, i, j: (b, i, j)),
        out_shape=jax.ShapeDtypeStruct((G, T, N), out_dtype),
        scratch_shapes=[pltpu.VMEM((tm, D), BF16)],
        compiler_params=_cparams(("parallel", "parallel", "arbitrary")),
        name="norm_mod_proj",
    )(*args)


ATTN_KEY_CHUNK = 512


def _attn_kernel(q_ref, k_ref, v_ref, *rest, tq, has_prefix):
    if has_prefix:
        kp_ref, vp_ref, o_ref = rest
    else:
        (o_ref,) = rest
    nt = (((1,), (1,)), ((), ()))
    q = jnp.concatenate(
        [q_ref[0, :, g * HEAD_DIM:(g + 1) * HEAD_DIM] for g in range(ATTN_GROUP)], axis=0)
    n_keys = k_ref.shape[1]
    kc = min(ATTN_KEY_CHUNK, n_keys)
    chunks = [(kp_ref, vp_ref, 0, kp_ref.shape[1])] if has_prefix else []
    chunks += [(k_ref, v_ref, c * kc, kc) for c in range(n_keys // kc)]

    def scores(chunk):
        kr, _, start, size = chunk
        return lax.dot_general(q, kr[0, start:start + size, :], nt, preferred_element_type=F32)

    m = jnp.full((q.shape[0], 1), -1e30, F32)
    l = jnp.zeros((q.shape[0], 1), F32)
    o = jnp.zeros((q.shape[0], HEAD_DIM), F32)
    s_next = scores(chunks[0])
    for ci, (_, vr, start, size) in enumerate(chunks):
        s = s_next
        if ci + 1 < len(chunks):
            s_next = scores(chunks[ci + 1])
        m_new = jnp.maximum(m, jnp.max(s, axis=-1, keepdims=True))
        alpha = jnp.exp(m - m_new)
        p = jnp.exp(s - m_new)
        l = alpha * l + jnp.sum(p, axis=-1, keepdims=True)
        o = alpha * o + jnp.dot(p.astype(BF16), vr[0, start:start + size, :], preferred_element_type=F32)
        m = m_new
    o = o * (1.0 / l)
    for g in range(ATTN_GROUP):
        o_ref[0, :, g * HEAD_DIM:(g + 1) * HEAD_DIM] = o[g * tq:(g + 1) * tq].astype(o_ref.dtype)


def _attention(qkvu, prefix, *, tq):
    B, L, _ = qkvu.shape
    tq = min(tq, L)
    gw = ATTN_GROUP * HEAD_DIM
    k_blk = ATTN_HEADS
    v_blk = ATTN_HEADS + ATTN_KV_HEADS
    in_specs = [
        pl.BlockSpec((1, tq, gw), lambda b, h, i: (b, i, h)),
        pl.BlockSpec((1, L, HEAD_DIM), lambda b, h, i: (b, 0, k_blk + h)),
        pl.BlockSpec((1, L, HEAD_DIM), lambda b, h, i: (b, 0, v_blk + h)),
    ]
    args = [qkvu, qkvu, qkvu]
    if prefix is not None:
        Lp = prefix.shape[1]
        in_specs += [
            pl.BlockSpec((1, Lp, HEAD_DIM), lambda b, h, i: (b, 0, k_blk + h)),
            pl.BlockSpec((1, Lp, HEAD_DIM), lambda b, h, i: (b, 0, v_blk + h)),
        ]
        args += [prefix, prefix]
    return pl.pallas_call(
        functools.partial(_attn_kernel, tq=tq, has_prefix=prefix is not None),
        grid=(B, ATTN_KV_HEADS, L // tq),
        in_specs=in_specs,
        out_specs=pl.BlockSpec((1, tq, gw), lambda b, h, i: (b, i, h)),
        out_shape=jax.ShapeDtypeStruct((B, L, ATTN_HEADS * HEAD_DIM), BF16),
        compiler_params=_cparams(("parallel", "parallel", "arbitrary")),
        name="gqa_attention",
    )(*args)


S5_SLAB_GROUPS = LANES // S5_GROUP
S5_KD = S5_CHUNK * LANES
S5_PAIR_COLS = 4 * LANES


def _s5_end_kernel(u_ref, w_ref, e_ref):
    e_ref[...] = jnp.dot(u_ref[0], w_ref[0], preferred_element_type=F32)


def _s5_scan_kernel(e_ref, cf_ref, cb_ref, h_ref, *, batch, nc_ctx, nc_all):
    cps = SUBLANES // batch
    g_ctx, g_all = nc_ctx // cps, nc_all // cps

    def advance(hr, hi, er, ei, c_ref):
        ar, ai = c_ref[0:1, :], c_ref[1:2, :]
        return ar * hr - ai * hi + er, ar * hi + ai * hr + ei

    def sweep(r, state, col0, c_ref, order):
        e8 = e_ref[pl.ds(r, SUBLANES), col0:col0 + 2 * LANES]
        hr, hi = state
        out_r, out_i = [None] * cps, [None] * cps
        for c in order:
            out_r[c], out_i[c] = hr, hi
            rows = slice(c * batch, (c + 1) * batch)
            hr, hi = advance(hr, hi, e8[rows, :LANES], e8[rows, LANES:], c_ref)
        h_ref[pl.ds(r, SUBLANES), col0:col0 + LANES] = jnp.concatenate(out_r, axis=0).astype(h_ref.dtype)
        h_ref[pl.ds(r, SUBLANES), col0 + LANES:col0 + 2 * LANES] = (
            jnp.concatenate(out_i, axis=0).astype(h_ref.dtype))
        return hr, hi

    def body(i, carry):
        fwd, bwd = carry
        fwd = sweep(pl.multiple_of(i * SUBLANES, SUBLANES), fwd, 0, cf_ref, range(cps))
        gi = jnp.where(i < g_ctx, g_ctx - 1 - i, g_all - 1 - (i - g_ctx))
        bwd = sweep(pl.multiple_of(gi * SUBLANES, SUBLANES), bwd, 2 * LANES, cb_ref, reversed(range(cps)))
        return fwd, bwd

    zero = jnp.zeros((batch, LANES), F32)
    lax.fori_loop(0, g_all, body, ((zero, zero), (zero, zero)))


def _s5_out_kernel(u_ref, us_ref, lag_ref, h_ref, win_ref, d_ref, y_ref, tz_ref):
    T = S5_CHUNK
    t_blocks = tz_ref.shape[1] // LANES
    t0 = pl.program_id(1) * t_blocks
    for tt in range(t_blocks):
        for s in range(T):
            tz_ref[s * LANES:(s + 1) * LANES, tt * LANES:(tt + 1) * LANES] = lag_ref[0, t0 + tt - s + (T - 1)]
    y = jnp.dot(u_ref[0], tz_ref[...], preferred_element_type=F32)
    y += jnp.dot(h_ref[...].astype(BF16), win_ref[0], preferred_element_type=F32)
    y += us_ref[0].astype(F32) * d_ref[0]
    y_ref[0] = y.astype(y_ref.dtype)


def _s5_tables(lam_re, lam_im, log_dt, b_re, b_im, c_re, c_im, d_skip):
    T, C, P = S5_CHUNK, S5_GROUP, S5_STATE
    G = lam_re.shape[1]
    hi = lax.Precision.HIGHEST
    lam = lax.complex(jnp.minimum(lam_re.astype(F32), -1e-4), lam_im.astype(F32))
    dt = jnp.exp(log_dt.astype(F32))[..., None]
    lam_dt = lam * dt
    lam_bar = jnp.exp(lam_dt)
    bmat = lax.complex(b_re.astype(F32), b_im.astype(F32))
    b_bar = ((lam_bar - 1.0) / lam)[..., None] * bmat
    cmat = lax.complex(c_re.astype(F32), c_im.astype(F32))
    steps = jnp.arange(T + 1, dtype=F32)
    pw = jnp.exp(lam_dt[:, None] * steps[None, :, None, None])

    kern = jnp.real(jnp.einsum('zgcp,zjgp,zgpd->zjgcd', cmat, pw[:, :T], b_bar, precision=hi))

    def end_w(p_sel, bb):
        w = p_sel[:, :, None, :] * bb.transpose(0, 2, 1)[None]
        return [jnp.real(w), jnp.imag(w)]

    w_end = jnp.concatenate(end_w(pw[0, :T][::-1], b_bar[0]) + end_w(pw[1, :T], b_bar[1]), axis=-1)

    def in_w(p_sel, cm):
        z = cm.transpose(0, 2, 1)[:, :, None, :] * p_sel.transpose(1, 2, 0)[:, :, :, None]
        return [jnp.real(z), -jnp.imag(z)]

    w_in = jnp.stack(in_w(pw[0, 1:], cmat[0]) + in_w(pw[1, 1:][::-1], cmat[1]), axis=1)

    def coef(z):
        return jnp.stack([jnp.real(z).reshape(G * P), jnp.imag(z).reshape(G * P)])

    sg = S5_SLAB_GROUPS
    ngb, npair = G // sg, sg // 2
    grp = np.arange(sg)
    tok_cols_grp = np.tile(np.repeat(grp, C), T)
    state_cols_grp = (2 * np.arange(npair)[:, None, None, None] + np.arange(2)[None, None, :, None]
                      + np.zeros((1, 4, 1, P), np.int64)).reshape(-1)
    rows_gd = np.repeat(grp, C)

    lag_k = jnp.concatenate([kern[1][1:][::-1], (kern[0][0] + kern[1][0])[None], kern[0][1:]], axis=0)
    lag_src = lag_k.reshape(2 * T - 1, ngb, sg, C, C).transpose(1, 0, 2, 4, 3).reshape(ngb * (2 * T - 1), LANES, C)
    rep_c = np.tile(np.eye(C, dtype=np.float32), (1, sg))
    mask_c = (rows_gd[:, None] == np.repeat(grp, C)[None, :]).astype(np.float32)[None]
    lag_tiles = _spread(lag_src, rep_c, mask_c, lambda n: 0).reshape(ngb, 2 * T - 1, LANES, LANES)

    end_src = w_end.reshape(T, ngb, sg * C, 4 * P).transpose(1, 0, 2, 3).reshape(ngb * T, LANES, 4 * P)
    rep_state = np.tile(np.eye(4 * P, dtype=np.float32).reshape(4 * P, 1, 4, 1, P), (1, npair, 1, 2, 1))
    rep_state = rep_state.reshape(4 * P, npair * S5_PAIR_COLS)
    mask_state = (rows_gd[:, None] == state_cols_grp[None, :]).astype(np.float32)[None]
    end_slab = _spread(end_src, rep_state, mask_state, lambda n: 0).reshape(ngb, S5_KD, npair * S5_PAIR_COLS)

    in_src = w_in.reshape(ngb, npair, 2, 4, P, T * C).transpose(0, 1, 3, 2, 4, 5)
    in_src = in_src.reshape(ngb * npair * 4, 2 * P, T * C)
    rep_tok = np.tile(np.eye(T * C, dtype=np.float32).reshape(T * C, T, 1, C), (1, 1, sg, 1)).reshape(T * C, S5_KD)
    row_grp = 2 * np.arange(npair)[:, None, None] + np.repeat(np.arange(2), P)[None, :, None]
    mask_tok = (row_grp == tok_cols_grp[None, None, :]).astype(np.float32)
    in_slab = _spread(in_src, rep_tok, mask_tok, lambda n: (n // 4) % npair)
    in_slab = in_slab.reshape(ngb, npair * S5_PAIR_COLS, S5_KD)

    d_slab = jnp.tile(d_skip.astype(F32).reshape(ngb, 1, LANES), (1, 1, T))
    return lag_tiles, end_slab, in_slab, coef(pw[0, T]), coef(pw[1, T]), d_slab


SPREAD_TILES = 4


def _spread_kernel(src_ref, rep_ref, mask_ref, o_ref):
    for i in range(SPREAD_TILES):
        o_ref[i] = (jnp.dot(src_ref[i].astype(BF16), rep_ref[...], preferred_element_type=F32)
                    * mask_ref[0]).astype(o_ref.dtype)


def _spread(src, rep, mask, mask_index):
    n, rows, k = src.shape
    width = rep.shape[1]
    st = SPREAD_TILES
    assert n % st == 0
    return pl.pallas_call(
        _spread_kernel,
        grid=(n // st,),
        in_specs=[
            pl.BlockSpec((st, rows, k), lambda i: (i, 0, 0)),
            pl.BlockSpec((k, width), lambda i: (0, 0)),
            pl.BlockSpec((1, rows, width), lambda i: (mask_index(i * st), 0, 0)),
        ],
        out_specs=pl.BlockSpec((st, rows, width), lambda i: (i, 0, 0)),
        out_shape=jax.ShapeDtypeStruct((n, rows, width), BF16),
        compiler_params=_cparams(("parallel",)),
        name="s5_spread_operator",
    )(src, jnp.asarray(rep, BF16), jnp.asarray(mask, F32))


def _s5(u_ctx, u_lat, tables):
    tz, w_end, w_in, cf, cb, d_slab = tables
    T = S5_CHUNK
    B, Lc, W = u_ctx.shape
    L = u_lat.shape[1]
    ngb = W // LANES
    nc_ctx, nc_all = Lc // T, (Lc + L) // T
    cps = SUBLANES // B
    assert SUBLANES % B == 0 and nc_ctx % cps == 0 and (nc_all - nc_ctx) % cps == 0
    M = nc_all * B
    kd, sw = S5_KD, w_end.shape[-1]
    n_half = 4
    hw = kd // n_half
    u = jnp.concatenate([u_ctx, u_lat], axis=1)
    u = u.reshape(B, nc_all, T, ngb, LANES).transpose(3, 1, 0, 2, 4).reshape(ngb, M, kd)

    e = pl.pallas_call(
        _s5_end_kernel,
        grid=(ngb, n_half),
        in_specs=[
            pl.BlockSpec((1, M, kd), lambda g, j: (g, 0, 0)),
            pl.BlockSpec((1, kd, sw // n_half), lambda g, j: (g, 0, j)),
        ],
        out_specs=pl.BlockSpec((M, sw // n_half), lambda g, j: (0, g * n_half + j)),
        out_shape=jax.ShapeDtypeStruct((M, ngb * sw), F32),
        compiler_params=_cparams(("parallel", "parallel")),
        name="s5_chunk_end_states",
    )(u, w_end)

    h = pl.pallas_call(
        functools.partial(_s5_scan_kernel, batch=B, nc_ctx=nc_ctx, nc_all=nc_all),
        grid=(ngb * sw // S5_PAIR_COLS,),
        in_specs=[pl.BlockSpec((M, S5_PAIR_COLS), lambda j: (0, j))]
        + [pl.BlockSpec((2, LANES), lambda j: (0, j))] * 2,
        out_specs=pl.BlockSpec((M, S5_PAIR_COLS), lambda j: (0, j)),
        out_shape=jax.ShapeDtypeStruct((M, ngb * sw), F32),
        compiler_params=_cparams(("parallel",)),
        name="s5_chunk_scan",
    )(e, cf, cb)

    y = pl.pallas_call(
        _s5_out_kernel,
        grid=(ngb, n_half),
        in_specs=[
            pl.BlockSpec((1, M, kd), lambda g, j: (g, 0, 0)),
            pl.BlockSpec((1, M, hw), lambda g, j: (g, 0, j)),
            pl.BlockSpec((1,) + tz.shape[1:], lambda g, j: (g, 0, 0, 0)),
            pl.BlockSpec((M, sw), lambda g, j: (0, g)),
            pl.BlockSpec((1, sw, hw), lambda g, j: (g, 0, j)),
            pl.BlockSpec((1, 1, hw), lambda g, j: (g, 0, j)),
        ],
        out_specs=pl.BlockSpec((1, M, hw), lambda g, j: (g, 0, j)),
        out_shape=jax.ShapeDtypeStruct((ngb, M, kd), BF16),
        scratch_shapes=[pltpu.VMEM((kd, hw), BF16)],
        compiler_params=_cparams(("parallel", "parallel")),
        name="s5_chunk_outputs",
    )(u, u, tz, h, w_in, d_slab)

    y = y.reshape(ngb, nc_all, B, T, LANES).transpose(2, 1, 3, 0, 4).reshape(B, Lc + L, W)
    return y[:, :Lc], y[:, Lc:]


def _glu_kernel(y_ref, w_ref, b_ref, o_ref):
    y = y_ref[0].astype(F32)
    z = jnp.dot(jax.nn.gelu(y).astype(BF16), w_ref[...], preferred_element_type=F32) + b_ref[...]
    n = o_ref.shape[-1]
    o_ref[0] = (z[:, :n] * jax.nn.sigmoid(z[:, n:])).astype(o_ref.dtype)


def _glu(y, w, b, *, tm):
    G, T, W = y.shape
    tm = min(tm, T)
    return pl.pallas_call(
        _glu_kernel,
        grid=(G, T // tm),
        in_specs=[
            pl.BlockSpec((1, tm, W), lambda g, i: (g, i, 0)),
            pl.BlockSpec((W, 2 * W), lambda g, i: (0, 0)),
            pl.BlockSpec((1, 2 * W), lambda g, i: (0, 0)),
        ],
        out_specs=pl.BlockSpec((1, tm, W), lambda g, i: (g, i, 0)),
        out_shape=jax.ShapeDtypeStruct((G, T, W), BF16),
        compiler_params=_cparams(("parallel", "parallel")),
        name="s5_glu",
    )(y, w, b.reshape(1, 2 * W))


def _out_res_kernel(*refs, widths):
    n = len(widths)
    lhs = refs[:n]
    w_ref, x_ref, gate_ref, g_ref, o_ref = refs[n:]
    y = None
    off = 0
    for r, k in zip(lhs, widths):
        part = jnp.dot(r[0], w_ref[off:off + k, :], preferred_element_type=F32)
        y = part if y is None else y + part
        off += k
    o_ref[0] = x_ref[0] + gate_ref[0] * (_rms(y) * g_ref[...])


def _out_res(parts, w, x, gate, g, *, tm):
    G, T, D = x.shape
    tm = min(tm, T)
    widths = tuple(p.shape[-1] for p in parts)
    K = sum(widths)
    in_specs = [pl.BlockSpec((1, tm, k), lambda b, i: (b, i, 0)) for k in widths]
    in_specs += [
        pl.BlockSpec((K, D), lambda b, i: (0, 0)),
        pl.BlockSpec((1, tm, D), lambda b, i: (b, i, 0)),
        pl.BlockSpec((1, 1, D), _vec_map(gate, 2)),
        pl.BlockSpec((1, D), lambda b, i: (0, 0)),
    ]
    return pl.pallas_call(
        functools.partial(_out_res_kernel, widths=widths),
        grid=(G, T // tm),
        in_specs=in_specs,
        out_specs=pl.BlockSpec((1, tm, D), lambda b, i: (b, i, 0)),
        out_shape=jax.ShapeDtypeStruct((G, T, D), F32),
        compiler_params=_cparams(("parallel", "parallel")),
        name="out_proj_residual",
    )(*parts, w, x, gate, g.reshape(1, D))


HALO = BF16_ROWS


def _ffn_kernel(x_ref, xp_ref, xn_ref, g2_ref, sh_ref, sc_ref, wa_ref, wv_ref, cwa_ref, cwv_ref,
                cba_ref, cbv_ref, wd_ref, gate_ref, g3_ref, o_ref, h_ref, acc_ref, ua_ref, uv_ref, *, tm):
    i = pl.program_id(1)
    f = pl.program_id(2)

    def normmod(x):
        h = _rms(x) * g2_ref[...]
        return h * (1.0 + sc_ref[0]) + sh_ref[0]

    @pl.when(f == 0)
    def _():
        zeros = jnp.zeros((HALO - SUBLANES, x_ref.shape[-1]), F32)
        hp = jnp.where(i == 0, 0.0, normmod(xp_ref[0]))
        hn = jnp.where(i == pl.num_programs(1) - 1, 0.0, normmod(xn_ref[0]))
        h_ref[0:HALO] = jnp.concatenate([zeros, hp], axis=0).astype(BF16)
        h_ref[HALO:HALO + tm] = normmod(x_ref[0]).astype(BF16)
        h_ref[HALO + tm:] = jnp.concatenate([hn, zeros], axis=0).astype(BF16)
        acc_ref[...] = jnp.zeros_like(acc_ref)

    h = h_ref[...]
    ua_ref[...] = jnp.dot(h, wa_ref[...], preferred_element_type=F32)
    uv_ref[...] = jnp.dot(h, wv_ref[...], preferred_element_type=F32)

    def conv(u_ref, cw_ref, cb_ref):
        return (u_ref[HALO - 1:HALO - 1 + tm] * cw_ref[0:1, :] + u_ref[HALO:HALO + tm] * cw_ref[1:2, :]
                + u_ref[HALO + 1:HALO + 1 + tm] * cw_ref[2:3, :] + cb_ref[...])

    a = conv(ua_ref, cwa_ref, cba_ref)
    v = conv(uv_ref, cwv_ref, cbv_ref)
    gated = (a * jax.nn.sigmoid(a) * v).astype(BF16)
    acc_ref[...] += jnp.dot(gated, wd_ref[...], preferred_element_type=F32)

    @pl.when(f == pl.num_programs(2) - 1)
    def _():
        o_ref[0] = x_ref[0] + gate_ref[0] * (_rms(acc_ref[...]) * g3_ref[...])


def _conv_ffn(x, g2, shift, scale, w_up, conv_w, conv_b, w_down, gate, g3, *, tm, tf):
    G, T, D = x.shape
    F = w_down.shape[0]
    tm = min(tm, T)
    nf = F // tf
    n_i = T // tm
    rb = tm // SUBLANES
    last_rb = T // SUBLANES - 1
    conv_b = conv_b.reshape(1, 2 * F)
    return pl.pallas_call(
        functools.partial(_ffn_kernel, tm=tm),
        grid=(G, n_i, nf),
        in_specs=[
            pl.BlockSpec((1, tm, D), lambda b, i, f: (b, i, 0)),
            pl.BlockSpec((1, SUBLANES, D), lambda b, i, f: (b, jnp.maximum(i * rb - 1, 0), 0)),
            pl.BlockSpec((1, SUBLANES, D), lambda b, i, f: (b, jnp.minimum((i + 1) * rb, last_rb), 0)),
            pl.BlockSpec((1, D), lambda b, i, f: (0, 0)),
            pl.BlockSpec((1, 1, D), _vec_map(shift, 3)),
            pl.BlockSpec((1, 1, D), _vec_map(scale, 3)),
            pl.BlockSpec((D, tf), lambda b, i, f: (0, f)),
            pl.BlockSpec((D, tf), lambda b, i, f: (0, nf + f)),
            pl.BlockSpec((3, tf), lambda b, i, f: (0, f)),
            pl.BlockSpec((3, tf), lambda b, i, f: (0, nf + f)),
            pl.BlockSpec((1, tf), lambda b, i, f: (0, f)),
            pl.BlockSpec((1, tf), lambda b, i, f: (0, nf + f)),
            pl.BlockSpec((tf, D), lambda b, i, f: (f, 0)),
            pl.BlockSpec((1, 1, D), _vec_map(gate, 3)),
            pl.BlockSpec((1, D), lambda b, i, f: (0, 0)),
        ],
        out_specs=pl.BlockSpec((1, tm, D), lambda b, i, f: (b, i, 0)),
        out_shape=jax.ShapeDtypeStruct((G, T, D), F32),
        scratch_shapes=[
            pltpu.VMEM((tm + 2 * HALO, D), BF16),
            pltpu.VMEM((tm, D), F32),
            pltpu.VMEM((tm + 2 * HALO, tf), F32),
            pltpu.VMEM((tm + 2 * HALO, tf), F32),
        ],
        compiler_params=_cparams(("parallel", "parallel", "arbitrary")),
        name="conv_ffn",
    )(x, x, x, g2.reshape(1, D), shift, scale, w_up, w_up, conv_w, conv_w, conv_b, conv_b, w_down,
      gate, g3.reshape(1, D))


GLA_BLOCKS = HGRN_CHUNK // SUBLANES


def _gla_operands(z, v, q, lb, rev):
    nb, rb = GLA_BLOCKS, SUBLANES
    order = list(range(nb))[::-1] if rev else list(range(nb))
    scan_of = {b: j for j, b in enumerate(order)}
    pos = lax.broadcasted_iota(jnp.int32, (rb, HGRN_DK), 0)
    if rev:
        pos = rb - 1 - pos

    def prev(x, k):
        return pltpu.roll(x, (rb - k) if rev else k, 0)

    def nxt(x, k):
        return pltpu.roll(x, k if rev else (rb - k), 0)

    def blocks(x):
        return [x[rb * b:rb * (b + 1)] for b in range(nb)]

    def rows(bl):
        return jnp.concatenate(bl, axis=0)

    f = lb + (1.0 - lb) * jax.nn.sigmoid(z)
    fb = blocks(f)
    kb = blocks(1.0 - f)

    hb = rb // 2
    hpos = pos & (hb - 1)
    lo_last = hb if rev else hb - 1
    hi_first = hb - 1 if rev else hb
    p4, s4, p8, s8, bt = [], [], [], [], []
    for b in range(nb):
        x = y = fb[b]
        for k in (1, 2):
            x = x * jnp.where(hpos >= k, prev(x, k), 1.0)
            y = y * jnp.where(hpos <= hb - 1 - k, nxt(y, k), 1.0)
        lo_tot = x[lo_last:lo_last + 1, :]
        hi_tot = y[hi_first:hi_first + 1, :]
        sfx = jnp.where(hpos <= hb - 2, nxt(y, 1), 1.0)
        p4.append(x)
        s4.append(sfx)
        p8.append(x * jnp.where(pos >= hb, lo_tot, 1.0))
        s8.append(sfx * jnp.where(pos < hb, hi_tot, 1.0))
        bt.append(lo_tot * hi_tot)
    bts = [bt[order[j]] for j in range(nb)]
    ones = jnp.ones_like(bts[0])
    before = [ones]
    for j in range(1, nb):
        before.append(before[j - 1] * bts[j - 1])
    after = [ones] * nb
    for j in range(nb - 2, -1, -1):
        after[j] = after[j + 1] * bts[j + 1]
    ftot = before[nb - 1] * bts[nb - 1]

    kbase = [kb[b] * s8[b] for b in range(nb)]
    ops = {"v": v, "ftot": ftot,
           "kdec": rows([kbase[b] * after[scan_of[b]] for b in range(nb)]).astype(BF16)}
    if q is None:
        return ops

    qb = blocks(q)
    q8 = [qb[b] * p8[b] for b in range(nb)]
    ops["q_state"] = rows([q8[b] * before[scan_of[b]] for b in range(nb)]).astype(BF16)
    ops["q_far"] = rows(q8).astype(BF16)
    ops["q_near"] = q.astype(BF16)

    vb = blocks(v.astype(F32))
    kcols, vcols = [], []
    chain = {}
    for d in range(1, nb):
        for j in range(nb - d):
            chain[j] = kbase[order[j]] if d == 1 else chain[j] * bts[j + d - 1]
            kcols.append(chain[j])
            vcols.append(vb[order[j]])
    ops["k_far"] = rows(kcols).astype(BF16)
    ops["v_far"] = rows(vcols).astype(BF16)

    ops["q_cross"] = rows([qb[b] * p4[b] for b in range(nb)]).astype(BF16)
    ops["k_cross"] = rows([kb[b] * s4[b] for b in range(nb)]).astype(BF16)

    kvar = []
    cur = kb
    for d in range(hb):
        if d:
            cur = [cur[b] * nxt(fb[b], d) for b in range(nb)]
        kvar.append(rows(cur).astype(BF16))
    ops["k_near"] = jnp.concatenate(kvar, axis=0)
    return ops


def _gla_first_dots(ops, st):
    nt = (((1,), (1,)), ((), ()))
    st_new = st * ops["ftot"] + lax.dot_general(ops["v"], ops["kdec"], (((0,), (0,)), ((), ())),
                                                preferred_element_type=F32)
    if "q_near" not in ops:
        return st_new, None
    o = lax.dot_general(ops["q_state"], st.astype(BF16), nt, preferred_element_type=F32)
    far = lax.dot_general(ops["q_far"], ops["k_far"], nt, preferred_element_type=F32)
    cross = lax.dot_general(ops["q_cross"], ops["k_cross"], nt, preferred_element_type=F32)
    near = lax.dot_general(ops["q_near"], ops["k_near"], nt, preferred_element_type=F32)
    return st_new, (o, far, cross, near)


def _gla_second_dots(ops, scores, masks):
    o, far, cross, near = scores
    m_far, m_cross, m_near = masks
    o = o + jnp.dot((far * m_far).astype(BF16), ops["v_far"], preferred_element_type=F32)
    o = o + jnp.dot((cross * m_cross).astype(BF16), ops["v"], preferred_element_type=F32)
    v_near = jnp.concatenate([ops["v"]] * (near.shape[1] // HGRN_CHUNK), axis=0)
    return o + jnp.dot((near * m_near).astype(BF16), v_near, preferred_element_type=F32)


def _operand_rows():
    T, nb, rb = HGRN_CHUNK, GLA_BLOCKS, SUBLANES
    far = rb * nb * (nb - 1) // 2
    sizes = [("kdec", T), ("q_state", T), ("q_far", T), ("q_near", T), ("q_cross", T), ("k_cross", T),
             ("v", T), ("k_far", far), ("v_far", far), ("k_near", (rb // 2) * T)]
    rows, off = {}, 0
    for name, n in sizes:
        rows[name] = (off, n)
        off += n
    return rows, off


GLA_OPERAND_ROWS, GLA_OPERAND_TOTAL = _operand_rows()


def _gla_mask_tables():
    T, nb, rb = HGRN_CHUNK, GLA_BLOCKS, SUBLANES
    hb = rb // 2
    t = np.arange(T)
    far, cross, near = [], [], []
    for rev in (False, True):
        sblk = (nb - 1 - t // rb) if rev else t // rb
        spos = (rb - 1 - t % rb) if rev else t % rb
        same_block = t[:, None] // rb == t[None, :] // rb
        cols = [(d, j) for d in range(1, nb) for j in range(nb - d)]
        m_far = np.zeros((T, len(cols) * rb), np.float32)
        for c, (d, j) in enumerate(cols):
            m_far[sblk == j + d, c * rb:(c + 1) * rb] = 1.0
        m_cross = same_block & (spos[:, None] >= hb) & (spos[None, :] < hb)
        same_half = same_block & (spos[:, None] // hb == spos[None, :] // hb)
        m_near = np.zeros((T, hb * T), np.float32)
        for d in range(hb):
            m_near[:, d * T:(d + 1) * T] = same_half & (spos[:, None] - spos[None, :] == d)
        far.append(m_far)
        cross.append(m_cross.astype(np.float32))
        near.append(m_near)
    return tuple(jnp.asarray(np.stack(m)) for m in (far, cross, near))


def _gla_kernel(q_ref, zf_ref, zb_ref, v_ref, g_ref, czf_ref, czb_ref, cv_ref, lb_ref, ng_ref,
                mfar_ref, mcross_ref, mnear_ref, o_ref, oacc_ref, ops_a_ref, ft_a_ref, ops_b_ref, ft_b_ref,
                *, n_lat, n_ctx, hp):
    T, dk = HGRN_CHUNK, HGRN_DK
    jobs = [(h, rev) for h in range(hp) for rev in (False, True)]
    zero_state = jnp.zeros((dk, dk), F32)

    def cols(h):
        return slice(h * dk, (h + 1) * dk)

    def start(rev, i, n):
        r = ((n - 1 - i) if rev else i) * T
        return r if isinstance(r, int) else pl.multiple_of(r, T)

    def lb_row(h, rev):
        return lb_ref[int(rev):int(rev) + 1, cols(h)]

    def prep_ctx(i):
        ops = []
        for h, rev in jobs:
            r = start(rev, i, n_ctx)
            z = (czb_ref if rev else czf_ref)[0, pl.ds(r, T), cols(h)].astype(F32)
            ops.append(_gla_operands(z, cv_ref[0, pl.ds(r, T), cols(h)], None, lb_row(h, rev), rev))
        return tuple(ops)

    def prep_lat(i):
        ops = []
        for h, rev in jobs:
            r = start(rev, i, n_lat)
            qz = q_ref[0, pl.ds(r, T), cols(h)].astype(F32)
            z = (zb_ref if rev else zf_ref)[0, pl.ds(r, T), cols(h)].astype(F32)
            ops.append(_gla_operands(z, v_ref[0, pl.ds(r, T), cols(h)], qz * jax.nn.sigmoid(qz),
                                     lb_row(h, rev), rev))
        return tuple(ops)

    def stash(ops, buf):
        ops_ref, ft_ref = buf
        for j, o in enumerate(ops):
            for name, (off, n) in GLA_OPERAND_ROWS.items():
                ops_ref[j, off:off + n, :] = o[name]
            ft_ref[j] = jnp.broadcast_to(o["ftot"], (SUBLANES, dk))

    def fetch(buf):
        ops_ref, ft_ref = buf
        out = []
        for j in range(len(jobs)):
            o = {name: ops_ref[j, off:off + n, :] for name, (off, n) in GLA_OPERAND_ROWS.items()}
            o["ftot"] = ft_ref[j, 0:1, :]
            out.append(o)
        return out

    def ctx_step(i, sts):
        return tuple(_gla_first_dots(o, st)[0] for o, st in zip(prep_ctx(i), sts))

    def lat_step(i, sts, second_half, cur, nxt):
        ops = fetch(cur)
        firsts = [_gla_first_dots(o, st) for o, st in zip(ops, sts)]
        stash(prep_lat(jnp.minimum(i + 1, n_lat - 1)), nxt)
        outs = [_gla_second_dots(o, f[1], tuple(m[int(rev)] for m in (mfar_ref, mcross_ref, mnear_ref)))
                for o, f, (_, rev) in zip(ops, firsts, jobs)]
        for (h, rev), o in zip(jobs, outs):
            r = start(rev, i, n_lat)
            if second_half:
                o = o + oacc_ref[pl.ds(r, T), cols(h)]
                gz = g_ref[0, pl.ds(r, T), cols(h)].astype(F32)
                o = _rms(o) * ng_ref[...] * (gz * jax.nn.sigmoid(gz))
                o_ref[0, pl.ds(r, T), cols(h)] = o.astype(o_ref.dtype)
            else:
                oacc_ref[pl.ds(r, T), cols(h)] = o
        return tuple(f[0] for f in firsts)

    buf_a, buf_b = (ops_a_ref, ft_a_ref), (ops_b_ref, ft_b_ref)

    def lat_pair(k, sts, second_half):
        sts = lat_step(2 * k, sts, second_half, buf_a, buf_b)
        return lat_step(2 * k + 1, sts, second_half, buf_b, buf_a)

    sts = lax.fori_loop(0, n_ctx, ctx_step, (zero_state,) * len(jobs))
    stash(prep_lat(0), buf_a)
    quarter = n_lat // 4
    sts = lax.fori_loop(0, quarter, functools.partial(lat_pair, second_half=False), sts)
    lax.fori_loop(quarter, 2 * quarter, functools.partial(lat_pair, second_half=True), sts)


GLA_HEADS_PER_STEP = 2


def _gla(p_lat, p_ctx, lb, norm_g, n_heads):
    B, L, _ = p_lat.shape
    Lc = p_ctx.shape[1]
    hp = GLA_HEADS_PER_STEP
    steps = n_heads // hp
    dv = HGRN_DK
    w = hp * dv

    def col(group):
        return lambda b, h: (b, 0, group * steps + h)

    lat_specs = [pl.BlockSpec((1, L, w), col(gidx)) for gidx in range(5)]
    ctx_specs = [pl.BlockSpec((1, Lc, w), col(gidx)) for gidx in (1, 2, 3)]
    masks = _gla_mask_tables()
    n_lat = L // HGRN_CHUNK
    assert n_lat % 4 == 0 and n_heads % hp == 0
    operand_bufs = [pltpu.VMEM((2 * hp, GLA_OPERAND_TOTAL, dv), BF16), pltpu.VMEM((2 * hp, SUBLANES, dv), F32)]
    return pl.pallas_call(
        functools.partial(_gla_kernel, n_lat=n_lat, n_ctx=Lc // HGRN_CHUNK, hp=hp),
        grid=(B, steps),
        in_specs=lat_specs + ctx_specs + [
            pl.BlockSpec((2, w), lambda b, h: (0, h)),
            pl.BlockSpec((1, dv), lambda b, h: (0, 0)),
        ] + [pl.BlockSpec(m.shape, lambda b, h: (0, 0, 0)) for m in masks],
        out_specs=pl.BlockSpec((1, L, w), lambda b, h: (b, 0, h)),
        out_shape=jax.ShapeDtypeStruct((B, L, n_heads * dv), BF16),
        scratch_shapes=[pltpu.VMEM((L, w), F32)] + operand_bufs * 2,
        compiler_params=_cparams(("parallel", "parallel")),
        name="hgrn2_bidirectional",
    )(*([p_lat] * 5), *([p_ctx] * 3), lb, norm_g.reshape(1, dv), *masks)


def _rope_tables(n_tokens):
    rows = n_tokens // GRID_W
    row = jnp.repeat(jnp.arange(rows, dtype=F32), GRID_W)
    colp = jnp.tile(jnp.arange(GRID_W, dtype=F32), rows)
    inv = ROPE_THETA ** (-jnp.arange(AXIS_FREQS, dtype=F32) / AXIS_FREQS)
    ang = jnp.stack([row[:, None] * inv, colp[:, None] * inv], axis=1)
    cos, sin = jnp.cos(ang), jnp.sin(ang)
    zero = jnp.zeros_like(sin)
    full = lambda a, b: jnp.stack([a, b], axis=2).reshape(n_tokens, HEAD_DIM)
    return full(cos, cos), full(-sin, zero), full(zero, sin)


def _identity_rope(n_tokens):
    return (jnp.ones((n_tokens, HEAD_DIM), F32), jnp.zeros((n_tokens, HEAD_DIM), F32),
            jnp.zeros((n_tokens, HEAD_DIM), F32))


def kernel(x, c, ctx, c_ctx, mod_w, mod_b, norm_g, ab_w_in, ab_w_out, attn_q_norm, attn_k_norm,
           s5_lam_re, s5_lam_im, s5_log_dt, s5_b_re, s5_b_im, s5_c_re, s5_c_im, s5_d,
           s5_glu_w, s5_glu_b, c_w_in, c_w_out, hgrn_lb_logits, hgrn_norm,
           ffn_w_up, ffn_conv_w, ffn_conv_b, ffn_w_down):
    B, L, D = x.shape
    Lc = ctx.shape[1]
    depth = mod_w.shape[0]
    attn_w = ATTN_HEADS * HEAD_DIM
    kv_w = ATTN_KV_HEADS * HEAD_DIM
    qk_w = attn_w + kv_w

    lb_all = jnp.cumsum(jax.nn.softmax(hgrn_lb_logits.astype(F32), axis=0), axis=0)
    lb_all = lb_all - lb_all[:1]

    cvec = jnp.concatenate([c, c_ctx[None], jnp.zeros((SUBLANES - (B + 1) % SUBLANES, D), F32)], axis=0)
    mods = _modulation(cvec, mod_w, mod_b)

    rope_lat = _rope_tables(L)
    rope_ctx = _identity_rope(Lc)

    for l in range(depth):
        last = l == depth - 1
        m_lat = [mods[l, :B, k * D:(k + 1) * D][:, None, :] for k in range(6)]
        m_ctx = [mods[l, B:B + 1, k * D:(k + 1) * D][:, None, :] for k in range(6)]
        g = norm_g[l]
        w_up = ffn_w_up[l].astype(BF16)
        w_down = ffn_w_down[l].astype(BF16)
        if l % 2 == 0:
            e = l // 2
            w_in = ab_w_in[e].astype(BF16)
            gain = jnp.concatenate([jnp.tile(attn_q_norm[e] * (HEAD_DIM ** -0.5), ATTN_HEADS),
                                    jnp.tile(attn_k_norm[e], ATTN_KV_HEADS)]).reshape(1, qk_w)
            p_lat = _proj(x, g[0], m_lat[0], m_lat[1], w_in, tm=1024, tn=qk_w, qk=(gain,) + rope_lat)
            p_ctx = _proj(ctx, g[0], m_ctx[0], m_ctx[1], w_in, tm=1024, tn=qk_w, qk=(gain,) + rope_ctx)
            a_lat = _attention(p_lat, p_ctx, tq=256)
            tables = _s5_tables(s5_lam_re[e], s5_lam_im[e], s5_log_dt[e], s5_b_re[e], s5_b_im[e],
                                s5_c_re[e], s5_c_im[e], s5_d[e])
            u0 = qk_w + kv_w
            y_ctx, y_lat = _s5(p_ctx[..., u0:], p_lat[..., u0:], tables)
            glu_w = s5_glu_w[e].astype(BF16)
            w_out = ab_w_out[e].astype(BF16)
            s_lat = _glu(y_lat, glu_w, s5_glu_b[e], tm=512)
            x = _out_res([a_lat, s_lat], w_out, x, m_lat[2], g[1], tm=512)
            if not last:
                a_ctx = _attention(p_ctx, None, tq=128)
                s_ctx = _glu(y_ctx, glu_w, s5_glu_b[e], tm=512)
                ctx = _out_res([a_ctx, s_ctx], w_out, ctx, m_ctx[2], g[1], tm=512)
        else:
            o_idx = l // 2
            w_in = c_w_in[o_idx].astype(BF16)
            n_heads = c_w_out.shape[1] // HGRN_DK
            p_lat = _proj(x, g[0], m_lat[0], m_lat[1], w_in, tm=1024, tn=2048)
            p_ctx = _proj(ctx, g[0], m_ctx[0], m_ctx[1], w_in, tm=1024, tn=2048)
            o_lat = _gla(p_lat, p_ctx, lb_all[l], hgrn_norm[o_idx], n_heads)
            x = _out_res([o_lat], c_w_out[o_idx].astype(BF16), x, m_lat[2], g[1], tm=512)
            assert last, "context outputs of the HGRN2 mixer are only needed by a following layer"
        x = _conv_ffn(x, g[2], m_lat[3], m_lat[4], w_up, ffn_conv_w[l], ffn_conv_b[l], w_down,
                      m_lat[5], g[3], tm=512, tf=512)
        if not last:
            ctx = _conv_ffn(ctx, g[2], m_ctx[3], m_ctx[4], w_up, ffn_conv_w[l], ffn_conv_b[l], w_down,
                            m_ctx[5], g[3], tm=512, tf=512)
    return x
```

```python
import functools
import math

import jax
import jax.numpy as jnp
import numpy as np
from jax import lax
from jax.experimental import pallas as pl
from jax.experimental.pallas import tpu as pltpu

F32 = jnp.float32
BF16 = jnp.bfloat16
EPS = 1e-6

LANES = 128
SUBLANES = 8
BF16_ROWS = 16

HEAD_DIM = 128
ATTN_HEADS = 8
ATTN_KV_HEADS = 2
ATTN_GROUP = ATTN_HEADS // ATTN_KV_HEADS
GRID_W = 64
ROPE_THETA = 10000.0
AXIS_FREQS = HEAD_DIM // 4

S5_GROUP = 16
S5_STATE = 64
S5_CHUNK = 16

HGRN_DK = 128
HGRN_CHUNK = 64

VMEM_LIMIT = 56 * 1024 * 1024


VMEM_LIMIT_LARGE = 60 * 1024 * 1024


def _cparams(sem, vmem_limit=VMEM_LIMIT):
    return pltpu.CompilerParams(dimension_semantics=sem, vmem_limit_bytes=vmem_limit)


def _vec_map(vec, nd_grid):
    shared = vec.shape[0] == 1
    if nd_grid == 2:
        return lambda g, i: (0 if shared else g, 0, 0)
    return lambda g, i, j: (0 if shared else g, 0, 0)


def _rms(x):
    return x * lax.rsqrt(jnp.mean(x * x, axis=-1, keepdims=True) + EPS)


def _mod_kernel(s_ref, w_ref, b_ref, o_ref):
    s = s_ref[...]
    a = s * jax.nn.sigmoid(s)
    o_ref[0] = jnp.dot(a.astype(BF16), w_ref[0].astype(BF16), preferred_element_type=F32) + b_ref[0]


def _modulation(cvec, mod_w, mod_b):
    depth, d, n = mod_w.shape
    rows = cvec.shape[0]
    tn = 1024
    return pl.pallas_call(
        _mod_kernel,
        grid=(depth, n // tn),
        in_specs=[
            pl.BlockSpec((rows, d), lambda l, j: (0, 0)),
            pl.BlockSpec((1, d, tn), lambda l, j: (l, 0, j)),
            pl.BlockSpec((1, 1, tn), lambda l, j: (l, 0, j)),
        ],
        out_specs=pl.BlockSpec((1, rows, tn), lambda l, j: (l, 0, j)),
        out_shape=jax.ShapeDtypeStruct((depth, rows, n), F32),
        compiler_params=_cparams(("parallel", "parallel")),
        name="modulation",
    )(cvec, mod_w, mod_b.reshape(depth, 1, n))


def _norm_mod_to(h_ref, x_ref, g_ref, sh_ref, sc_ref):
    h = _rms(x_ref[0]) * g_ref[...]
    h = h * (1.0 + sc_ref[0]) + sh_ref[0]
    h_ref[...] = h.astype(BF16)


def _proj_kernel(x_ref, g_ref, sh_ref, sc_ref, w_ref, o_ref, h_ref):
    @pl.when(pl.program_id(2) == 0)
    def _():
        _norm_mod_to(h_ref, x_ref, g_ref, sh_ref, sc_ref)

    o_ref[0] = jnp.dot(h_ref[...], w_ref[...], preferred_element_type=F32).astype(o_ref.dtype)


def _proj_specs(x, g, shift, scale, w, tm, tn):
    G, T, D = x.shape
    in_specs = [
        pl.BlockSpec((1, tm, D), lambda b, i, j: (b, i, 0)),
        pl.BlockSpec((1, D), lambda b, i, j: (0, 0)),
        pl.BlockSpec((1, 1, D), _vec_map(shift, 3)),
        pl.BlockSpec((1, 1, D), _vec_map(scale, 3)),
        pl.BlockSpec((D, tn), lambda b, i, j: (0, j)),
    ]
    return in_specs, [x, g.reshape(1, D), shift, scale, w]


def _proj(x, g, shift, scale, w, *, tm, tn, out_dtype=BF16):
    G, T, D = x.shape
    N = w.shape[1]
    tm = min(tm, T)
    in_specs, args = _proj_specs(x, g, shift, scale, w, tm, tn)
    return pl.pallas_call(
        _proj_kernel,
        grid=(G, T // tm, N // tn),
        in_specs=in_specs,
        out_specs=pl.BlockSpec((1, tm, tn), lambda b, i, j: (b, i, j)),
        out_shape=jax.ShapeDtypeStruct((G, T, N), out_dtype),
        scratch_shapes=[pltpu.VMEM((tm, D), BF16)],
        compiler_params=_cparams(("parallel", "parallel", "arbitrary")),
        name="norm_mod_proj",
    )(*args)


AB_TILE_HEADS = 4
AB_QKV_TILES = (ATTN_HEADS + 2 * ATTN_KV_HEADS) // AB_TILE_HEADS


def _proj_ab_kernel(x_ref, g_ref, sh_ref, sc_ref, w_ref, gain_ref, cos_ref, s1_ref, s2_ref,
                    qkv_ref, u_ref, h_ref):
    j = pl.program_id(2)

    @pl.when(j == 0)
    def _():
        _norm_mod_to(h_ref, x_ref, g_ref, sh_ref, sc_ref)

    acc = jnp.dot(h_ref[...], w_ref[...], preferred_element_type=F32)
    k_tile = ATTN_HEADS // AB_TILE_HEADS

    def head(hh):
        return slice(hh * HEAD_DIM, (hh + 1) * HEAD_DIM)

    def normed_rotated(hh):
        y = _rms(acc[:, head(hh)]) * gain_ref[:, head(hh)]
        return (y * cos_ref[...] + pltpu.roll(y, HEAD_DIM - AXIS_FREQS, 1) * s1_ref[...]
                + pltpu.roll(y, AXIS_FREQS, 1) * s2_ref[...])

    @pl.when(j < k_tile)
    def _():
        for hh in range(AB_TILE_HEADS):
            qkv_ref[0, :, head(hh)] = normed_rotated(hh).astype(qkv_ref.dtype)

    @pl.when(j == k_tile)
    def _():
        for hh in range(AB_TILE_HEADS):
            y = normed_rotated(hh) if hh < ATTN_KV_HEADS else acc[:, head(hh)]
            qkv_ref[0, :, head(hh)] = y.astype(qkv_ref.dtype)

    @pl.when(j > k_tile)
    def _():
        chunks = u_ref.shape[1]
        for s in range(AB_TILE_HEADS):
            u_ref[s, :, 0] = acc[:, head(s)].astype(u_ref.dtype).reshape(chunks, S5_CHUNK, LANES)


def _proj_ab(x, g, shift, scale, w, gain, rope, *, tm):
    G, T, D = x.shape
    tm = min(tm, T)
    tn = AB_TILE_HEADS * HEAD_DIM
    n_tiles = w.shape[1] // tn
    slabs = (n_tiles - AB_QKV_TILES) * AB_TILE_HEADS
    in_specs, args = _proj_specs(x, g, shift, scale, w, tm, tn)
    last_qkv = AB_QKV_TILES - 1
    in_specs += [pl.BlockSpec((1, tn), lambda b, i, j: (0, jnp.minimum(j, last_qkv)))]
    in_specs += [pl.BlockSpec((tm, HEAD_DIM), lambda b, i, j: (i, 0))] * 3
    return pl.pallas_call(
        _proj_ab_kernel,
        grid=(G, T // tm, n_tiles),
        in_specs=in_specs,
        out_specs=[
            pl.BlockSpec((1, tm, tn), lambda b, i, j: (b, i, jnp.minimum(j, last_qkv))),
            pl.BlockSpec((AB_TILE_HEADS, tm // S5_CHUNK, 1, S5_CHUNK, LANES),
                         lambda b, i, j: (jnp.maximum(j - AB_QKV_TILES, 0), i, b, 0, 0)),
        ],
        out_shape=[
            jax.ShapeDtypeStruct((G, T, AB_QKV_TILES * tn), BF16),
            jax.ShapeDtypeStruct((slabs, T // S5_CHUNK, G, S5_CHUNK, LANES), BF16),
        ],
        scratch_shapes=[pltpu.VMEM((tm, D), BF16)],
        compiler_params=_cparams(("parallel", "parallel", "arbitrary")),
        name="norm_mod_proj_ab",
    )(*args, gain, *rope)


ATTN_KEY_CHUNK = 512


def _attn_kernel(q_ref, k_ref, v_ref, *rest, tq, has_prefix):
    if has_prefix:
        kp_ref, vp_ref, o_ref = rest
    else:
        (o_ref,) = rest
    nt = (((1,), (1,)), ((), ()))
    q = jnp.concatenate(
        [q_ref[0, :, g * HEAD_DIM:(g + 1) * HEAD_DIM] for g in range(ATTN_GROUP)], axis=0)
    n_keys = k_ref.shape[1]
    kc = min(ATTN_KEY_CHUNK, n_keys)
    chunks = [(kp_ref, vp_ref, 0, kp_ref.shape[1])] if has_prefix else []
    chunks += [(k_ref, v_ref, c * kc, kc) for c in range(n_keys // kc)]

    def scores(chunk):
        kr, _, start, size = chunk
        return lax.dot_general(q, kr[0, start:start + size, :], nt, preferred_element_type=F32)

    m = jnp.full((q.shape[0], 1), -1e30, F32)
    l = jnp.zeros((q.shape[0], 1), F32)
    o = jnp.zeros((q.shape[0], HEAD_DIM), F32)
    s_next = scores(chunks[0])
    for ci, (_, vr, start, size) in enumerate(chunks):
        s = s_next
        if ci + 1 < len(chunks):
            s_next = scores(chunks[ci + 1])
        m_new = jnp.maximum(m, jnp.max(s, axis=-1, keepdims=True))
        alpha = jnp.exp(m - m_new)
        p = jnp.exp(s - m_new)
        l = alpha * l + jnp.sum(p, axis=-1, keepdims=True)
        o = alpha * o + jnp.dot(p.astype(BF16), vr[0, start:start + size, :], preferred_element_type=F32)
        m = m_new
    o = o * (1.0 / l)
    for g in range(ATTN_GROUP):
        o_ref[0, :, g * HEAD_DIM:(g + 1) * HEAD_DIM] = o[g * tq:(g + 1) * tq].astype(o_ref.dtype)


def _attention(qkvu, prefix, *, tq):
    B, L, _ = qkvu.shape
    tq = min(tq, L)
    gw = ATTN_GROUP * HEAD_DIM
    k_blk = ATTN_HEADS
    v_blk = ATTN_HEADS + ATTN_KV_HEADS
    in_specs = [
        pl.BlockSpec((1, tq, gw), lambda b, h, i: (b, i, h)),
        pl.BlockSpec((1, L, HEAD_DIM), lambda b, h, i: (b, 0, k_blk + h)),
        pl.BlockSpec((1, L, HEAD_DIM), lambda b, h, i: (b, 0, v_blk + h)),
    ]
    args = [qkvu, qkvu, qkvu]
    if prefix is not None:
        Lp = prefix.shape[1]
        in_specs += [
            pl.BlockSpec((1, Lp, HEAD_DIM), lambda b, h, i: (b, 0, k_blk + h)),
            pl.BlockSpec((1, Lp, HEAD_DIM), lambda b, h, i: (b, 0, v_blk + h)),
        ]
        args += [prefix, prefix]
    return pl.pallas_call(
        functools.partial(_attn_kernel, tq=tq, has_prefix=prefix is not None),
        grid=(B, ATTN_KV_HEADS, L // tq),
        in_specs=in_specs,
        out_specs=pl.BlockSpec((1, tq, gw), lambda b, h, i: (b, i, h)),
        out_shape=jax.ShapeDtypeStruct((B, L, ATTN_HEADS * HEAD_DIM), BF16),
        compiler_params=_cparams(("parallel", "parallel", "arbitrary")),
        name="gqa_attention",
    )(*args)


S5_SLAB_GROUPS = LANES // S5_GROUP
S5_KD = S5_CHUNK * LANES
S5_PAIR_COLS = 4 * LANES


def _s5_end_kernel(u_ref, w_ref, e_ref):
    e_ref[...] = jnp.dot(u_ref[0], w_ref[0], preferred_element_type=F32)


def _s5_scan_kernel(e_ref, cf_ref, cb_ref, h_ref, *, batch, nc_ctx, nc_all):
    cps = SUBLANES // batch
    g_ctx, g_all = nc_ctx // cps, nc_all // cps

    def advance(hr, hi, er, ei, c_ref):
        ar, ai = c_ref[0:1, :], c_ref[1:2, :]
        return ar * hr - ai * hi + er, ar * hi + ai * hr + ei

    def sweep(r, state, col0, c_ref, order):
        e8 = e_ref[pl.ds(r, SUBLANES), col0:col0 + 2 * LANES]
        hr, hi = state
        out_r, out_i = [None] * cps, [None] * cps
        for c in order:
            out_r[c], out_i[c] = hr, hi
            rows = slice(c * batch, (c + 1) * batch)
            hr, hi = advance(hr, hi, e8[rows, :LANES], e8[rows, LANES:], c_ref)
        h_ref[pl.ds(r, SUBLANES), col0:col0 + LANES] = jnp.concatenate(out_r, axis=0).astype(h_ref.dtype)
        h_ref[pl.ds(r, SUBLANES), col0 + LANES:col0 + 2 * LANES] = (
            jnp.concatenate(out_i, axis=0).astype(h_ref.dtype))
        return hr, hi

    def body(i, carry):
        fwd, bwd = carry
        fwd = sweep(pl.multiple_of(i * SUBLANES, SUBLANES), fwd, 0, cf_ref, range(cps))
        gi = jnp.where(i < g_ctx, g_ctx - 1 - i, g_all - 1 - (i - g_ctx))
        bwd = sweep(pl.multiple_of(gi * SUBLANES, SUBLANES), bwd, 2 * LANES, cb_ref, reversed(range(cps)))
        return fwd, bwd

    zero = jnp.zeros((batch, LANES), F32)
    lax.fori_loop(0, g_all, body, ((zero, zero), (zero, zero)))


def _s5_out_kernel(u_ref, us_ref, lag_ref, h_ref, win_ref, d_ref, y_ref, tz_ref):
    T = S5_CHUNK
    t_blocks = tz_ref.shape[1] // LANES
    t0 = pl.program_id(1) * t_blocks
    for tt in range(t_blocks):
        for s in range(T):
            tz_ref[s * LANES:(s + 1) * LANES, tt * LANES:(tt + 1) * LANES] = lag_ref[0, t0 + tt - s + (T - 1)]
    y = jnp.dot(u_ref[0], tz_ref[...], preferred_element_type=F32)
    y += jnp.dot(h_ref[...].astype(BF16), win_ref[0], preferred_element_type=F32)
    y += us_ref[0].astype(F32) * d_ref[0]
    y_ref[0] = y.astype(y_ref.dtype)


def _s5_tables(lam_re, lam_im, log_dt, b_re, b_im, c_re, c_im, d_skip):
    T, C, P = S5_CHUNK, S5_GROUP, S5_STATE
    G = lam_re.shape[1]
    hi = lax.Precision.HIGHEST
    lam = lax.complex(jnp.minimum(lam_re.astype(F32), -1e-4), lam_im.astype(F32))
    dt = jnp.exp(log_dt.astype(F32))[..., None]
    lam_dt = lam * dt
    lam_bar = jnp.exp(lam_dt)
    bmat = lax.complex(b_re.astype(F32), b_im.astype(F32))
    b_bar = ((lam_bar - 1.0) / lam)[..., None] * bmat
    cmat = lax.complex(c_re.astype(F32), c_im.astype(F32))
    steps = jnp.arange(T + 1, dtype=F32)
    pw = jnp.exp(lam_dt[:, None] * steps[None, :, None, None])

    kern = jnp.real(jnp.einsum('zgcp,zjgp,zgpd->zjgcd', cmat, pw[:, :T], b_bar, precision=hi))

    def end_w(p_sel, bb):
        w = p_sel[:, :, None, :] * bb.transpose(0, 2, 1)[None]
        return [jnp.real(w), jnp.imag(w)]

    w_end = jnp.concatenate(end_w(pw[0, :T][::-1], b_bar[0]) + end_w(pw[1, :T], b_bar[1]), axis=-1)

    def in_w(p_sel, cm):
        z = cm.transpose(0, 2, 1)[:, :, None, :] * p_sel.transpose(1, 2, 0)[:, :, :, None]
        return [jnp.real(z), -jnp.imag(z)]

    w_in = jnp.stack(in_w(pw[0, 1:], cmat[0]) + in_w(pw[1, 1:][::-1], cmat[1]), axis=1)

    def coef(z):
        return jnp.stack([jnp.real(z).reshape(G * P), jnp.imag(z).reshape(G * P)])

    sg = S5_SLAB_GROUPS
    ngb, npair = G // sg, sg // 2
    grp = np.arange(sg)
    tok_cols_grp = np.tile(np.repeat(grp, C), T)
    state_cols_grp = (2 * np.arange(npair)[:, None, None, None] + np.arange(2)[None, None, :, None]
                      + np.zeros((1, 4, 1, P), np.int64)).reshape(-1)
    rows_gd = np.repeat(grp, C)

    lag_k = jnp.concatenate([kern[1][1:][::-1], (kern[0][0] + kern[1][0])[None], kern[0][1:]], axis=0)
    lag_src = lag_k.reshape(2 * T - 1, ngb, sg, C, C).transpose(1, 0, 2, 4, 3).reshape(ngb * (2 * T - 1), LANES, C)
    rep_c = np.tile(np.eye(C, dtype=np.float32), (1, sg))
    mask_c = (rows_gd[:, None] == np.repeat(grp, C)[None, :]).astype(np.float32)[None]
    lag_tiles = _spread(lag_src, rep_c, mask_c, lambda n: 0).reshape(ngb, 2 * T - 1, LANES, LANES)

    end_src = w_end.reshape(T, ngb, sg * C, 4 * P).transpose(1, 0, 2, 3).reshape(ngb * T, LANES, 4 * P)
    rep_state = np.tile(np.eye(4 * P, dtype=np.float32).reshape(4 * P, 1, 4, 1, P), (1, npair, 1, 2, 1))
    rep_state = rep_state.reshape(4 * P, npair * S5_PAIR_COLS)
    mask_state = (rows_gd[:, None] == state_cols_grp[None, :]).astype(np.float32)[None]
    end_slab = _spread(end_src, rep_state, mask_state, lambda n: 0).reshape(ngb, S5_KD, npair * S5_PAIR_COLS)

    in_src = w_in.reshape(ngb, npair, 2, 4, P, T * C).transpose(0, 1, 3, 2, 4, 5)
    in_src = in_src.reshape(ngb * npair * 4, 2 * P, T * C)
    rep_tok = np.tile(np.eye(T * C, dtype=np.float32).reshape(T * C, T, 1, C), (1, 1, sg, 1)).reshape(T * C, S5_KD)
    row_grp = 2 * np.arange(npair)[:, None, None] + np.repeat(np.arange(2), P)[None, :, None]
    mask_tok = (row_grp == tok_cols_grp[None, None, :]).astype(np.float32)
    in_slab = _spread(in_src, rep_tok, mask_tok, lambda n: (n // 4) % npair)
    in_slab = in_slab.reshape(ngb, npair * S5_PAIR_COLS, S5_KD)

    d_slab = jnp.tile(d_skip.astype(F32).reshape(ngb, 1, LANES), (1, 1, T))
    return lag_tiles, end_slab, in_slab, coef(pw[0, T]), coef(pw[1, T]), d_slab


SPREAD_TILES = 4


def _spread_kernel(src_ref, rep_ref, mask_ref, o_ref):
    for i in range(SPREAD_TILES):
        o_ref[i] = (jnp.dot(src_ref[i].astype(BF16), rep_ref[...], preferred_element_type=F32)
                    * mask_ref[0]).astype(o_ref.dtype)


def _spread(src, rep, mask, mask_index):
    n, rows, k = src.shape
    width = rep.shape[1]
    st = SPREAD_TILES
    assert n % st == 0
    return pl.pallas_call(
        _spread_kernel,
        grid=(n // st,),
        in_specs=[
            pl.BlockSpec((st, rows, k), lambda i: (i, 0, 0)),
            pl.BlockSpec((k, width), lambda i: (0, 0)),
            pl.BlockSpec((1, rows, width), lambda i: (mask_index(i * st), 0, 0)),
        ],
        out_specs=pl.BlockSpec((st, rows, width), lambda i: (i, 0, 0)),
        out_shape=jax.ShapeDtypeStruct((n, rows, width), BF16),
        compiler_params=_cparams(("parallel",)),
        name="s5_spread_operator",
    )(src, jnp.asarray(rep, BF16), jnp.asarray(mask, F32))


def _s5(u_ctx, u_lat, tables):
    tz, w_end, w_in, cf, cb, d_slab = tables
    ngb, nc_ctx, B = u_ctx.shape[:3]
    nc_all = nc_ctx + u_lat.shape[1]
    cps = SUBLANES // B
    assert SUBLANES % B == 0 and nc_ctx % cps == 0 and (nc_all - nc_ctx) % cps == 0
    M = nc_all * B
    kd, sw = S5_KD, w_end.shape[-1]
    n_half = 4
    hw = kd // n_half
    u = jnp.concatenate([u_ctx, u_lat], axis=1).reshape(ngb, M, kd)

    e = pl.pallas_call(
        _s5_end_kernel,
        grid=(ngb, n_half),
        in_specs=[
            pl.BlockSpec((1, M, kd), lambda g, j: (g, 0, 0)),
            pl.BlockSpec((1, kd, sw // n_half), lambda g, j: (g, 0, j)),
        ],
        out_specs=pl.BlockSpec((M, sw // n_half), lambda g, j: (0, g * n_half + j)),
        out_shape=jax.ShapeDtypeStruct((M, ngb * sw), F32),
        compiler_params=_cparams(("parallel", "parallel")),
        name="s5_chunk_end_states",
    )(u, w_end)

    h = pl.pallas_call(
        functools.partial(_s5_scan_kernel, batch=B, nc_ctx=nc_ctx, nc_all=nc_all),
        grid=(ngb * sw // S5_PAIR_COLS,),
        in_specs=[pl.BlockSpec((M, S5_PAIR_COLS), lambda j: (0, j))]
        + [pl.BlockSpec((2, LANES), lambda j: (0, j))] * 2,
        out_specs=pl.BlockSpec((M, S5_PAIR_COLS), lambda j: (0, j)),
        out_shape=jax.ShapeDtypeStruct((M, ngb * sw), F32),
        compiler_params=_cparams(("parallel",)),
        name="s5_chunk_scan",
    )(e, cf, cb)

    y = pl.pallas_call(
        _s5_out_kernel,
        grid=(ngb, n_half),
        in_specs=[
            pl.BlockSpec((1, M, kd), lambda g, j: (g, 0, 0)),
            pl.BlockSpec((1, M, hw), lambda g, j: (g, 0, j)),
            pl.BlockSpec((1,) + tz.shape[1:], lambda g, j: (g, 0, 0, 0)),
            pl.BlockSpec((M, sw), lambda g, j: (0, g)),
            pl.BlockSpec((1, sw, hw), lambda g, j: (g, 0, j)),
            pl.BlockSpec((1, 1, hw), lambda g, j: (g, 0, j)),
        ],
        out_specs=pl.BlockSpec((1, M, hw), lambda g, j: (g, 0, j)),
        out_shape=jax.ShapeDtypeStruct((ngb, M, kd), BF16),
        scratch_shapes=[pltpu.VMEM((kd, hw), BF16)],
        compiler_params=_cparams(("parallel", "parallel")),
        name="s5_chunk_outputs",
    )(u, u, tz, h, w_in, d_slab)

    return y.reshape(ngb, nc_all, B, S5_CHUNK, LANES)


def _glu_kernel(y_ref, w_ref, b_ref, o_ref):
    slabs, chunks = y_ref.shape[:2]
    y = jnp.concatenate([y_ref[s, :, 0].reshape(chunks * S5_CHUNK, LANES) for s in range(slabs)], axis=1)
    z = jnp.dot(jax.nn.gelu(y.astype(F32)).astype(BF16), w_ref[...], preferred_element_type=F32) + b_ref[...]
    n = o_ref.shape[-1]
    o_ref[0] = (z[:, :n] * jax.nn.sigmoid(z[:, n:])).astype(o_ref.dtype)


def _glu(y, first_chunk, n_tokens, w, b, *, tm):
    slabs, _, B = y.shape[:3]
    W = slabs * LANES
    tm = min(tm, n_tokens)
    tc = tm // S5_CHUNK
    assert first_chunk % tc == 0
    off = first_chunk // tc
    return pl.pallas_call(
        _glu_kernel,
        grid=(B, n_tokens // tm),
        in_specs=[
            pl.BlockSpec((slabs, tc, 1, S5_CHUNK, LANES), lambda g, i: (0, off + i, g, 0, 0)),
            pl.BlockSpec((W, 2 * W), lambda g, i: (0, 0)),
            pl.BlockSpec((1, 2 * W), lambda g, i: (0, 0)),
        ],
        out_specs=pl.BlockSpec((1, tm, W), lambda g, i: (g, i, 0)),
        out_shape=jax.ShapeDtypeStruct((B, n_tokens, W), BF16),
        compiler_params=_cparams(("parallel", "parallel")),
        name="s5_glu",
    )(y, w, b.reshape(1, 2 * W))


def _out_res_kernel(*refs, widths):
    n = len(widths)
    lhs = refs[:n]
    w_ref, x_ref, gate_ref, g_ref, o_ref = refs[n:]
    y = None
    off = 0
    for r, k in zip(lhs, widths):
        part = jnp.dot(r[0], w_ref[off:off + k, :], preferred_element_type=F32)
        y = part if y is None else y + part
        off += k
    o_ref[0] = x_ref[0] + gate_ref[0] * (_rms(y) * g_ref[...])


def _out_res(parts, w, x, gate, g, *, tm):
    G, T, D = x.shape
    tm = min(tm, T)
    widths = tuple(p.shape[-1] for p in parts)
    K = sum(widths)
    in_specs = [pl.BlockSpec((1, tm, k), lambda b, i: (b, i, 0)) for k in widths]
    in_specs += [
        pl.BlockSpec((K, D), lambda b, i: (0, 0)),
        pl.BlockSpec((1, tm, D), lambda b, i: (b, i, 0)),
        pl.BlockSpec((1, 1, D), _vec_map(gate, 2)),
        pl.BlockSpec((1, D), lambda b, i: (0, 0)),
    ]
    return pl.pallas_call(
        functools.partial(_out_res_kernel, widths=widths),
        grid=(G, T // tm),
        in_specs=in_specs,
        out_specs=pl.BlockSpec((1, tm, D), lambda b, i: (b, i, 0)),
        out_shape=jax.ShapeDtypeStruct((G, T, D), F32),
        compiler_params=_cparams(("parallel", "parallel")),
        name="out_proj_residual",
    )(*parts, w, x, gate, g.reshape(1, D))


HALO = BF16_ROWS


def _ffn_kernel(x_ref, xp_ref, xn_ref, g2_ref, sh_ref, sc_ref, wa_ref, wv_ref, cwa_ref, cwv_ref,
                cba_ref, cbv_ref, wd_ref, gate_ref, g3_ref, o_ref, h_ref, acc_ref, ua_ref, uv_ref, *, tm):
    i = pl.program_id(1)
    f = pl.program_id(2)

    def normmod(x):
        h = _rms(x) * g2_ref[...]
        return h * (1.0 + sc_ref[0]) + sh_ref[0]

    @pl.when(f == 0)
    def _():
        zeros = jnp.zeros((HALO - SUBLANES, x_ref.shape[-1]), F32)
        hp = jnp.where(i == 0, 0.0, normmod(xp_ref[0]))
        hn = jnp.where(i == pl.num_programs(1) - 1, 0.0, normmod(xn_ref[0]))
        h_ref[0:HALO] = jnp.concatenate([zeros, hp], axis=0).astype(BF16)
        h_ref[HALO:HALO + tm] = normmod(x_ref[0]).astype(BF16)
        h_ref[HALO + tm:] = jnp.concatenate([hn, zeros], axis=0).astype(BF16)
        acc_ref[...] = jnp.zeros_like(acc_ref)

    h = h_ref[...]
    ua_ref[...] = jnp.dot(h, wa_ref[...], preferred_element_type=F32)
    uv_ref[...] = jnp.dot(h, wv_ref[...], preferred_element_type=F32)

    def conv(u_ref, cw_ref, cb_ref):
        return (u_ref[HALO - 1:HALO - 1 + tm] * cw_ref[0:1, :] + u_ref[HALO:HALO + tm] * cw_ref[1:2, :]
                + u_ref[HALO + 1:HALO + 1 + tm] * cw_ref[2:3, :] + cb_ref[...])

    a = conv(ua_ref, cwa_ref, cba_ref)
    v = conv(uv_ref, cwv_ref, cbv_ref)
    gated = (a * jax.nn.sigmoid(a) * v).astype(BF16)
    acc_ref[...] += jnp.dot(gated, wd_ref[...], preferred_element_type=F32)

    @pl.when(f == pl.num_programs(2) - 1)
    def _():
        o_ref[0] = x_ref[0] + gate_ref[0] * (_rms(acc_ref[...]) * g3_ref[...])


def _conv_ffn(x, g2, shift, scale, w_up, conv_w, conv_b, w_down, gate, g3, *, tm, tf):
    G, T, D = x.shape
    F = w_down.shape[0]
    tm = min(tm, T)
    nf = F // tf
    n_i = T // tm
    rb = tm // SUBLANES
    last_rb = T // SUBLANES - 1
    conv_b = conv_b.reshape(1, 2 * F)
    return pl.pallas_call(
        functools.partial(_ffn_kernel, tm=tm),
        grid=(G, n_i, nf),
        in_specs=[
            pl.BlockSpec((1, tm, D), lambda b, i, f: (b, i, 0), pipeline_mode=pl.Buffered(1)),
            pl.BlockSpec((1, SUBLANES, D), lambda b, i, f: (b, jnp.maximum(i * rb - 1, 0), 0)),
            pl.BlockSpec((1, SUBLANES, D), lambda b, i, f: (b, jnp.minimum((i + 1) * rb, last_rb), 0)),
            pl.BlockSpec((1, D), lambda b, i, f: (0, 0)),
            pl.BlockSpec((1, 1, D), _vec_map(shift, 3)),
            pl.BlockSpec((1, 1, D), _vec_map(scale, 3)),
            pl.BlockSpec((D, tf), lambda b, i, f: (0, f)),
            pl.BlockSpec((D, tf), lambda b, i, f: (0, nf + f)),
            pl.BlockSpec((3, tf), lambda b, i, f: (0, f)),
            pl.BlockSpec((3, tf), lambda b, i, f: (0, nf + f)),
            pl.BlockSpec((1, tf), lambda b, i, f: (0, f)),
            pl.BlockSpec((1, tf), lambda b, i, f: (0, nf + f)),
            pl.BlockSpec((tf, D), lambda b, i, f: (f, 0)),
            pl.BlockSpec((1, 1, D), _vec_map(gate, 3)),
            pl.BlockSpec((1, D), lambda b, i, f: (0, 0)),
        ],
        out_specs=pl.BlockSpec((1, tm, D), lambda b, i, f: (b, i, 0), pipeline_mode=pl.Buffered(1)),
        out_shape=jax.ShapeDtypeStruct((G, T, D), F32),
        scratch_shapes=[
            pltpu.VMEM((tm + 2 * HALO, D), BF16),
            pltpu.VMEM((tm, D), F32),
            pltpu.VMEM((tm + 2 * HALO, tf), F32),
            pltpu.VMEM((tm + 2 * HALO, tf), F32),
        ],
        compiler_params=_cparams(("parallel", "parallel", "arbitrary"), VMEM_LIMIT_LARGE),
        name="conv_ffn",
    )(x, x, x, g2.reshape(1, D), shift, scale, w_up, w_up, conv_w, conv_w, conv_b, conv_b, w_down,
      gate, g3.reshape(1, D))


GLA_BLOCKS = HGRN_CHUNK // SUBLANES


def _gla_operands(z, v, q, lb, rev):
    nb, rb = GLA_BLOCKS, SUBLANES
    order = list(range(nb))[::-1] if rev else list(range(nb))
    scan_of = {b: j for j, b in enumerate(order)}
    pos = lax.broadcasted_iota(jnp.int32, (rb, HGRN_DK), 0)
    if rev:
        pos = rb - 1 - pos

    def prev(x, k):
        return pltpu.roll(x, (rb - k) if rev else k, 0)

    def nxt(x, k):
        return pltpu.roll(x, k if rev else (rb - k), 0)

    def blocks(x):
        return [x[rb * b:rb * (b + 1)] for b in range(nb)]

    def rows(bl):
        return jnp.concatenate(bl, axis=0)

    f = lb + (1.0 - lb) * jax.nn.sigmoid(z)
    fb = blocks(f)
    kb = blocks(1.0 - f)

    hb = rb // 2
    hpos = pos & (hb - 1)
    lo_last = hb if rev else hb - 1
    hi_first = hb - 1 if rev else hb
    p4, s4, p8, s8, bt = [], [], [], [], []
    for b in range(nb):
        x = y = fb[b]
        for k in (1, 2):
            x = x * jnp.where(hpos >= k, prev(x, k), 1.0)
            y = y * jnp.where(hpos <= hb - 1 - k, nxt(y, k), 1.0)
        lo_tot = x[lo_last:lo_last + 1, :]
        hi_tot = y[hi_first:hi_first + 1, :]
        sfx = jnp.where(hpos <= hb - 2, nxt(y, 1), 1.0)
        p4.append(x)
        s4.append(sfx)
        p8.append(x * jnp.where(pos >= hb, lo_tot, 1.0))
        s8.append(sfx * jnp.where(pos < hb, hi_tot, 1.0))
        bt.append(lo_tot * hi_tot)
    bts = [bt[order[j]] for j in range(nb)]
    ones = jnp.ones_like(bts[0])
    before = [ones]
    for j in range(1, nb):
        before.append(before[j - 1] * bts[j - 1])
    after = [ones] * nb
    for j in range(nb - 2, -1, -1):
        after[j] = after[j + 1] * bts[j + 1]
    ftot = before[nb - 1] * bts[nb - 1]

    kbase = [kb[b] * s8[b] for b in range(nb)]
    ops = {"v": v, "ftot": ftot,
           "kdec": rows([kbase[b] * after[scan_of[b]] for b in range(nb)]).astype(BF16)}
    if q is None:
        return ops

    qb = blocks(q)
    q8 = [qb[b] * p8[b] for b in range(nb)]
    ops["q_state"] = rows([q8[b] * before[scan_of[b]] for b in range(nb)]).astype(BF16)
    ops["q_far"] = rows(q8).astype(BF16)
    ops["q_near"] = q.astype(BF16)

    vb = blocks(v.astype(F32))
    kcols, vcols = [], []
    chain = {}
    for d in range(1, nb):
        for j in range(nb - d):
            chain[j] = kbase[order[j]] if d == 1 else chain[j] * bts[j + d - 1]
            kcols.append(chain[j])
            vcols.append(vb[order[j]])
    ops["k_far"] = rows(kcols).astype(BF16)
    ops["v_far"] = rows(vcols).astype(BF16)

    ops["q_cross"] = rows([qb[b] * p4[b] for b in range(nb)]).astype(BF16)
    ops["k_cross"] = rows([kb[b] * s4[b] for b in range(nb)]).astype(BF16)

    kvar = []
    cur = kb
    for d in range(hb):
        if d:
            cur = [cur[b] * nxt(fb[b], d) for b in range(nb)]
        kvar.append(rows(cur).astype(BF16))
    ops["k_near"] = jnp.concatenate(kvar, axis=0)
    return ops


def _gla_first_dots(ops, st):
    nt = (((1,), (1,)), ((), ()))
    st_new = st * ops["ftot"] + lax.dot_general(ops["v"], ops["kdec"], (((0,), (0,)), ((), ())),
                                                preferred_element_type=F32)
    if "q_near" not in ops:
        return st_new, None
    o = lax.dot_general(ops["q_state"], st.astype(BF16), nt, preferred_element_type=F32)
    far = lax.dot_general(ops["q_far"], ops["k_far"], nt, preferred_element_type=F32)
    cross = lax.dot_general(ops["q_cross"], ops["k_cross"], nt, preferred_element_type=F32)
    near = lax.dot_general(ops["q_near"], ops["k_near"], nt, preferred_element_type=F32)
    return st_new, (o, far, cross, near)


def _gla_second_dots(ops, scores, masks):
    o, far, cross, near = scores
    m_far, m_cross, m_near = masks
    o = o + jnp.dot((far * m_far).astype(BF16), ops["v_far"], preferred_element_type=F32)
    o = o + jnp.dot((cross * m_cross).astype(BF16), ops["v"], preferred_element_type=F32)
    v_near = jnp.concatenate([ops["v"]] * (near.shape[1] // HGRN_CHUNK), axis=0)
    return o + jnp.dot((near * m_near).astype(BF16), v_near, preferred_element_type=F32)


def _operand_rows():
    T, nb, rb = HGRN_CHUNK, GLA_BLOCKS, SUBLANES
    far = rb * nb * (nb - 1) // 2
    sizes = [("kdec", T), ("q_state", T), ("q_far", T), ("q_near", T), ("q_cross", T), ("k_cross", T),
             ("v", T), ("k_far", far), ("v_far", far), ("k_near", (rb // 2) * T)]
    rows, off = {}, 0
    for name, n in sizes:
        rows[name] = (off, n)
        off += n
    return rows, off


GLA_OPERAND_ROWS, GLA_OPERAND_TOTAL = _operand_rows()


def _gla_mask_tables():
    T, nb, rb = HGRN_CHUNK, GLA_BLOCKS, SUBLANES
    hb = rb // 2
    t = np.arange(T)
    far, cross, near = [], [], []
    for rev in (False, True):
        sblk = (nb - 1 - t // rb) if rev else t // rb
        spos = (rb - 1 - t % rb) if rev else t % rb
        same_block = t[:, None] // rb == t[None, :] // rb
        cols = [(d, j) for d in range(1, nb) for j in range(nb - d)]
        m_far = np.zeros((T, len(cols) * rb), np.float32)
        for c, (d, j) in enumerate(cols):
            m_far[sblk == j + d, c * rb:(c + 1) * rb] = 1.0
        m_cross = same_block & (spos[:, None] >= hb) & (spos[None, :] < hb)
        same_half = same_block & (spos[:, None] // hb == spos[None, :] // hb)
        m_near = np.zeros((T, hb * T), np.float32)
        for d in range(hb):
            m_near[:, d * T:(d + 1) * T] = same_half & (spos[:, None] - spos[None, :] == d)
        far.append(m_far)
        cross.append(m_cross.astype(np.float32))
        near.append(m_near)
    return tuple(jnp.asarray(np.stack(m)) for m in (far, cross, near))


def _gla_kernel(q_ref, zf_ref, zb_ref, v_ref, g_ref, czf_ref, czb_ref, cv_ref, lb_ref, ng_ref,
                mfar_ref, mcross_ref, mnear_ref, o_ref, oacc_ref, ops_a_ref, ft_a_ref, ops_b_ref, ft_b_ref,
                *, n_lat, n_ctx, hp):
    T, dk = HGRN_CHUNK, HGRN_DK
    jobs = [(h, rev) for h in range(hp) for rev in (False, True)]
    zero_state = jnp.zeros((dk, dk), F32)

    def cols(h):
        return slice(h * dk, (h + 1) * dk)

    def start(rev, i, n):
        r = ((n - 1 - i) if rev else i) * T
        return r if isinstance(r, int) else pl.multiple_of(r, T)

    def lb_row(h, rev):
        return lb_ref[int(rev):int(rev) + 1, cols(h)]

    def prep_ctx(i):
        ops = []
        for h, rev in jobs:
            r = start(rev, i, n_ctx)
            z = (czb_ref if rev else czf_ref)[0, pl.ds(r, T), cols(h)].astype(F32)
            ops.append(_gla_operands(z, cv_ref[0, pl.ds(r, T), cols(h)], None, lb_row(h, rev), rev))
        return tuple(ops)

    def prep_lat(i):
        ops = []
        for h, rev in jobs:
            r = start(rev, i, n_lat)
            qz = q_ref[0, pl.ds(r, T), cols(h)].astype(F32)
            z = (zb_ref if rev else zf_ref)[0, pl.ds(r, T), cols(h)].astype(F32)
            ops.append(_gla_operands(z, v_ref[0, pl.ds(r, T), cols(h)], qz * jax.nn.sigmoid(qz),
                                     lb_row(h, rev), rev))
        return tuple(ops)

    def stash(ops, buf):
        ops_ref, ft_ref = buf
        for j, o in enumerate(ops):
            for name, (off, n) in GLA_OPERAND_ROWS.items():
                ops_ref[j, off:off + n, :] = o[name]
            ft_ref[j] = jnp.broadcast_to(o["ftot"], (SUBLANES, dk))

    def fetch(buf):
        ops_ref, ft_ref = buf
        out = []
        for j in range(len(jobs)):
            o = {name: ops_ref[j, off:off + n, :] for name, (off, n) in GLA_OPERAND_ROWS.items()}
            o["ftot"] = ft_ref[j, 0:1, :]
            out.append(o)
        return out

    def ctx_step(i, sts):
        return tuple(_gla_first_dots(o, st)[0] for o, st in zip(prep_ctx(i), sts))

    def lat_step(i, sts, second_half, cur, nxt):
        ops = fetch(cur)
        firsts = [_gla_first_dots(o, st) for o, st in zip(ops, sts)]
        stash(prep_lat(jnp.minimum(i + 1, n_lat - 1)), nxt)
        outs = [_gla_second_dots(o, f[1], tuple(m[int(rev)] for m in (mfar_ref, mcross_ref, mnear_ref)))
                for o, f, (_, rev) in zip(ops, firsts, jobs)]
        for (h, rev), o in zip(jobs, outs):
            r = start(rev, i, n_lat)
            if second_half:
                o = o + oacc_ref[pl.ds(r, T), cols(h)]
                gz = g_ref[0, pl.ds(r, T), cols(h)].astype(F32)
                o = _rms(o) * ng_ref[...] * (gz * jax.nn.sigmoid(gz))
                o_ref[0, pl.ds(r, T), cols(h)] = o.astype(o_ref.dtype)
            else:
                oacc_ref[pl.ds(r, T), cols(h)] = o
        return tuple(f[0] for f in firsts)

    buf_a, buf_b = (ops_a_ref, ft_a_ref), (ops_b_ref, ft_b_ref)

    def lat_pair(k, sts, second_half):
        sts = lat_step(2 * k, sts, second_half, buf_a, buf_b)
        return lat_step(2 * k + 1, sts, second_half, buf_b, buf_a)

    sts = lax.fori_loop(0, n_ctx, ctx_step, (zero_state,) * len(jobs))
    stash(prep_lat(0), buf_a)
    quarter = n_lat // 4
    sts = lax.fori_loop(0, quarter, functools.partial(lat_pair, second_half=False), sts)
    lax.fori_loop(quarter, 2 * quarter, functools.partial(lat_pair, second_half=True), sts)


GLA_HEADS_PER_STEP = 2


def _gla(p_lat, p_ctx, lb, norm_g, n_heads):
    B, L, _ = p_lat.shape
    Lc = p_ctx.shape[1]
    hp = GLA_HEADS_PER_STEP
    steps = n_heads // hp
    dv = HGRN_DK
    w = hp * dv

    def col(group):
        return lambda b, h: (b, 0, group * steps + h)

    lat_specs = [pl.BlockSpec((1, L, w), col(gidx)) for gidx in range(5)]
    ctx_specs = [pl.BlockSpec((1, Lc, w), col(gidx)) for gidx in (1, 2, 3)]
    masks = _gla_mask_tables()
    n_lat = L // HGRN_CHUNK
    assert n_lat % 4 == 0 and n_heads % hp == 0
    operand_bufs = [pltpu.VMEM((2 * hp, GLA_OPERAND_TOTAL, dv), BF16), pltpu.VMEM((2 * hp, SUBLANES, dv), F32)]
    return pl.pallas_call(
        functools.partial(_gla_kernel, n_lat=n_lat, n_ctx=Lc // HGRN_CHUNK, hp=hp),
        grid=(B, steps),
        in_specs=lat_specs + ctx_specs + [
            pl.BlockSpec((2, w), lambda b, h: (0, h)),
            pl.BlockSpec((1, dv), lambda b, h: (0, 0)),
        ] + [pl.BlockSpec(m.shape, lambda b, h: (0, 0, 0)) for m in masks],
        out_specs=pl.BlockSpec((1, L, w), lambda b, h: (b, 0, h)),
        out_shape=jax.ShapeDtypeStruct((B, L, n_heads * dv), BF16),
        scratch_shapes=[pltpu.VMEM((L, w), F32)] + operand_bufs * 2,
        compiler_params=_cparams(("parallel", "parallel")),
        name="hgrn2_bidirectional",
    )(*([p_lat] * 5), *([p_ctx] * 3), lb, norm_g.reshape(1, dv), *masks)


def _rope_tables(n_tokens):
    rows = n_tokens // GRID_W
    row = jnp.repeat(jnp.arange(rows, dtype=F32), GRID_W)
    colp = jnp.tile(jnp.arange(GRID_W, dtype=F32), rows)
    inv = ROPE_THETA ** (-jnp.arange(AXIS_FREQS, dtype=F32) / AXIS_FREQS)
    ang = jnp.stack([row[:, None] * inv, colp[:, None] * inv], axis=1)
    cos, sin = jnp.cos(ang), jnp.sin(ang)
    zero = jnp.zeros_like(sin)
    full = lambda a, b: jnp.stack([a, b], axis=2).reshape(n_tokens, HEAD_DIM)
    return full(cos, cos), full(-sin, zero), full(zero, sin)


def _identity_rope(n_tokens):
    return (jnp.ones((n_tokens, HEAD_DIM), F32), jnp.zeros((n_tokens, HEAD_DIM), F32),
            jnp.zeros((n_tokens, HEAD_DIM), F32))


def kernel(x, c, ctx, c_ctx, mod_w, mod_b, norm_g, ab_w_in, ab_w_out, attn_q_norm, attn_k_norm,
           s5_lam_re, s5_lam_im, s5_log_dt, s5_b_re, s5_b_im, s5_c_re, s5_c_im, s5_d,
           s5_glu_w, s5_glu_b, c_w_in, c_w_out, hgrn_lb_logits, hgrn_norm,
           ffn_w_up, ffn_conv_w, ffn_conv_b, ffn_w_down):
    B, L, D = x.shape
    Lc = ctx.shape[1]
    depth = mod_w.shape[0]
    attn_w = ATTN_HEADS * HEAD_DIM
    kv_w = ATTN_KV_HEADS * HEAD_DIM
    qk_w = attn_w + kv_w

    lb_all = jnp.cumsum(jax.nn.softmax(hgrn_lb_logits.astype(F32), axis=0), axis=0)
    lb_all = lb_all - lb_all[:1]

    cvec = jnp.concatenate([c, c_ctx[None], jnp.zeros((SUBLANES - (B + 1) % SUBLANES, D), F32)], axis=0)
    mods = _modulation(cvec, mod_w, mod_b)

    rope_lat = _rope_tables(L)
    rope_ctx = _identity_rope(Lc)

    for l in range(depth):
        last = l == depth - 1
        m_lat = [mods[l, :B, k * D:(k + 1) * D][:, None, :] for k in range(6)]
        m_ctx = [mods[l, B:B + 1, k * D:(k + 1) * D][:, None, :] for k in range(6)]
        g = norm_g[l]
        w_up = ffn_w_up[l].astype(BF16)
        w_down = ffn_w_down[l].astype(BF16)
        if l % 2 == 0:
            e = l // 2
            w_in = ab_w_in[e].astype(BF16)
            gain = jnp.concatenate([jnp.tile(attn_q_norm[e] * (HEAD_DIM ** -0.5), ATTN_HEADS),
                                    jnp.tile(attn_k_norm[e], ATTN_KV_HEADS),
                                    jnp.ones((kv_w,), F32)]).reshape(1, qk_w + kv_w)
            p_lat, u_lat = _proj_ab(x, g[0], m_lat[0], m_lat[1], w_in, gain, rope_lat, tm=1024)
            p_ctx, u_ctx = _proj_ab(ctx, g[0], m_ctx[0], m_ctx[1], w_in, gain, rope_ctx, tm=1024)
            a_lat = _attention(p_lat, p_ctx, tq=256)
            tables = _s5_tables(s5_lam_re[e], s5_lam_im[e], s5_log_dt[e], s5_b_re[e], s5_b_im[e],
                                s5_c_re[e], s5_c_im[e], s5_d[e])
            y = _s5(u_ctx, u_lat, tables)
            glu_w = s5_glu_w[e].astype(BF16)
            w_out = ab_w_out[e].astype(BF16)
            s_lat = _glu(y, Lc // S5_CHUNK, L, glu_w, s5_glu_b[e], tm=256)
            x = _out_res([a_lat, s_lat], w_out, x, m_lat[2], g[1], tm=512)
            if not last:
                a_ctx = _attention(p_ctx, None, tq=128)
                s_ctx = _glu(y, 0, Lc, glu_w, s5_glu_b[e], tm=256)
                ctx = _out_res([a_ctx, s_ctx], w_out, ctx, m_ctx[2], g[1], tm=512)
        else:
            o_idx = l // 2
            w_in = c_w_in[o_idx].astype(BF16)
            n_heads = c_w_out.shape[1] // HGRN_DK
            p_lat = _proj(x, g[0], m_lat[0], m_lat[1], w_in, tm=1024, tn=2048)
            p_ctx = _proj(ctx, g[0], m_ctx[0], m_ctx[1], w_in, tm=1024, tn=2048)
            o_lat = _gla(p_lat, p_ctx, lb_all[l], hgrn_norm[o_idx], n_heads)
            x = _out_res([o_lat], c_w_out[o_idx].astype(BF16), x, m_lat[2], g[1], tm=512)
            assert last, "context outputs of the HGRN2 mixer are only needed by a following layer"
        x = _conv_ffn(x, g[2], m_lat[3], m_lat[4], w_up, ffn_conv_w[l], ffn_conv_b[l], w_down,
                      m_lat[5], g[3], tm=1024, tf=512)
        if not last:
            ctx = _conv_ffn(ctx, g[2], m_ctx[3], m_ctx[4], w_up, ffn_conv_w[l], ffn_conv_b[l], w_down,
                            m_ctx[5], g[3], tm=512, tf=512)
    return x
```

```python
import functools
import math

import jax
import jax.numpy as jnp
import numpy as np
from jax import lax
from jax.experimental import pallas as pl
from jax.experimental.pallas import tpu as pltpu

F32 = jnp.float32
BF16 = jnp.bfloat16
EPS = 1e-6

LANES = 128
SUBLANES = 8
BF16_ROWS = 16

HEAD_DIM = 128
ATTN_HEADS = 8
ATTN_KV_HEADS = 2
ATTN_GROUP = ATTN_HEADS // ATTN_KV_HEADS
GRID_W = 64
ROPE_THETA = 10000.0
AXIS_FREQS = HEAD_DIM // 4

S5_GROUP = 16
S5_STATE = 64
S5_CHUNK = 16

HGRN_DK = 128
HGRN_CHUNK = 64

VMEM_LIMIT = 56 * 1024 * 1024


def _cparams(sem):
    return pltpu.CompilerParams(dimension_semantics=sem, vmem_limit_bytes=VMEM_LIMIT)


def _vec_map(vec, nd_grid):
    shared = vec.shape[0] == 1
    if nd_grid == 2:
        return lambda g, i: (0 if shared else g, 0, 0)
    return lambda g, i, j: (0 if shared else g, 0, 0)


def _rms(x):
    return x * lax.rsqrt(jnp.mean(x * x, axis=-1, keepdims=True) + EPS)


def _mod_kernel(s_ref, w_ref, b_ref, o_ref):
    s = s_ref[...]
    a = s * jax.nn.sigmoid(s)
    o_ref[0] = jnp.dot(a.astype(BF16), w_ref[0].astype(BF16), preferred_element_type=F32) + b_ref[0]


def _modulation(cvec, mod_w, mod_b):
    depth, d, n = mod_w.shape
    rows = cvec.shape[0]
    tn = 1024
    return pl.pallas_call(
        _mod_kernel,
        grid=(depth, n // tn),
        in_specs=[
            pl.BlockSpec((rows, d), lambda l, j: (0, 0)),
            pl.BlockSpec((1, d, tn), lambda l, j: (l, 0, j)),
            pl.BlockSpec((1, 1, tn), lambda l, j: (l, 0, j)),
        ],
        out_specs=pl.BlockSpec((1, rows, tn), lambda l, j: (l, 0, j)),
        out_shape=jax.ShapeDtypeStruct((depth, rows, n), F32),
        compiler_params=_cparams(("parallel", "parallel")),
        name="modulation",
    )(cvec, mod_w, mod_b.reshape(depth, 1, n))


NORM_ROWS = 32


def _row_blocks(n_rows):
    step = NORM_ROWS if n_rows % NORM_ROWS == 0 else n_rows
    return [slice(r, r + step) for r in range(0, n_rows, step)]


def _norm_mod_to(h_ref, x_ref, g_ref, sh_ref, sc_ref):
    gain = g_ref[...] * (1.0 + sc_ref[0])
    for rows in _row_blocks(h_ref.shape[0]):
        h_ref[rows] = (_rms(x_ref[0, rows]) * gain + sh_ref[0]).astype(BF16)


def _proj_kernel(x_ref, g_ref, sh_ref, sc_ref, w_ref, o_ref, h_ref):
    @pl.when(pl.program_id(2) == 0)
    def _():
        _norm_mod_to(h_ref, x_ref, g_ref, sh_ref, sc_ref)

    o_ref[0] = jnp.dot(h_ref[...], w_ref[...], preferred_element_type=F32).astype(o_ref.dtype)


def _proj_specs(x, g, shift, scale, w, tm, tn):
    G, T, D = x.shape
    in_specs = [
        pl.BlockSpec((1, tm, D), lambda b, i, j: (b, i, 0)),
        pl.BlockSpec((1, D), lambda b, i, j: (0, 0)),
        pl.BlockSpec((1, 1, D), _vec_map(shift, 3)),
        pl.BlockSpec((1, 1, D), _vec_map(scale, 3)),
        pl.BlockSpec((D, tn), lambda b, i, j: (0, j)),
    ]
    return in_specs, [x, g.reshape(1, D), shift, scale, w]


def _proj(x, g, shift, scale, w, *, tm, tn, out_dtype=BF16):
    G, T, D = x.shape
    N = w.shape[1]
    tm = min(tm, T)
    in_specs, args = _proj_specs(x, g, shift, scale, w, tm, tn)
    return pl.pallas_call(
        _proj_kernel,
        grid=(G, T // tm, N // tn),
        in_specs=in_specs,
        out_specs=pl.BlockSpec((1, tm, tn), lambda b, i, j: (b, i, j)),
        out_shape=jax.ShapeDtypeStruct((G, T, N), out_dtype),
        scratch_shapes=[pltpu.VMEM((tm, D), BF16)],
        compiler_params=_cparams(("parallel", "parallel", "arbitrary")),
        name="norm_mod_proj",
    )(*args)


AB_TILE_HEADS = 4
AB_QKV_TILES = (ATTN_HEADS + 2 * ATTN_KV_HEADS) // AB_TILE_HEADS


def _proj_ab_kernel(x_ref, g_ref, sh_ref, sc_ref, w_ref, gain_ref, cos_ref, s1_ref, s2_ref,
                    qkv_ref, u_ref, h_ref):
    j = pl.program_id(2)

    @pl.when(j == 0)
    def _():
        _norm_mod_to(h_ref, x_ref, g_ref, sh_ref, sc_ref)

    acc = jnp.dot(h_ref[...], w_ref[...], preferred_element_type=F32)
    k_tile = ATTN_HEADS // AB_TILE_HEADS

    def head(hh):
        return slice(hh * HEAD_DIM, (hh + 1) * HEAD_DIM)

    def normed_rotated(hh):
        y = _rms(acc[:, head(hh)]) * gain_ref[:, head(hh)]
        return (y * cos_ref[...] + pltpu.roll(y, HEAD_DIM - AXIS_FREQS, 1) * s1_ref[...]
                + pltpu.roll(y, AXIS_FREQS, 1) * s2_ref[...])

    @pl.when(j < k_tile)
    def _():
        for hh in range(AB_TILE_HEADS):
            qkv_ref[0, :, head(hh)] = normed_rotated(hh).astype(qkv_ref.dtype)

    @pl.when(j == k_tile)
    def _():
        for hh in range(AB_TILE_HEADS):
            y = normed_rotated(hh) if hh < ATTN_KV_HEADS else acc[:, head(hh)]
            qkv_ref[0, :, head(hh)] = y.astype(qkv_ref.dtype)

    @pl.when(j > k_tile)
    def _():
        chunks = u_ref.shape[1]
        for s in range(AB_TILE_HEADS):
            u_ref[s, :, 0] = acc[:, head(s)].astype(u_ref.dtype).reshape(chunks, S5_CHUNK, LANES)


def _proj_ab(x, g, shift, scale, w, gain, rope, *, tm):
    G, T, D = x.shape
    tm = min(tm, T)
    tn = AB_TILE_HEADS * HEAD_DIM
    n_tiles = w.shape[1] // tn
    slabs = (n_tiles - AB_QKV_TILES) * AB_TILE_HEADS
    in_specs, args = _proj_specs(x, g, shift, scale, w, tm, tn)
    last_qkv = AB_QKV_TILES - 1
    in_specs += [pl.BlockSpec((1, tn), lambda b, i, j: (0, jnp.minimum(j, last_qkv)))]
    in_specs += [pl.BlockSpec((tm, HEAD_DIM), lambda b, i, j: (i, 0))] * 3
    return pl.pallas_call(
        _proj_ab_kernel,
        grid=(G, T // tm, n_tiles),
        in_specs=in_specs,
        out_specs=[
            pl.BlockSpec((1, tm, tn), lambda b, i, j: (b, i, jnp.minimum(j, last_qkv))),
            pl.BlockSpec((AB_TILE_HEADS, tm // S5_CHUNK, 1, S5_CHUNK, LANES),
                         lambda b, i, j: (jnp.maximum(j - AB_QKV_TILES, 0), i, b, 0, 0)),
        ],
        out_shape=[
            jax.ShapeDtypeStruct((G, T, AB_QKV_TILES * tn), BF16),
            jax.ShapeDtypeStruct((slabs, T // S5_CHUNK, G, S5_CHUNK, LANES), BF16),
        ],
        scratch_shapes=[pltpu.VMEM((tm, D), BF16)],
        compiler_params=_cparams(("parallel", "parallel", "arbitrary")),
        name="norm_mod_proj_ab",
    )(*args, gain, *rope)


ATTN_KEY_CHUNK = 512


def _attn_kernel(q_ref, k_ref, v_ref, *rest, tq, has_prefix):
    if has_prefix:
        kp_ref, vp_ref, o_ref = rest
    else:
        (o_ref,) = rest
    nt = (((1,), (1,)), ((), ()))
    q = jnp.concatenate(
        [q_ref[0, :, g * HEAD_DIM:(g + 1) * HEAD_DIM] for g in range(ATTN_GROUP)], axis=0)
    n_keys = k_ref.shape[1]
    kc = min(ATTN_KEY_CHUNK, n_keys)
    chunks = [(kp_ref, vp_ref, 0, kp_ref.shape[1])] if has_prefix else []
    chunks += [(k_ref, v_ref, c * kc, kc) for c in range(n_keys // kc)]

    def scores(chunk):
        kr, _, start, size = chunk
        return lax.dot_general(q, kr[0, start:start + size, :], nt, preferred_element_type=F32)

    m = jnp.full((q.shape[0], 1), -1e30, F32)
    l = jnp.zeros((q.shape[0], 1), F32)
    o = jnp.zeros((q.shape[0], HEAD_DIM), F32)
    s_next = scores(chunks[0])
    for ci, (_, vr, start, size) in enumerate(chunks):
        s = s_next
        if ci + 1 < len(chunks):
            s_next = scores(chunks[ci + 1])
        m_new = jnp.maximum(m, jnp.max(s, axis=-1, keepdims=True))
        alpha = jnp.exp(m - m_new)
        p = jnp.exp(s - m_new)
        l = alpha * l + jnp.sum(p, axis=-1, keepdims=True)
        o = alpha * o + jnp.dot(p.astype(BF16), vr[0, start:start + size, :], preferred_element_type=F32)
        m = m_new
    o = o * (1.0 / l)
    for g in range(ATTN_GROUP):
        o_ref[0, :, g * HEAD_DIM:(g + 1) * HEAD_DIM] = o[g * tq:(g + 1) * tq].astype(o_ref.dtype)


def _attention(qkvu, prefix, *, tq):
    B, L, _ = qkvu.shape
    tq = min(tq, L)
    gw = ATTN_GROUP * HEAD_DIM
    k_blk = ATTN_HEADS
    v_blk = ATTN_HEADS + ATTN_KV_HEADS
    in_specs = [
        pl.BlockSpec((1, tq, gw), lambda b, h, i: (b, i, h)),
        pl.BlockSpec((1, L, HEAD_DIM), lambda b, h, i: (b, 0, k_blk + h)),
        pl.BlockSpec((1, L, HEAD_DIM), lambda b, h, i: (b, 0, v_blk + h)),
    ]
    args = [qkvu, qkvu, qkvu]
    if prefix is not None:
        Lp = prefix.shape[1]
        in_specs += [
            pl.BlockSpec((1, Lp, HEAD_DIM), lambda b, h, i: (b, 0, k_blk + h)),
            pl.BlockSpec((1, Lp, HEAD_DIM), lambda b, h, i: (b, 0, v_blk + h)),
        ]
        args += [prefix, prefix]
    return pl.pallas_call(
        functools.partial(_attn_kernel, tq=tq, has_prefix=prefix is not None),
        grid=(B, ATTN_KV_HEADS, L // tq),
        in_specs=in_specs,
        out_specs=pl.BlockSpec((1, tq, gw), lambda b, h, i: (b, i, h)),
        out_shape=jax.ShapeDtypeStruct((B, L, ATTN_HEADS * HEAD_DIM), BF16),
        compiler_params=_cparams(("parallel", "parallel", "arbitrary")),
        name="gqa_attention",
    )(*args)


S5_SLAB_GROUPS = LANES // S5_GROUP
S5_KD = S5_CHUNK * LANES
S5_PAIR_COLS = 4 * LANES


def _s5_end_kernel(u_ref, w_ref, e_ref):
    e_ref[...] = jnp.dot(u_ref[0], w_ref[0], preferred_element_type=F32)


def _s5_scan_kernel(e_ref, cf_ref, cb_ref, h_ref, *, batch, nc_ctx, nc_all):
    cps = SUBLANES // batch
    g_ctx, g_all = nc_ctx // cps, nc_all // cps

    def advance(hr, hi, er, ei, c_ref):
        ar, ai = c_ref[0:1, :], c_ref[1:2, :]
        return ar * hr - ai * hi + er, ar * hi + ai * hr + ei

    def sweep(r, state, col0, c_ref, order):
        e8 = e_ref[pl.ds(r, SUBLANES), col0:col0 + 2 * LANES]
        hr, hi = state
        out_r, out_i = [None] * cps, [None] * cps
        for c in order:
            out_r[c], out_i[c] = hr, hi
            rows = slice(c * batch, (c + 1) * batch)
            hr, hi = advance(hr, hi, e8[rows, :LANES], e8[rows, LANES:], c_ref)
        h_ref[pl.ds(r, SUBLANES), col0:col0 + LANES] = jnp.concatenate(out_r, axis=0).astype(h_ref.dtype)
        h_ref[pl.ds(r, SUBLANES), col0 + LANES:col0 + 2 * LANES] = (
            jnp.concatenate(out_i, axis=0).astype(h_ref.dtype))
        return hr, hi

    def body(i, carry):
        fwd, bwd = carry
        fwd = sweep(pl.multiple_of(i * SUBLANES, SUBLANES), fwd, 0, cf_ref, range(cps))
        gi = jnp.where(i < g_ctx, g_ctx - 1 - i, g_all - 1 - (i - g_ctx))
        bwd = sweep(pl.multiple_of(gi * SUBLANES, SUBLANES), bwd, 2 * LANES, cb_ref, reversed(range(cps)))
        return fwd, bwd

    zero = jnp.zeros((batch, LANES), F32)
    lax.fori_loop(0, g_all, body, ((zero, zero), (zero, zero)))


def _s5_out_kernel(u_ref, us_ref, lag_ref, h_ref, win_ref, d_ref, y_ref, tz_ref):
    T = S5_CHUNK
    t_blocks = tz_ref.shape[1] // LANES
    t0 = pl.program_id(1) * t_blocks
    for tt in range(t_blocks):
        for s in range(T):
            tz_ref[s * LANES:(s + 1) * LANES, tt * LANES:(tt + 1) * LANES] = lag_ref[0, t0 + tt - s + (T - 1)]
    y = jnp.dot(u_ref[0], tz_ref[...], preferred_element_type=F32)
    y += jnp.dot(h_ref[...].astype(BF16), win_ref[0], preferred_element_type=F32)
    y += us_ref[0].astype(F32) * d_ref[0]
    y_ref[0] = y.astype(y_ref.dtype)


def _s5_tables(lam_re, lam_im, log_dt, b_re, b_im, c_re, c_im, d_skip):
    T, C, P = S5_CHUNK, S5_GROUP, S5_STATE
    G = lam_re.shape[1]
    hi = lax.Precision.HIGHEST
    lam = lax.complex(jnp.minimum(lam_re.astype(F32), -1e-4), lam_im.astype(F32))
    dt = jnp.exp(log_dt.astype(F32))[..., None]
    lam_dt = lam * dt
    lam_bar = jnp.exp(lam_dt)
    bmat = lax.complex(b_re.astype(F32), b_im.astype(F32))
    b_bar = ((lam_bar - 1.0) / lam)[..., None] * bmat
    cmat = lax.complex(c_re.astype(F32), c_im.astype(F32))
    steps = jnp.arange(T + 1, dtype=F32)
    pw = jnp.exp(lam_dt[:, None] * steps[None, :, None, None])

    kern = jnp.real(jnp.einsum('zgcp,zjgp,zgpd->zjgcd', cmat, pw[:, :T], b_bar, precision=hi))

    def end_w(p_sel, bb):
        w = p_sel[:, :, None, :] * bb.transpose(0, 2, 1)[None]
        return [jnp.real(w), jnp.imag(w)]

    w_end = jnp.concatenate(end_w(pw[0, :T][::-1], b_bar[0]) + end_w(pw[1, :T], b_bar[1]), axis=-1)

    def in_w(p_sel, cm):
        z = cm.transpose(0, 2, 1)[:, :, None, :] * p_sel.transpose(1, 2, 0)[:, :, :, None]
        return [jnp.real(z), -jnp.imag(z)]

    w_in = jnp.stack(in_w(pw[0, 1:], cmat[0]) + in_w(pw[1, 1:][::-1], cmat[1]), axis=1)

    def coef(z):
        return jnp.stack([jnp.real(z).reshape(G * P), jnp.imag(z).reshape(G * P)])

    sg = S5_SLAB_GROUPS
    ngb, npair = G // sg, sg // 2
    grp = np.arange(sg)
    tok_cols_grp = np.tile(np.repeat(grp, C), T)
    state_cols_grp = (2 * np.arange(npair)[:, None, None, None] + np.arange(2)[None, None, :, None]
                      + np.zeros((1, 4, 1, P), np.int64)).reshape(-1)
    rows_gd = np.repeat(grp, C)

    lag_k = jnp.concatenate([kern[1][1:][::-1], (kern[0][0] + kern[1][0])[None], kern[0][1:]], axis=0)
    lag_src = lag_k.reshape(2 * T - 1, ngb, sg, C, C).transpose(1, 0, 2, 4, 3).reshape(ngb * (2 * T - 1), LANES, C)
    rep_c = np.tile(np.eye(C, dtype=np.float32), (1, sg))
    mask_c = (rows_gd[:, None] == np.repeat(grp, C)[None, :]).astype(np.float32)[None]
    lag_tiles = _spread(lag_src, rep_c, mask_c, lambda n: 0).reshape(ngb, 2 * T - 1, LANES, LANES)

    end_src = w_end.reshape(T, ngb, sg * C, 4 * P).transpose(1, 0, 2, 3).reshape(ngb * T, LANES, 4 * P)
    rep_state = np.tile(np.eye(4 * P, dtype=np.float32).reshape(4 * P, 1, 4, 1, P), (1, npair, 1, 2, 1))
    rep_state = rep_state.reshape(4 * P, npair * S5_PAIR_COLS)
    mask_state = (rows_gd[:, None] == state_cols_grp[None, :]).astype(np.float32)[None]
    end_slab = _spread(end_src, rep_state, mask_state, lambda n: 0).reshape(ngb, S5_KD, npair * S5_PAIR_COLS)

    in_src = w_in.reshape(ngb, npair, 2, 4, P, T * C).transpose(0, 1, 3, 2, 4, 5)
    in_src = in_src.reshape(ngb * npair * 4, 2 * P, T * C)
    rep_tok = np.tile(np.eye(T * C, dtype=np.float32).reshape(T * C, T, 1, C), (1, 1, sg, 1)).reshape(T * C, S5_KD)
    row_grp = 2 * np.arange(npair)[:, None, None] + np.repeat(np.arange(2), P)[None, :, None]
    mask_tok = (row_grp == tok_cols_grp[None, None, :]).astype(np.float32)
    in_slab = _spread(in_src, rep_tok, mask_tok, lambda n: (n // 4) % npair)
    in_slab = in_slab.reshape(ngb, npair * S5_PAIR_COLS, S5_KD)

    d_slab = jnp.tile(d_skip.astype(F32).reshape(ngb, 1, LANES), (1, 1, T))
    return lag_tiles, end_slab, in_slab, coef(pw[0, T]), coef(pw[1, T]), d_slab


SPREAD_TILES = 4


def _spread_kernel(src_ref, rep_ref, mask_ref, o_ref):
    for i in range(SPREAD_TILES):
        o_ref[i] = (jnp.dot(src_ref[i].astype(BF16), rep_ref[...], preferred_element_type=F32)
                    * mask_ref[0]).astype(o_ref.dtype)


def _spread(src, rep, mask, mask_index):
    n, rows, k = src.shape
    width = rep.shape[1]
    st = SPREAD_TILES
    assert n % st == 0
    return pl.pallas_call(
        _spread_kernel,
        grid=(n // st,),
        in_specs=[
            pl.BlockSpec((st, rows, k), lambda i: (i, 0, 0)),
            pl.BlockSpec((k, width), lambda i: (0, 0)),
            pl.BlockSpec((1, rows, width), lambda i: (mask_index(i * st), 0, 0)),
        ],
        out_specs=pl.BlockSpec((st, rows, width), lambda i: (i, 0, 0)),
        out_shape=jax.ShapeDtypeStruct((n, rows, width), BF16),
        compiler_params=_cparams(("parallel",)),
        name="s5_spread_operator",
    )(src, jnp.asarray(rep, BF16), jnp.asarray(mask, F32))


def _s5(u_ctx, u_lat, tables):
    tz, w_end, w_in, cf, cb, d_slab = tables
    ngb, nc_ctx, B = u_ctx.shape[:3]
    nc_all = nc_ctx + u_lat.shape[1]
    cps = SUBLANES // B
    assert SUBLANES % B == 0 and nc_ctx % cps == 0 and (nc_all - nc_ctx) % cps == 0
    M = nc_all * B
    kd, sw = S5_KD, w_end.shape[-1]
    n_half = 4
    hw = kd // n_half
    u = jnp.concatenate([u_ctx, u_lat], axis=1).reshape(ngb, M, kd)

    e = pl.pallas_call(
        _s5_end_kernel,
        grid=(ngb, n_half),
        in_specs=[
            pl.BlockSpec((1, M, kd), lambda g, j: (g, 0, 0)),
            pl.BlockSpec((1, kd, sw // n_half), lambda g, j: (g, 0, j)),
        ],
        out_specs=pl.BlockSpec((M, sw // n_half), lambda g, j: (0, g * n_half + j)),
        out_shape=jax.ShapeDtypeStruct((M, ngb * sw), F32),
        compiler_params=_cparams(("parallel", "parallel")),
        name="s5_chunk_end_states",
    )(u, w_end)

    h = pl.pallas_call(
        functools.partial(_s5_scan_kernel, batch=B, nc_ctx=nc_ctx, nc_all=nc_all),
        grid=(ngb * sw // S5_PAIR_COLS,),
        in_specs=[pl.BlockSpec((M, S5_PAIR_COLS), lambda j: (0, j))]
        + [pl.BlockSpec((2, LANES), lambda j: (0, j))] * 2,
        out_specs=pl.BlockSpec((M, S5_PAIR_COLS), lambda j: (0, j)),
        out_shape=jax.ShapeDtypeStruct((M, ngb * sw), F32),
        compiler_params=_cparams(("parallel",)),
        name="s5_chunk_scan",
    )(e, cf, cb)

    y = pl.pallas_call(
        _s5_out_kernel,
        grid=(ngb, n_half),
        in_specs=[
            pl.BlockSpec((1, M, kd), lambda g, j: (g, 0, 0)),
            pl.BlockSpec((1, M, hw), lambda g, j: (g, 0, j)),
            pl.BlockSpec((1,) + tz.shape[1:], lambda g, j: (g, 0, 0, 0)),
            pl.BlockSpec((M, sw), lambda g, j: (0, g)),
            pl.BlockSpec((1, sw, hw), lambda g, j: (g, 0, j)),
            pl.BlockSpec((1, 1, hw), lambda g, j: (g, 0, j)),
        ],
        out_specs=pl.BlockSpec((1, M, hw), lambda g, j: (g, 0, j)),
        out_shape=jax.ShapeDtypeStruct((ngb, M, kd), BF16),
        scratch_shapes=[pltpu.VMEM((kd, hw), BF16)],
        compiler_params=_cparams(("parallel", "parallel")),
        name="s5_chunk_outputs",
    )(u, u, tz, h, w_in, d_slab)

    return y.reshape(ngb, nc_all, B, S5_CHUNK, LANES)


def _glu_kernel(y_ref, w_ref, b_ref, o_ref):
    slabs, chunks = y_ref.shape[:2]
    y = jnp.concatenate([y_ref[s, :, 0].reshape(chunks * S5_CHUNK, LANES) for s in range(slabs)], axis=1)
    z = jnp.dot(jax.nn.gelu(y.astype(F32)).astype(BF16), w_ref[...], preferred_element_type=F32) + b_ref[...]
    n = o_ref.shape[-1]
    o_ref[0] = (z[:, :n] * jax.nn.sigmoid(z[:, n:])).astype(o_ref.dtype)


def _glu(y, first_chunk, n_tokens, w, b, *, tm):
    slabs, _, B = y.shape[:3]
    W = slabs * LANES
    tm = min(tm, n_tokens)
    tc = tm // S5_CHUNK
    assert first_chunk % tc == 0
    off = first_chunk // tc
    return pl.pallas_call(
        _glu_kernel,
        grid=(B, n_tokens // tm),
        in_specs=[
            pl.BlockSpec((slabs, tc, 1, S5_CHUNK, LANES), lambda g, i: (0, off + i, g, 0, 0)),
            pl.BlockSpec((W, 2 * W), lambda g, i: (0, 0)),
            pl.BlockSpec((1, 2 * W), lambda g, i: (0, 0)),
        ],
        out_specs=pl.BlockSpec((1, tm, W), lambda g, i: (g, i, 0)),
        out_shape=jax.ShapeDtypeStruct((B, n_tokens, W), BF16),
        compiler_params=_cparams(("parallel", "parallel")),
        name="s5_glu",
    )(y, w, b.reshape(1, 2 * W))


def _out_res_kernel(*refs, widths):
    n = len(widths)
    lhs = refs[:n]
    w_ref, x_ref, gate_ref, g_ref, o_ref = refs[n:]
    y = None
    off = 0
    for r, k in zip(lhs, widths):
        part = jnp.dot(r[0], w_ref[off:off + k, :], preferred_element_type=F32)
        y = part if y is None else y + part
        off += k
    gain = gate_ref[0] * g_ref[...]
    for rows in _row_blocks(y.shape[0]):
        o_ref[0, rows] = x_ref[0, rows] + _rms(y[rows]) * gain


def _out_res(parts, w, x, gate, g, *, tm):
    G, T, D = x.shape
    tm = min(tm, T)
    widths = tuple(p.shape[-1] for p in parts)
    K = sum(widths)
    in_specs = [pl.BlockSpec((1, tm, k), lambda b, i: (b, i, 0)) for k in widths]
    in_specs += [
        pl.BlockSpec((K, D), lambda b, i: (0, 0)),
        pl.BlockSpec((1, tm, D), lambda b, i: (b, i, 0)),
        pl.BlockSpec((1, 1, D), _vec_map(gate, 2)),
        pl.BlockSpec((1, D), lambda b, i: (0, 0)),
    ]
    return pl.pallas_call(
        functools.partial(_out_res_kernel, widths=widths),
        grid=(G, T // tm),
        in_specs=in_specs,
        out_specs=pl.BlockSpec((1, tm, D), lambda b, i: (b, i, 0)),
        out_shape=jax.ShapeDtypeStruct((G, T, D), F32),
        compiler_params=_cparams(("parallel", "parallel")),
        name="out_proj_residual",
    )(*parts, w, x, gate, g.reshape(1, D))


HALO = BF16_ROWS


def _ffn_kernel(x_ref, xp_ref, xn_ref, g2_ref, sh_ref, sc_ref, wa_ref, wv_ref, cwa_ref, cwv_ref,
                cba_ref, cbv_ref, wd_ref, gate_ref, g3_ref, o_ref, h_ref, acc_ref, ua_ref, uv_ref, *, tm):
    i = pl.program_id(1)
    f = pl.program_id(2)

    def normmod(x):
        return _rms(x) * (g2_ref[...] * (1.0 + sc_ref[0])) + sh_ref[0]

    @pl.when(f == 0)
    def _():
        d = x_ref.shape[-1]
        hp = jnp.where(i == 0, 0.0, normmod(xp_ref[0]))[SUBLANES - 1:SUBLANES]
        hn = jnp.where(i == pl.num_programs(1) - 1, 0.0, normmod(xn_ref[0]))[0:1]
        row = lax.broadcasted_iota(jnp.int32, (HALO, d), 0)
        halo = jnp.where(row == 0, hn, jnp.where(row == HALO - 1, hp, 0.0))
        for rows in _row_blocks(tm):
            h_ref[rows] = normmod(x_ref[0, rows]).astype(BF16)
        h_ref[tm:] = halo.astype(BF16)
        acc_ref[...] = jnp.zeros_like(acc_ref)

    h = h_ref[...]
    ua_ref[...] = jnp.dot(h, wa_ref[...], preferred_element_type=F32)
    uv_ref[...] = jnp.dot(h, wv_ref[...], preferred_element_type=F32)

    def conv(u_ref, cw_ref, cb_ref):
        prev = pltpu.roll(u_ref[...], 1, 0)[0:tm]
        return (prev * cw_ref[0:1, :] + u_ref[0:tm] * cw_ref[1:2, :]
                + u_ref[1:tm + 1] * cw_ref[2:3, :] + cb_ref[...])

    a = conv(ua_ref, cwa_ref, cba_ref)
    v = conv(uv_ref, cwv_ref, cbv_ref)
    gated = (a * jax.nn.sigmoid(a) * v).astype(BF16)
    acc_ref[...] += jnp.dot(gated, wd_ref[...], preferred_element_type=F32)

    @pl.when(f == pl.num_programs(2) - 1)
    def _():
        gain = gate_ref[0] * g3_ref[...]
        for rows in _row_blocks(tm):
            o_ref[0, rows] = x_ref[0, rows] + _rms(acc_ref[rows]) * gain


def _conv_ffn(x, g2, shift, scale, w_up, conv_w, conv_b, w_down, gate, g3, *, layer, tm, tf):
    G, T, D = x.shape
    F = w_down.shape[1]
    tm = min(tm, T)
    nf = F // tf
    n_i = T // tm
    rb = tm // SUBLANES
    last_rb = T // SUBLANES - 1
    conv_b = conv_b.reshape(1, 2 * F)
    return pl.pallas_call(
        functools.partial(_ffn_kernel, tm=tm),
        grid=(G, n_i, nf),
        in_specs=[
            pl.BlockSpec((1, tm, D), lambda b, i, f: (b, i, 0)),
            pl.BlockSpec((1, SUBLANES, D), lambda b, i, f: (b, jnp.maximum(i * rb - 1, 0), 0)),
            pl.BlockSpec((1, SUBLANES, D), lambda b, i, f: (b, jnp.minimum((i + 1) * rb, last_rb), 0)),
            pl.BlockSpec((1, D), lambda b, i, f: (0, 0)),
            pl.BlockSpec((1, 1, D), _vec_map(shift, 3)),
            pl.BlockSpec((1, 1, D), _vec_map(scale, 3)),
            pl.BlockSpec((None, D, tf), lambda b, i, f: (layer, 0, f)),
            pl.BlockSpec((None, D, tf), lambda b, i, f: (layer, 0, nf + f)),
            pl.BlockSpec((3, tf), lambda b, i, f: (0, f)),
            pl.BlockSpec((3, tf), lambda b, i, f: (0, nf + f)),
            pl.BlockSpec((1, tf), lambda b, i, f: (0, f)),
            pl.BlockSpec((1, tf), lambda b, i, f: (0, nf + f)),
            pl.BlockSpec((None, tf, D), lambda b, i, f: (layer, f, 0)),
            pl.BlockSpec((1, 1, D), _vec_map(gate, 3)),
            pl.BlockSpec((1, D), lambda b, i, f: (0, 0)),
        ],
        out_specs=pl.BlockSpec((1, tm, D), lambda b, i, f: (b, i, 0)),
        out_shape=jax.ShapeDtypeStruct((G, T, D), F32),
        scratch_shapes=[
            pltpu.VMEM((tm + HALO, D), BF16),
            pltpu.VMEM((tm, D), F32),
            pltpu.VMEM((tm + HALO, tf), F32),
            pltpu.VMEM((tm + HALO, tf), F32),
        ],
        compiler_params=_cparams(("parallel", "parallel", "arbitrary")),
        name="conv_ffn",
    )(x, x, x, g2.reshape(1, D), shift, scale, w_up, w_up, conv_w, conv_w, conv_b, conv_b, w_down,
      gate, g3.reshape(1, D))


GLA_BLOCKS = HGRN_CHUNK // SUBLANES


def _gla_operands(z, v, q, lb, rev):
    nb, rb = GLA_BLOCKS, SUBLANES
    order = list(range(nb))[::-1] if rev else list(range(nb))
    scan_of = {b: j for j, b in enumerate(order)}
    pos = lax.broadcasted_iota(jnp.int32, (rb, HGRN_DK), 0)
    if rev:
        pos = rb - 1 - pos

    def prev(x, k):
        return pltpu.roll(x, (rb - k) if rev else k, 0)

    def nxt(x, k):
        return pltpu.roll(x, k if rev else (rb - k), 0)

    def blocks(x):
        return [x[rb * b:rb * (b + 1)] for b in range(nb)]

    def rows(bl):
        return jnp.concatenate(bl, axis=0)

    f = lb + (1.0 - lb) * jax.nn.sigmoid(z)
    fb = blocks(f)
    kb = blocks(1.0 - f)

    hb = rb // 2
    hpos = pos & (hb - 1)
    lo_last = hb if rev else hb - 1
    hi_first = hb - 1 if rev else hb
    p4, s4, p8, s8, bt = [], [], [], [], []
    for b in range(nb):
        x = y = fb[b]
        for k in (1, 2):
            x = x * jnp.where(hpos >= k, prev(x, k), 1.0)
            y = y * jnp.where(hpos <= hb - 1 - k, nxt(y, k), 1.0)
        lo_tot = x[lo_last:lo_last + 1, :]
        hi_tot = y[hi_first:hi_first + 1, :]
        sfx = jnp.where(hpos <= hb - 2, nxt(y, 1), 1.0)
        p4.append(x)
        s4.append(sfx)
        p8.append(x * jnp.where(pos >= hb, lo_tot, 1.0))
        s8.append(sfx * jnp.where(pos < hb, hi_tot, 1.0))
        bt.append(lo_tot * hi_tot)
    bts = [bt[order[j]] for j in range(nb)]
    ones = jnp.ones_like(bts[0])
    before = [ones]
    for j in range(1, nb):
        before.append(before[j - 1] * bts[j - 1])
    after = [ones] * nb
    for j in range(nb - 2, -1, -1):
        after[j] = after[j + 1] * bts[j + 1]
    ftot = before[nb - 1] * bts[nb - 1]

    kbase = [kb[b] * s8[b] for b in range(nb)]
    ops = {"v": v, "ftot": ftot,
           "kdec": rows([kbase[b] * after[scan_of[b]] for b in range(nb)]).astype(BF16)}
    if q is None:
        return ops

    qb = blocks(q)
    q8 = [qb[b] * p8[b] for b in range(nb)]
    ops["q_state"] = rows([q8[b] * before[scan_of[b]] for b in range(nb)]).astype(BF16)
    ops["q_far"] = rows(q8).astype(BF16)
    ops["q_near"] = q.astype(BF16)

    vb = blocks(v.astype(F32))
    kcols, vcols = [], []
    chain = {}
    for d in range(1, nb):
        for j in range(nb - d):
            chain[j] = kbase[order[j]] if d == 1 else chain[j] * bts[j + d - 1]
            kcols.append(chain[j])
            vcols.append(vb[order[j]])
    ops["k_far"] = rows(kcols).astype(BF16)
    ops["v_far"] = rows(vcols).astype(BF16)

    ops["q_cross"] = rows([qb[b] * p4[b] for b in range(nb)]).astype(BF16)
    ops["k_cross"] = rows([kb[b] * s4[b] for b in range(nb)]).astype(BF16)

    kvar = []
    cur = kb
    for d in range(hb):
        if d:
            cur = [cur[b] * nxt(fb[b], d) for b in range(nb)]
        kvar.append(rows(cur).astype(BF16))
    ops["k_near"] = jnp.concatenate(kvar, axis=0)
    return ops


def _gla_first_dots(ops, st):
    nt = (((1,), (1,)), ((), ()))
    st_new = st * ops["ftot"] + lax.dot_general(ops["v"], ops["kdec"], (((0,), (0,)), ((), ())),
                                                preferred_element_type=F32)
    if "q_near" not in ops:
        return st_new, None
    o = lax.dot_general(ops["q_state"], st.astype(BF16), nt, preferred_element_type=F32)
    far = lax.dot_general(ops["q_far"], ops["k_far"], nt, preferred_element_type=F32)
    cross = lax.dot_general(ops["q_cross"], ops["k_cross"], nt, preferred_element_type=F32)
    near = lax.dot_general(ops["q_near"], ops["k_near"], nt, preferred_element_type=F32)
    return st_new, (o, far, cross, near)


def _gla_second_dots(ops, scores, masks):
    o, far, cross, near = scores
    m_far, m_cross, m_near = masks
    o = o + jnp.dot((far * m_far).astype(BF16), ops["v_far"], preferred_element_type=F32)
    o = o + jnp.dot((cross * m_cross).astype(BF16), ops["v"], preferred_element_type=F32)
    v_near = jnp.concatenate([ops["v"]] * (near.shape[1] // HGRN_CHUNK), axis=0)
    return o + jnp.dot((near * m_near).astype(BF16), v_near, preferred_element_type=F32)


def _operand_rows():
    T, nb, rb = HGRN_CHUNK, GLA_BLOCKS, SUBLANES
    far = rb * nb * (nb - 1) // 2
    sizes = [("kdec", T), ("q_state", T), ("q_far", T), ("q_near", T), ("q_cross", T), ("k_cross", T),
             ("v", T), ("k_far", far), ("v_far", far), ("k_near", (rb // 2) * T)]
    rows, off = {}, 0
    for name, n in sizes:
        rows[name] = (off, n)
        off += n
    return rows, off


GLA_OPERAND_ROWS, GLA_OPERAND_TOTAL = _operand_rows()


def _gla_mask_tables():
    T, nb, rb = HGRN_CHUNK, GLA_BLOCKS, SUBLANES
    hb = rb // 2
    t = np.arange(T)
    far, cross, near = [], [], []
    for rev in (False, True):
        sblk = (nb - 1 - t // rb) if rev else t // rb
        spos = (rb - 1 - t % rb) if rev else t % rb
        same_block = t[:, None] // rb == t[None, :] // rb
        cols = [(d, j) for d in range(1, nb) for j in range(nb - d)]
        m_far = np.zeros((T, len(cols) * rb), np.float32)
        for c, (d, j) in enumerate(cols):
            m_far[sblk == j + d, c * rb:(c + 1) * rb] = 1.0
        m_cross = same_block & (spos[:, None] >= hb) & (spos[None, :] < hb)
        same_half = same_block & (spos[:, None] // hb == spos[None, :] // hb)
        m_near = np.zeros((T, hb * T), np.float32)
        for d in range(hb):
            m_near[:, d * T:(d + 1) * T] = same_half & (spos[:, None] - spos[None, :] == d)
        far.append(m_far)
        cross.append(m_cross.astype(np.float32))
        near.append(m_near)
    return tuple(jnp.asarray(np.stack(m)) for m in (far, cross, near))


def _gla_kernel(q_ref, zf_ref, zb_ref, v_ref, g_ref, czf_ref, czb_ref, cv_ref, lb_ref, ng_ref,
                mfar_ref, mcross_ref, mnear_ref, o_ref, oacc_ref, ops_a_ref, ft_a_ref, ops_b_ref, ft_b_ref,
                *, n_lat, n_ctx, hp):
    T, dk = HGRN_CHUNK, HGRN_DK
    jobs = [(h, rev) for h in range(hp) for rev in (False, True)]
    zero_state = jnp.zeros((dk, dk), F32)

    def cols(h):
        return slice(h * dk, (h + 1) * dk)

    def start(rev, i, n):
        r = ((n - 1 - i) if rev else i) * T
        return r if isinstance(r, int) else pl.multiple_of(r, T)

    def lb_row(h, rev):
        return lb_ref[int(rev):int(rev) + 1, cols(h)]

    def prep_ctx(i):
        ops = []
        for h, rev in jobs:
            r = start(rev, i, n_ctx)
            z = (czb_ref if rev else czf_ref)[0, pl.ds(r, T), cols(h)].astype(F32)
            ops.append(_gla_operands(z, cv_ref[0, pl.ds(r, T), cols(h)], None, lb_row(h, rev), rev))
        return tuple(ops)

    def prep_lat(i):
        ops = []
        for h, rev in jobs:
            r = start(rev, i, n_lat)
            qz = q_ref[0, pl.ds(r, T), cols(h)].astype(F32)
            z = (zb_ref if rev else zf_ref)[0, pl.ds(r, T), cols(h)].astype(F32)
            ops.append(_gla_operands(z, v_ref[0, pl.ds(r, T), cols(h)], qz * jax.nn.sigmoid(qz),
                                     lb_row(h, rev), rev))
        return tuple(ops)

    def stash(ops, buf):
        ops_ref, ft_ref = buf
        for j, o in enumerate(ops):
            for name, (off, n) in GLA_OPERAND_ROWS.items():
                ops_ref[j, off:off + n, :] = o[name]
            ft_ref[j] = jnp.broadcast_to(o["ftot"], (SUBLANES, dk))

    def fetch(buf):
        ops_ref, ft_ref = buf
        out = []
        for j in range(len(jobs)):
            o = {name: ops_ref[j, off:off + n, :] for name, (off, n) in GLA_OPERAND_ROWS.items()}
            o["ftot"] = ft_ref[j, 0:1, :]
            out.append(o)
        return out

    def ctx_step(i, sts):
        return tuple(_gla_first_dots(o, st)[0] for o, st in zip(prep_ctx(i), sts))

    def lat_step(i, sts, second_half, cur, nxt):
        ops = fetch(cur)
        firsts = [_gla_first_dots(o, st) for o, st in zip(ops, sts)]
        stash(prep_lat(jnp.minimum(i + 1, n_lat - 1)), nxt)
        outs = [_gla_second_dots(o, f[1], tuple(m[int(rev)] for m in (mfar_ref, mcross_ref, mnear_ref)))
                for o, f, (_, rev) in zip(ops, firsts, jobs)]
        for (h, rev), o in zip(jobs, outs):
            r = start(rev, i, n_lat)
            if second_half:
                o = o + oacc_ref[pl.ds(r, T), cols(h)]
                gz = g_ref[0, pl.ds(r, T), cols(h)].astype(F32)
                o = _rms(o) * ng_ref[...] * (gz * jax.nn.sigmoid(gz))
                o_ref[0, pl.ds(r, T), cols(h)] = o.astype(o_ref.dtype)
            else:
                oacc_ref[pl.ds(r, T), cols(h)] = o
        return tuple(f[0] for f in firsts)

    buf_a, buf_b = (ops_a_ref, ft_a_ref), (ops_b_ref, ft_b_ref)

    def lat_pair(k, sts, second_half):
        sts = lat_step(2 * k, sts, second_half, buf_a, buf_b)
        return lat_step(2 * k + 1, sts, second_half, buf_b, buf_a)

    sts = lax.fori_loop(0, n_ctx, ctx_step, (zero_state,) * len(jobs))
    stash(prep_lat(0), buf_a)
    quarter = n_lat // 4
    sts = lax.fori_loop(0, quarter, functools.partial(lat_pair, second_half=False), sts)
    lax.fori_loop(quarter, 2 * quarter, functools.partial(lat_pair, second_half=True), sts)


GLA_HEADS_PER_STEP = 2


def _gla(p_lat, p_ctx, lb, norm_g, n_heads):
    B, L, _ = p_lat.shape
    Lc = p_ctx.shape[1]
    hp = GLA_HEADS_PER_STEP
    steps = n_heads // hp
    dv = HGRN_DK
    w = hp * dv

    def col(group):
        return lambda b, h: (b, 0, group * steps + h)

    lat_specs = [pl.BlockSpec((1, L, w), col(gidx)) for gidx in range(5)]
    ctx_specs = [pl.BlockSpec((1, Lc, w), col(gidx)) for gidx in (1, 2, 3)]
    masks = _gla_mask_tables()
    n_lat = L // HGRN_CHUNK
    assert n_lat % 4 == 0 and n_heads % hp == 0
    operand_bufs = [pltpu.VMEM((2 * hp, GLA_OPERAND_TOTAL, dv), BF16), pltpu.VMEM((2 * hp, SUBLANES, dv), F32)]
    return pl.pallas_call(
        functools.partial(_gla_kernel, n_lat=n_lat, n_ctx=Lc // HGRN_CHUNK, hp=hp),
        grid=(B, steps),
        in_specs=lat_specs + ctx_specs + [
            pl.BlockSpec((2, w), lambda b, h: (0, h)),
            pl.BlockSpec((1, dv), lambda b, h: (0, 0)),
        ] + [pl.BlockSpec(m.shape, lambda b, h: (0, 0, 0)) for m in masks],
        out_specs=pl.BlockSpec((1, L, w), lambda b, h: (b, 0, h)),
        out_shape=jax.ShapeDtypeStruct((B, L, n_heads * dv), BF16),
        scratch_shapes=[pltpu.VMEM((L, w), F32)] + operand_bufs * 2,
        compiler_params=_cparams(("parallel", "parallel")),
        name="hgrn2_bidirectional",
    )(*([p_lat] * 5), *([p_ctx] * 3), lb, norm_g.reshape(1, dv), *masks)


def _rope_tables(n_tokens):
    rows = n_tokens // GRID_W
    row = jnp.repeat(jnp.arange(rows, dtype=F32), GRID_W)
    colp = jnp.tile(jnp.arange(GRID_W, dtype=F32), rows)
    inv = ROPE_THETA ** (-jnp.arange(AXIS_FREQS, dtype=F32) / AXIS_FREQS)
    ang = jnp.stack([row[:, None] * inv, colp[:, None] * inv], axis=1)
    cos, sin = jnp.cos(ang), jnp.sin(ang)
    zero = jnp.zeros_like(sin)
    full = lambda a, b: jnp.stack([a, b], axis=2).reshape(n_tokens, HEAD_DIM)
    return full(cos, cos), full(-sin, zero), full(zero, sin)


def _identity_rope(n_tokens):
    return (jnp.ones((n_tokens, HEAD_DIM), F32), jnp.zeros((n_tokens, HEAD_DIM), F32),
            jnp.zeros((n_tokens, HEAD_DIM), F32))


def kernel(x, c, ctx, c_ctx, mod_w, mod_b, norm_g, ab_w_in, ab_w_out, attn_q_norm, attn_k_norm,
           s5_lam_re, s5_lam_im, s5_log_dt, s5_b_re, s5_b_im, s5_c_re, s5_c_im, s5_d,
           s5_glu_w, s5_glu_b, c_w_in, c_w_out, hgrn_lb_logits, hgrn_norm,
           ffn_w_up, ffn_conv_w, ffn_conv_b, ffn_w_down):
    B, L, D = x.shape
    Lc = ctx.shape[1]
    depth = mod_w.shape[0]
    attn_w = ATTN_HEADS * HEAD_DIM
    kv_w = ATTN_KV_HEADS * HEAD_DIM
    qk_w = attn_w + kv_w

    lb_all = jnp.cumsum(jax.nn.softmax(hgrn_lb_logits.astype(F32), axis=0), axis=0)
    lb_all = lb_all - lb_all[:1]

    cvec = jnp.concatenate([c, c_ctx[None], jnp.zeros((SUBLANES - (B + 1) % SUBLANES, D), F32)], axis=0)
    mods = _modulation(cvec, mod_w, mod_b)

    rope_lat = _rope_tables(L)
    rope_ctx = _identity_rope(Lc)
    w_up = ffn_w_up.astype(BF16)
    w_down = ffn_w_down.astype(BF16)

    for l in range(depth):
        last = l == depth - 1
        m_lat = [mods[l, :B, k * D:(k + 1) * D][:, None, :] for k in range(6)]
        m_ctx = [mods[l, B:B + 1, k * D:(k + 1) * D][:, None, :] for k in range(6)]
        g = norm_g[l]
        if l % 2 == 0:
            e = l // 2
            w_in = ab_w_in[e].astype(BF16)
            gain = jnp.concatenate([jnp.tile(attn_q_norm[e] * (HEAD_DIM ** -0.5), ATTN_HEADS),
                                    jnp.tile(attn_k_norm[e], ATTN_KV_HEADS),
                                    jnp.ones((kv_w,), F32)]).reshape(1, qk_w + kv_w)
            p_lat, u_lat = _proj_ab(x, g[0], m_lat[0], m_lat[1], w_in, gain, rope_lat, tm=1024)
            p_ctx, u_ctx = _proj_ab(ctx, g[0], m_ctx[0], m_ctx[1], w_in, gain, rope_ctx, tm=1024)
            a_lat = _attention(p_lat, p_ctx, tq=256)
            tables = _s5_tables(s5_lam_re[e], s5_lam_im[e], s5_log_dt[e], s5_b_re[e], s5_b_im[e],
                                s5_c_re[e], s5_c_im[e], s5_d[e])
            y = _s5(u_ctx, u_lat, tables)
            glu_w = s5_glu_w[e].astype(BF16)
            w_out = ab_w_out[e].astype(BF16)
            s_lat = _glu(y, Lc // S5_CHUNK, L, glu_w, s5_glu_b[e], tm=256)
            x = _out_res([a_lat, s_lat], w_out, x, m_lat[2], g[1], tm=512)
            if not last:
                a_ctx = _attention(p_ctx, None, tq=128)
                s_ctx = _glu(y, 0, Lc, glu_w, s5_glu_b[e], tm=256)
                ctx = _out_res([a_ctx, s_ctx], w_out, ctx, m_ctx[2], g[1], tm=512)
        else:
            o_idx = l // 2
            w_in = c_w_in[o_idx].astype(BF16)
            n_heads = c_w_out.shape[1] // HGRN_DK
            p_lat = _proj(x, g[0], m_lat[0], m_lat[1], w_in, tm=1024, tn=2048)
            p_ctx = _proj(ctx, g[0], m_ctx[0], m_ctx[1], w_in, tm=1024, tn=2048)
            o_lat = _gla(p_lat, p_ctx, lb_all[l], hgrn_norm[o_idx], n_heads)
            x = _out_res([o_lat], c_w_out[o_idx].astype(BF16), x, m_lat[2], g[1], tm=512)
            assert last, "context outputs of the HGRN2 mixer are only needed by a following layer"
        x = _conv_ffn(x, g[2], m_lat[3], m_lat[4], w_up, ffn_conv_w[l], ffn_conv_b[l], w_down,
                      m_lat[5], g[3], layer=l, tm=512, tf=512)
        if not last:
            ctx = _conv_ffn(ctx, g[2], m_ctx[3], m_ctx[4], w_up, ffn_conv_w[l], ffn_conv_b[l], w_down,
                            m_ctx[5], g[3], layer=l, tm=512, tf=512)
    return x
```

```python
import functools
import math

import jax
import jax.numpy as jnp
import numpy as np
from jax import lax
from jax.experimental import pallas as pl
from jax.experimental.pallas import tpu as pltpu

F32 = jnp.float32
BF16 = jnp.bfloat16
EPS = 1e-6

LANES = 128
SUBLANES = 8
BF16_ROWS = 16

HEAD_DIM = 128
ATTN_HEADS = 8
ATTN_KV_HEADS = 2
ATTN_GROUP = ATTN_HEADS // ATTN_KV_HEADS
GRID_W = 64
ROPE_THETA = 10000.0
AXIS_FREQS = HEAD_DIM // 4

S5_GROUP = 16
S5_STATE = 64
S5_CHUNK = 16

HGRN_DK = 128
HGRN_CHUNK = 64

VMEM_LIMIT = 56 * 1024 * 1024


def _cparams(sem):
    return pltpu.CompilerParams(dimension_semantics=sem, vmem_limit_bytes=VMEM_LIMIT)


def _vec_map(vec, nd_grid):
    shared = vec.shape[0] == 1
    if nd_grid == 2:
        return lambda g, i: (0 if shared else g, 0, 0)
    return lambda g, i, j: (0 if shared else g, 0, 0)


def _rms(x):
    return x * lax.rsqrt(jnp.mean(x * x, axis=-1, keepdims=True) + EPS)


def _mod_kernel(s_ref, w_ref, b_ref, o_ref):
    s = s_ref[...]
    a = s * jax.nn.sigmoid(s)
    o_ref[0] = jnp.dot(a.astype(BF16), w_ref[0].astype(BF16), preferred_element_type=F32) + b_ref[0]


def _modulation(cvec, mod_w, mod_b):
    depth, d, n = mod_w.shape
    rows = cvec.shape[0]
    tn = 1024
    return pl.pallas_call(
        _mod_kernel,
        grid=(depth, n // tn),
        in_specs=[
            pl.BlockSpec((rows, d), lambda l, j: (0, 0)),
            pl.BlockSpec((1, d, tn), lambda l, j: (l, 0, j)),
            pl.BlockSpec((1, 1, tn), lambda l, j: (l, 0, j)),
        ],
        out_specs=pl.BlockSpec((1, rows, tn), lambda l, j: (l, 0, j)),
        out_shape=jax.ShapeDtypeStruct((depth, rows, n), F32),
        compiler_params=_cparams(("parallel", "parallel")),
        name="modulation",
    )(cvec, mod_w, mod_b.reshape(depth, 1, n))


NORM_ROWS = 32


def _row_blocks(n_rows):
    step = NORM_ROWS if n_rows % NORM_ROWS == 0 else n_rows
    return [slice(r, r + step) for r in range(0, n_rows, step)]


def _norm_mod_to(h_ref, x_ref, g_ref, sh_ref, sc_ref):
    gain = g_ref[...] * (1.0 + sc_ref[0])
    for rows in _row_blocks(h_ref.shape[0]):
        h_ref[rows] = (_rms(x_ref[0, rows]) * gain + sh_ref[0]).astype(BF16)


def _proj_kernel(x_ref, g_ref, sh_ref, sc_ref, w_ref, o_ref, h_ref):
    @pl.when(pl.program_id(2) == 0)
    def _():
        _norm_mod_to(h_ref, x_ref, g_ref, sh_ref, sc_ref)

    o_ref[0] = jnp.dot(h_ref[...], w_ref[...], preferred_element_type=F32).astype(o_ref.dtype)


def _proj_specs(x, g, shift, scale, w, tm, tn):
    G, T, D = x.shape
    in_specs = [
        pl.BlockSpec((1, tm, D), lambda b, i, j: (b, i, 0)),
        pl.BlockSpec((1, D), lambda b, i, j: (0, 0)),
        pl.BlockSpec((1, 1, D), _vec_map(shift, 3)),
        pl.BlockSpec((1, 1, D), _vec_map(scale, 3)),
        pl.BlockSpec((D, tn), lambda b, i, j: (0, j)),
    ]
    return in_specs, [x, g.reshape(1, D), shift, scale, w]


def _proj(x, g, shift, scale, w, *, tm, tn, out_dtype=BF16):
    G, T, D = x.shape
    N = w.shape[1]
    tm = min(tm, T)
    in_specs, args = _proj_specs(x, g, shift, scale, w, tm, tn)
    return pl.pallas_call(
        _proj_kernel,
        grid=(G, T // tm, N // tn),
        in_specs=in_specs,
        out_specs=pl.BlockSpec((1, tm, tn), lambda b, i, j: (b, i, j)),
        out_shape=jax.ShapeDtypeStruct((G, T, N), out_dtype),
        scratch_shapes=[pltpu.VMEM((tm, D), BF16)],
        compiler_params=_cparams(("parallel", "parallel", "arbitrary")),
        name="norm_mod_proj",
    )(*args)


AB_TILE_HEADS = 4
AB_QKV_TILES = (ATTN_HEADS + 2 * ATTN_KV_HEADS) // AB_TILE_HEADS


def _proj_ab_kernel(x_ref, g_ref, sh_ref, sc_ref, w_ref, gain_ref, cos_ref, s1_ref, s2_ref,
                    qkv_ref, u_ref, h_ref):
    j = pl.program_id(2)

    @pl.when(j == 0)
    def _():
        _norm_mod_to(h_ref, x_ref, g_ref, sh_ref, sc_ref)

    acc = jnp.dot(h_ref[...], w_ref[...], preferred_element_type=F32)
    k_tile = ATTN_HEADS // AB_TILE_HEADS

    def head(hh):
        return slice(hh * HEAD_DIM, (hh + 1) * HEAD_DIM)

    def normed_rotated(hh):
        y = _rms(acc[:, head(hh)]) * gain_ref[:, head(hh)]
        return (y * cos_ref[...] + pltpu.roll(y, HEAD_DIM - AXIS_FREQS, 1) * s1_ref[...]
                + pltpu.roll(y, AXIS_FREQS, 1) * s2_ref[...])

    @pl.when(j < k_tile)
    def _():
        for hh in range(AB_TILE_HEADS):
            qkv_ref[0, :, head(hh)] = normed_rotated(hh).astype(qkv_ref.dtype)

    @pl.when(j == k_tile)
    def _():
        for hh in range(AB_TILE_HEADS):
            y = normed_rotated(hh) if hh < ATTN_KV_HEADS else acc[:, head(hh)]
            qkv_ref[0, :, head(hh)] = y.astype(qkv_ref.dtype)

    @pl.when(j > k_tile)
    def _():
        chunks = u_ref.shape[1]
        for s in range(AB_TILE_HEADS):
            u_ref[s, :, 0] = acc[:, head(s)].astype(u_ref.dtype).reshape(chunks, S5_CHUNK, LANES)


def _proj_ab(x, g, shift, scale, w, gain, rope, *, tm):
    G, T, D = x.shape
    tm = min(tm, T)
    tn = AB_TILE_HEADS * HEAD_DIM
    n_tiles = w.shape[1] // tn
    slabs = (n_tiles - AB_QKV_TILES) * AB_TILE_HEADS
    in_specs, args = _proj_specs(x, g, shift, scale, w, tm, tn)
    last_qkv = AB_QKV_TILES - 1
    in_specs += [pl.BlockSpec((1, tn), lambda b, i, j: (0, jnp.minimum(j, last_qkv)))]
    in_specs += [pl.BlockSpec((tm, HEAD_DIM), lambda b, i, j: (i, 0))] * 3
    return pl.pallas_call(
        _proj_ab_kernel,
        grid=(G, T // tm, n_tiles),
        in_specs=in_specs,
        out_specs=[
            pl.BlockSpec((1, tm, tn), lambda b, i, j: (b, i, jnp.minimum(j, last_qkv))),
            pl.BlockSpec((AB_TILE_HEADS, tm // S5_CHUNK, 1, S5_CHUNK, LANES),
                         lambda b, i, j: (jnp.maximum(j - AB_QKV_TILES, 0), i, b, 0, 0)),
        ],
        out_shape=[
            jax.ShapeDtypeStruct((G, T, AB_QKV_TILES * tn), BF16),
            jax.ShapeDtypeStruct((slabs, T // S5_CHUNK, G, S5_CHUNK, LANES), BF16),
        ],
        scratch_shapes=[pltpu.VMEM((tm, D), BF16)],
        compiler_params=_cparams(("parallel", "parallel", "arbitrary")),
        name="norm_mod_proj_ab",
    )(*args, gain, *rope)


ATTN_KEY_CHUNK = 512


def _attn_kernel(q_ref, k_ref, v_ref, *rest, tq, has_prefix):
    if has_prefix:
        kp_ref, vp_ref, o_ref = rest
    else:
        (o_ref,) = rest
    nt = (((1,), (1,)), ((), ()))
    q = jnp.concatenate(
        [q_ref[0, :, g * HEAD_DIM:(g + 1) * HEAD_DIM] for g in range(ATTN_GROUP)], axis=0)
    n_keys = k_ref.shape[1]
    kc = min(ATTN_KEY_CHUNK, n_keys)
    chunks = [(kp_ref, vp_ref, 0, kp_ref.shape[1])] if has_prefix else []
    chunks += [(k_ref, v_ref, c * kc, kc) for c in range(n_keys // kc)]

    def scores(chunk):
        kr, _, start, size = chunk
        return lax.dot_general(q, kr[0, start:start + size, :], nt, preferred_element_type=F32)

    m = jnp.full((q.shape[0], 1), -1e30, F32)
    l = jnp.zeros((q.shape[0], 1), F32)
    o = jnp.zeros((q.shape[0], HEAD_DIM), F32)
    s_next = scores(chunks[0])
    for ci, (_, vr, start, size) in enumerate(chunks):
        s = s_next
        if ci + 1 < len(chunks):
            s_next = scores(chunks[ci + 1])
        m_new = jnp.maximum(m, jnp.max(s, axis=-1, keepdims=True))
        alpha = jnp.exp2(m - m_new)
        p = jnp.exp2(s - m_new)
        l = alpha * l + jnp.sum(p, axis=-1, keepdims=True)
        o = alpha * o + jnp.dot(p.astype(BF16), vr[0, start:start + size, :], preferred_element_type=F32)
        m = m_new
    o = o * (1.0 / l)
    for g in range(ATTN_GROUP):
        o_ref[0, :, g * HEAD_DIM:(g + 1) * HEAD_DIM] = o[g * tq:(g + 1) * tq].astype(o_ref.dtype)


def _attention(qkvu, prefix, *, tq):
    B, L, _ = qkvu.shape
    tq = min(tq, L)
    gw = ATTN_GROUP * HEAD_DIM
    k_blk = ATTN_HEADS
    v_blk = ATTN_HEADS + ATTN_KV_HEADS
    in_specs = [
        pl.BlockSpec((1, tq, gw), lambda b, h, i: (b, i, h)),
        pl.BlockSpec((1, L, HEAD_DIM), lambda b, h, i: (b, 0, k_blk + h)),
        pl.BlockSpec((1, L, HEAD_DIM), lambda b, h, i: (b, 0, v_blk + h)),
    ]
    args = [qkvu, qkvu, qkvu]
    if prefix is not None:
        Lp = prefix.shape[1]
        in_specs += [
            pl.BlockSpec((1, Lp, HEAD_DIM), lambda b, h, i: (b, 0, k_blk + h)),
            pl.BlockSpec((1, Lp, HEAD_DIM), lambda b, h, i: (b, 0, v_blk + h)),
        ]
        args += [prefix, prefix]
    return pl.pallas_call(
        functools.partial(_attn_kernel, tq=tq, has_prefix=prefix is not None),
        grid=(B, ATTN_KV_HEADS, L // tq),
        in_specs=in_specs,
        out_specs=pl.BlockSpec((1, tq, gw), lambda b, h, i: (b, i, h)),
        out_shape=jax.ShapeDtypeStruct((B, L, ATTN_HEADS * HEAD_DIM), BF16),
        compiler_params=_cparams(("parallel", "parallel", "arbitrary")),
        name="gqa_attention",
    )(*args)


S5_SLAB_GROUPS = LANES // S5_GROUP
S5_KD = S5_CHUNK * LANES
S5_PAIR_COLS = 4 * LANES


def _s5_end_kernel(u_ref, w_ref, e_ref):
    e_ref[...] = jnp.dot(u_ref[0], w_ref[0], preferred_element_type=F32)


def _s5_scan_kernel(e_ref, cf_ref, cb_ref, h_ref, *, batch, nc_ctx, nc_all):
    cps = SUBLANES // batch
    g_ctx, g_all = nc_ctx // cps, nc_all // cps

    def advance(hr, hi, er, ei, c_ref):
        ar, ai = c_ref[0:1, :], c_ref[1:2, :]
        return ar * hr - ai * hi + er, ar * hi + ai * hr + ei

    def sweep(r, state, col0, c_ref, order):
        e8 = e_ref[pl.ds(r, SUBLANES), col0:col0 + 2 * LANES]
        hr, hi = state
        out_r, out_i = [None] * cps, [None] * cps
        for c in order:
            out_r[c], out_i[c] = hr, hi
            rows = slice(c * batch, (c + 1) * batch)
            hr, hi = advance(hr, hi, e8[rows, :LANES], e8[rows, LANES:], c_ref)
        h_ref[pl.ds(r, SUBLANES), col0:col0 + LANES] = jnp.concatenate(out_r, axis=0).astype(h_ref.dtype)
        h_ref[pl.ds(r, SUBLANES), col0 + LANES:col0 + 2 * LANES] = (
            jnp.concatenate(out_i, axis=0).astype(h_ref.dtype))
        return hr, hi

    def body(i, carry):
        fwd, bwd = carry
        fwd = sweep(pl.multiple_of(i * SUBLANES, SUBLANES), fwd, 0, cf_ref, range(cps))
        gi = jnp.where(i < g_ctx, g_ctx - 1 - i, g_all - 1 - (i - g_ctx))
        bwd = sweep(pl.multiple_of(gi * SUBLANES, SUBLANES), bwd, 2 * LANES, cb_ref, reversed(range(cps)))
        return fwd, bwd

    zero = jnp.zeros((batch, LANES), F32)
    lax.fori_loop(0, g_all, body, ((zero, zero), (zero, zero)))


def _s5_out_kernel(u_ref, us_ref, lag_ref, h_ref, win_ref, d_ref, y_ref, tz_ref):
    T = S5_CHUNK
    t_blocks = tz_ref.shape[1] // LANES
    t0 = pl.program_id(1) * t_blocks
    for tt in range(t_blocks):
        for s in range(T):
            tz_ref[s * LANES:(s + 1) * LANES, tt * LANES:(tt + 1) * LANES] = lag_ref[0, t0 + tt - s + (T - 1)]
    y = jnp.dot(u_ref[0], tz_ref[...], preferred_element_type=F32)
    y += jnp.dot(h_ref[...].astype(BF16), win_ref[0], preferred_element_type=F32)
    y += us_ref[0].astype(F32) * d_ref[0]
    y_ref[0] = y.astype(y_ref.dtype)


def _s5_tables(lam_re, lam_im, log_dt, b_re, b_im, c_re, c_im, d_skip):
    T, C, P = S5_CHUNK, S5_GROUP, S5_STATE
    G = lam_re.shape[1]
    hi = lax.Precision.HIGHEST
    lam = lax.complex(jnp.minimum(lam_re.astype(F32), -1e-4), lam_im.astype(F32))
    dt = jnp.exp(log_dt.astype(F32))[..., None]
    lam_dt = lam * dt
    lam_bar = jnp.exp(lam_dt)
    bmat = lax.complex(b_re.astype(F32), b_im.astype(F32))
    b_bar = ((lam_bar - 1.0) / lam)[..., None] * bmat
    cmat = lax.complex(c_re.astype(F32), c_im.astype(F32))
    steps = jnp.arange(T + 1, dtype=F32)
    pw = jnp.exp(lam_dt[:, None] * steps[None, :, None, None])

    kern = jnp.real(jnp.einsum('zgcp,zjgp,zgpd->zjgcd', cmat, pw[:, :T], b_bar, precision=hi))

    def end_w(p_sel, bb):
        w = p_sel[:, :, None, :] * bb.transpose(0, 2, 1)[None]
        return [jnp.real(w), jnp.imag(w)]

    w_end = jnp.concatenate(end_w(pw[0, :T][::-1], b_bar[0]) + end_w(pw[1, :T], b_bar[1]), axis=-1)

    def in_w(p_sel, cm):
        z = cm.transpose(0, 2, 1)[:, :, None, :] * p_sel.transpose(1, 2, 0)[:, :, :, None]
        return [jnp.real(z), -jnp.imag(z)]

    w_in = jnp.stack(in_w(pw[0, 1:], cmat[0]) + in_w(pw[1, 1:][::-1], cmat[1]), axis=1)

    def coef(z):
        return jnp.stack([jnp.real(z).reshape(G * P), jnp.imag(z).reshape(G * P)])

    sg = S5_SLAB_GROUPS
    ngb, npair = G // sg, sg // 2
    grp = np.arange(sg)
    tok_cols_grp = np.tile(np.repeat(grp, C), T)
    state_cols_grp = (2 * np.arange(npair)[:, None, None, None] + np.arange(2)[None, None, :, None]
                      + np.zeros((1, 4, 1, P), np.int64)).reshape(-1)
    rows_gd = np.repeat(grp, C)

    lag_k = jnp.concatenate([kern[1][1:][::-1], (kern[0][0] + kern[1][0])[None], kern[0][1:]], axis=0)
    lag_src = lag_k.reshape(2 * T - 1, ngb, sg, C, C).transpose(1, 0, 2, 4, 3).reshape(ngb * (2 * T - 1), LANES, C)
    rep_c = np.tile(np.eye(C, dtype=np.float32), (1, sg))
    mask_c = (rows_gd[:, None] == np.repeat(grp, C)[None, :]).astype(np.float32)[None]
    lag_tiles = _spread(lag_src, rep_c, mask_c, lambda n: 0).reshape(ngb, 2 * T - 1, LANES, LANES)

    end_src = w_end.reshape(T, ngb, sg * C, 4 * P).transpose(1, 0, 2, 3).reshape(ngb * T, LANES, 4 * P)
    rep_state = np.tile(np.eye(4 * P, dtype=np.float32).reshape(4 * P, 1, 4, 1, P), (1, npair, 1, 2, 1))
    rep_state = rep_state.reshape(4 * P, npair * S5_PAIR_COLS)
    mask_state = (rows_gd[:, None] == state_cols_grp[None, :]).astype(np.float32)[None]
    end_slab = _spread(end_src, rep_state, mask_state, lambda n: 0).reshape(ngb, S5_KD, npair * S5_PAIR_COLS)

    in_src = w_in.reshape(ngb, npair, 2, 4, P, T * C).transpose(0, 1, 3, 2, 4, 5)
    in_src = in_src.reshape(ngb * npair * 4, 2 * P, T * C)
    rep_tok = np.tile(np.eye(T * C, dtype=np.float32).reshape(T * C, T, 1, C), (1, 1, sg, 1)).reshape(T * C, S5_KD)
    row_grp = 2 * np.arange(npair)[:, None, None] + np.repeat(np.arange(2), P)[None, :, None]
    mask_tok = (row_grp == tok_cols_grp[None, None, :]).astype(np.float32)
    in_slab = _spread(in_src, rep_tok, mask_tok, lambda n: (n // 4) % npair)
    in_slab = in_slab.reshape(ngb, npair * S5_PAIR_COLS, S5_KD)

    d_slab = jnp.tile(d_skip.astype(F32).reshape(ngb, 1, LANES), (1, 1, T))
    return lag_tiles, end_slab, in_slab, coef(pw[0, T]), coef(pw[1, T]), d_slab


SPREAD_TILES = 4


def _spread_kernel(src_ref, rep_ref, mask_ref, o_ref):
    for i in range(SPREAD_TILES):
        o_ref[i] = (jnp.dot(src_ref[i].astype(BF16), rep_ref[...], preferred_element_type=F32)
                    * mask_ref[0]).astype(o_ref.dtype)


def _spread(src, rep, mask, mask_index):
    n, rows, k = src.shape
    width = rep.shape[1]
    st = SPREAD_TILES
    assert n % st == 0
    return pl.pallas_call(
        _spread_kernel,
        grid=(n // st,),
        in_specs=[
            pl.BlockSpec((st, rows, k), lambda i: (i, 0, 0)),
            pl.BlockSpec((k, width), lambda i: (0, 0)),
            pl.BlockSpec((1, rows, width), lambda i: (mask_index(i * st), 0, 0)),
        ],
        out_specs=pl.BlockSpec((st, rows, width), lambda i: (i, 0, 0)),
        out_shape=jax.ShapeDtypeStruct((n, rows, width), BF16),
        compiler_params=_cparams(("parallel",)),
        name="s5_spread_operator",
    )(src, jnp.asarray(rep, BF16), jnp.asarray(mask, F32))


def _s5(u_ctx, u_lat, tables):
    tz, w_end, w_in, cf, cb, d_slab = tables
    ngb, nc_ctx, B = u_ctx.shape[:3]
    nc_all = nc_ctx + u_lat.shape[1]
    cps = SUBLANES // B
    assert SUBLANES % B == 0 and nc_ctx % cps == 0 and (nc_all - nc_ctx) % cps == 0
    M = nc_all * B
    kd, sw = S5_KD, w_end.shape[-1]
    n_half = 4
    hw = kd // n_half
    u = jnp.concatenate([u_ctx, u_lat], axis=1).reshape(ngb, M, kd)

    e = pl.pallas_call(
        _s5_end_kernel,
        grid=(ngb, n_half),
        in_specs=[
            pl.BlockSpec((1, M, kd), lambda g, j: (g, 0, 0)),
            pl.BlockSpec((1, kd, sw // n_half), lambda g, j: (g, 0, j)),
        ],
        out_specs=pl.BlockSpec((M, sw // n_half), lambda g, j: (0, g * n_half + j)),
        out_shape=jax.ShapeDtypeStruct((M, ngb * sw), F32),
        compiler_params=_cparams(("parallel", "parallel")),
        name="s5_chunk_end_states",
    )(u, w_end)

    h = pl.pallas_call(
        functools.partial(_s5_scan_kernel, batch=B, nc_ctx=nc_ctx, nc_all=nc_all),
        grid=(ngb * sw // S5_PAIR_COLS,),
        in_specs=[pl.BlockSpec((M, S5_PAIR_COLS), lambda j: (0, j))]
        + [pl.BlockSpec((2, LANES), lambda j: (0, j))] * 2,
        out_specs=pl.BlockSpec((M, S5_PAIR_COLS), lambda j: (0, j)),
        out_shape=jax.ShapeDtypeStruct((M, ngb * sw), F32),
        compiler_params=_cparams(("parallel",)),
        name="s5_chunk_scan",
    )(e, cf, cb)

    y = pl.pallas_call(
        _s5_out_kernel,
        grid=(ngb, n_half),
        in_specs=[
            pl.BlockSpec((1, M, kd), lambda g, j: (g, 0, 0)),
            pl.BlockSpec((1, M, hw), lambda g, j: (g, 0, j)),
            pl.BlockSpec((1,) + tz.shape[1:], lambda g, j: (g, 0, 0, 0)),
            pl.BlockSpec((M, sw), lambda g, j: (0, g)),
            pl.BlockSpec((1, sw, hw), lambda g, j: (g, 0, j)),
            pl.BlockSpec((1, 1, hw), lambda g, j: (g, 0, j)),
        ],
        out_specs=pl.BlockSpec((1, M, hw), lambda g, j: (g, 0, j)),
        out_shape=jax.ShapeDtypeStruct((ngb, M, kd), BF16),
        scratch_shapes=[pltpu.VMEM((kd, hw), BF16)],
        compiler_params=_cparams(("parallel", "parallel")),
        name="s5_chunk_outputs",
    )(u, u, tz, h, w_in, d_slab)

    return y.reshape(ngb, nc_all, B, S5_CHUNK, LANES)


def _glu_kernel(y_ref, w_ref, b_ref, o_ref):
    slabs, chunks = y_ref.shape[:2]
    y = jnp.concatenate([y_ref[s, :, 0].reshape(chunks * S5_CHUNK, LANES) for s in range(slabs)], axis=1)
    z = jnp.dot(jax.nn.gelu(y.astype(F32)).astype(BF16), w_ref[...], preferred_element_type=F32) + b_ref[...]
    n = o_ref.shape[-1]
    o_ref[0] = (z[:, :n] * jax.nn.sigmoid(z[:, n:])).astype(o_ref.dtype)


def _glu(y, first_chunk, n_tokens, w, b, *, tm):
    slabs, _, B = y.shape[:3]
    W = slabs * LANES
    tm = min(tm, n_tokens)
    tc = tm // S5_CHUNK
    assert first_chunk % tc == 0
    off = first_chunk // tc
    return pl.pallas_call(
        _glu_kernel,
        grid=(B, n_tokens // tm),
        in_specs=[
            pl.BlockSpec((slabs, tc, 1, S5_CHUNK, LANES), lambda g, i: (0, off + i, g, 0, 0)),
            pl.BlockSpec((W, 2 * W), lambda g, i: (0, 0)),
            pl.BlockSpec((1, 2 * W), lambda g, i: (0, 0)),
        ],
        out_specs=pl.BlockSpec((1, tm, W), lambda g, i: (g, i, 0)),
        out_shape=jax.ShapeDtypeStruct((B, n_tokens, W), BF16),
        compiler_params=_cparams(("parallel", "parallel")),
        name="s5_glu",
    )(y, w, b.reshape(1, 2 * W))


def _out_res_kernel(*refs, widths):
    n = len(widths)
    lhs = refs[:n]
    w_ref, x_ref, gate_ref, g_ref, o_ref = refs[n:]
    y = None
    off = 0
    for r, k in zip(lhs, widths):
        part = jnp.dot(r[0], w_ref[off:off + k, :], preferred_element_type=F32)
        y = part if y is None else y + part
        off += k
    gain = gate_ref[0] * g_ref[...]
    for rows in _row_blocks(y.shape[0]):
        o_ref[0, rows] = x_ref[0, rows] + _rms(y[rows]) * gain


def _out_res(parts, w, x, gate, g, *, tm):
    G, T, D = x.shape
    tm = min(tm, T)
    widths = tuple(p.shape[-1] for p in parts)
    K = sum(widths)
    in_specs = [pl.BlockSpec((1, tm, k), lambda b, i: (b, i, 0)) for k in widths]
    in_specs += [
        pl.BlockSpec((K, D), lambda b, i: (0, 0)),
        pl.BlockSpec((1, tm, D), lambda b, i: (b, i, 0)),
        pl.BlockSpec((1, 1, D), _vec_map(gate, 2)),
        pl.BlockSpec((1, D), lambda b, i: (0, 0)),
    ]
    return pl.pallas_call(
        functools.partial(_out_res_kernel, widths=widths),
        grid=(G, T // tm),
        in_specs=in_specs,
        out_specs=pl.BlockSpec((1, tm, D), lambda b, i: (b, i, 0)),
        out_shape=jax.ShapeDtypeStruct((G, T, D), F32),
        compiler_params=_cparams(("parallel", "parallel")),
        name="out_proj_residual",
    )(*parts, w, x, gate, g.reshape(1, D))


HALO = BF16_ROWS


def _ffn_kernel(x_ref, xp_ref, xn_ref, g2_ref, sh_ref, sc_ref, wa_ref, wv_ref, cwa_ref, cwv_ref,
                cba_ref, cbv_ref, wd_ref, gate_ref, g3_ref, o_ref, h_ref, acc_ref, ua_ref, uv_ref, *, tm):
    i = pl.program_id(1)
    f = pl.program_id(2)

    def normmod(x):
        return _rms(x) * (g2_ref[...] * (1.0 + sc_ref[0])) + sh_ref[0]

    @pl.when(f == 0)
    def _():
        d = x_ref.shape[-1]
        hp = jnp.where(i == 0, 0.0, normmod(xp_ref[0]))[SUBLANES - 1:SUBLANES]
        hn = jnp.where(i == pl.num_programs(1) - 1, 0.0, normmod(xn_ref[0]))[0:1]
        row = lax.broadcasted_iota(jnp.int32, (HALO, d), 0)
        halo = jnp.where(row == 0, hn, jnp.where(row == HALO - 1, hp, 0.0))
        for rows in _row_blocks(tm):
            h_ref[rows] = normmod(x_ref[0, rows]).astype(BF16)
        h_ref[tm:] = halo.astype(BF16)
        acc_ref[...] = jnp.zeros_like(acc_ref)

    h = h_ref[...]
    ua_ref[...] = jnp.dot(h, wa_ref[...], preferred_element_type=F32)
    uv_ref[...] = jnp.dot(h, wv_ref[...], preferred_element_type=F32)

    def conv(u_ref, cw_ref, cb_ref):
        prev = pltpu.roll(u_ref[...], 1, 0)[0:tm]
        return (prev * cw_ref[0:1, :] + u_ref[0:tm] * cw_ref[1:2, :]
                + u_ref[1:tm + 1] * cw_ref[2:3, :] + cb_ref[...])

    a = conv(ua_ref, cwa_ref, cba_ref)
    v = conv(uv_ref, cwv_ref, cbv_ref)
    gated = (a * jax.nn.sigmoid(a) * v).astype(BF16)
    acc_ref[...] += jnp.dot(gated, wd_ref[...], preferred_element_type=F32)

    @pl.when(f == pl.num_programs(2) - 1)
    def _():
        gain = gate_ref[0] * g3_ref[...]
        for rows in _row_blocks(tm):
            o_ref[0, rows] = x_ref[0, rows] + _rms(acc_ref[rows]) * gain


def _conv_ffn(x, g2, shift, scale, w_up, conv_w, conv_b, w_down, gate, g3, *, layer, tm, tf):
    G, T, D = x.shape
    F = w_down.shape[1]
    tm = min(tm, T)
    nf = F // tf
    n_i = T // tm
    rb = tm // SUBLANES
    last_rb = T // SUBLANES - 1
    conv_b = conv_b.reshape(1, 2 * F)
    return pl.pallas_call(
        functools.partial(_ffn_kernel, tm=tm),
        grid=(G, n_i, nf),
        in_specs=[
            pl.BlockSpec((1, tm, D), lambda b, i, f: (b, i, 0)),
            pl.BlockSpec((1, SUBLANES, D), lambda b, i, f: (b, jnp.maximum(i * rb - 1, 0), 0)),
            pl.BlockSpec((1, SUBLANES, D), lambda b, i, f: (b, jnp.minimum((i + 1) * rb, last_rb), 0)),
            pl.BlockSpec((1, D), lambda b, i, f: (0, 0)),
            pl.BlockSpec((1, 1, D), _vec_map(shift, 3)),
            pl.BlockSpec((1, 1, D), _vec_map(scale, 3)),
            pl.BlockSpec((None, D, tf), lambda b, i, f: (layer, 0, f)),
            pl.BlockSpec((None, D, tf), lambda b, i, f: (layer, 0, nf + f)),
            pl.BlockSpec((3, tf), lambda b, i, f: (0, f)),
            pl.BlockSpec((3, tf), lambda b, i, f: (0, nf + f)),
            pl.BlockSpec((1, tf), lambda b, i, f: (0, f)),
            pl.BlockSpec((1, tf), lambda b, i, f: (0, nf + f)),
            pl.BlockSpec((None, tf, D), lambda b, i, f: (layer, f, 0)),
            pl.BlockSpec((1, 1, D), _vec_map(gate, 3)),
            pl.BlockSpec((1, D), lambda b, i, f: (0, 0)),
        ],
        out_specs=pl.BlockSpec((1, tm, D), lambda b, i, f: (b, i, 0)),
        out_shape=jax.ShapeDtypeStruct((G, T, D), F32),
        scratch_shapes=[
            pltpu.VMEM((tm + HALO, D), BF16),
            pltpu.VMEM((tm, D), F32),
            pltpu.VMEM((tm + HALO, tf), F32),
            pltpu.VMEM((tm + HALO, tf), F32),
        ],
        compiler_params=_cparams(("parallel", "parallel", "arbitrary")),
        name="conv_ffn",
    )(x, x, x, g2.reshape(1, D), shift, scale, w_up, w_up, conv_w, conv_w, conv_b, conv_b, w_down,
      gate, g3.reshape(1, D))


GLA_BLOCKS = HGRN_CHUNK // SUBLANES


def _gla_operands(z, v, q, lb, rev):
    nb, rb = GLA_BLOCKS, SUBLANES
    order = list(range(nb))[::-1] if rev else list(range(nb))
    scan_of = {b: j for j, b in enumerate(order)}
    pos = lax.broadcasted_iota(jnp.int32, (rb, HGRN_DK), 0)
    if rev:
        pos = rb - 1 - pos

    def prev(x, k):
        return pltpu.roll(x, (rb - k) if rev else k, 0)

    def nxt(x, k):
        return pltpu.roll(x, k if rev else (rb - k), 0)

    def blocks(x):
        return [x[rb * b:rb * (b + 1)] for b in range(nb)]

    def rows(bl):
        return jnp.concatenate(bl, axis=0)

    f = lb + (1.0 - lb) * jax.nn.sigmoid(z)
    fb = blocks(f)
    kb = blocks(1.0 - f)

    hb = rb // 2
    hpos = pos & (hb - 1)
    lo_last = hb if rev else hb - 1
    hi_first = hb - 1 if rev else hb
    p4, s4, p8, s8, bt = [], [], [], [], []
    for b in range(nb):
        x = y = fb[b]
        for k in (1, 2):
            x = x * jnp.where(hpos >= k, prev(x, k), 1.0)
            y = y * jnp.where(hpos <= hb - 1 - k, nxt(y, k), 1.0)
        lo_tot = x[lo_last:lo_last + 1, :]
        hi_tot = y[hi_first:hi_first + 1, :]
        sfx = jnp.where(hpos <= hb - 2, nxt(y, 1), 1.0)
        p4.append(x)
        s4.append(sfx)
        p8.append(x * jnp.where(pos >= hb, lo_tot, 1.0))
        s8.append(sfx * jnp.where(pos < hb, hi_tot, 1.0))
        bt.append(lo_tot * hi_tot)
    bts = [bt[order[j]] for j in range(nb)]
    ones = jnp.ones_like(bts[0])
    before = [ones]
    for j in range(1, nb):
        before.append(before[j - 1] * bts[j - 1])
    after = [ones] * nb
    for j in range(nb - 2, -1, -1):
        after[j] = after[j + 1] * bts[j + 1]
    ftot = before[nb - 1] * bts[nb - 1]

    kbase = [kb[b] * s8[b] for b in range(nb)]
    ops = {"v": v, "ftot": ftot,
           "kdec": rows([kbase[b] * after[scan_of[b]] for b in range(nb)]).astype(BF16)}
    if q is None:
        return ops

    qb = blocks(q)
    q8 = [qb[b] * p8[b] for b in range(nb)]
    ops["q_state"] = rows([q8[b] * before[scan_of[b]] for b in range(nb)]).astype(BF16)
    ops["q_far"] = rows(q8).astype(BF16)
    ops["q_near"] = q.astype(BF16)

    vb = blocks(v.astype(F32))
    kcols, vcols = [], []
    chain = {}
    for d in range(1, nb):
        for j in range(nb - d):
            chain[j] = kbase[order[j]] if d == 1 else chain[j] * bts[j + d - 1]
            kcols.append(chain[j])
            vcols.append(vb[order[j]])
    ops["k_far"] = rows(kcols).astype(BF16)
    ops["v_far"] = rows(vcols).astype(BF16)

    ops["q_cross"] = rows([qb[b] * p4[b] for b in range(nb)]).astype(BF16)
    ops["k_cross"] = rows([kb[b] * s4[b] for b in range(nb)]).astype(BF16)

    kvar = []
    cur = kb
    for d in range(hb):
        if d:
            cur = [cur[b] * nxt(fb[b], d) for b in range(nb)]
        kvar.append(rows(cur).astype(BF16))
    ops["k_near"] = jnp.concatenate(kvar, axis=0)
    return ops


def _gla_first_dots(ops, st):
    nt = (((1,), (1,)), ((), ()))
    st_new = st * ops["ftot"] + lax.dot_general(ops["v"], ops["kdec"], (((0,), (0,)), ((), ())),
                                                preferred_element_type=F32)
    if "q_near" not in ops:
        return st_new, None
    o = lax.dot_general(ops["q_state"], st.astype(BF16), nt, preferred_element_type=F32)
    far = lax.dot_general(ops["q_far"], ops["k_far"], nt, preferred_element_type=F32)
    cross = lax.dot_general(ops["q_cross"], ops["k_cross"], nt, preferred_element_type=F32)
    near = lax.dot_general(ops["q_near"], ops["k_near"], nt, preferred_element_type=F32)
    return st_new, (o, far, cross, near)


def _gla_second_dots(ops, scores, masks):
    o, far, cross, near = scores
    m_far, m_cross, m_near = masks
    o = o + jnp.dot(far.astype(BF16) * m_far, ops["v_far"], preferred_element_type=F32)
    o = o + jnp.dot(cross.astype(BF16) * m_cross, ops["v"], preferred_element_type=F32)
    v_near = jnp.concatenate([ops["v"]] * (near.shape[1] // HGRN_CHUNK), axis=0)
    return o + jnp.dot(near.astype(BF16) * m_near, v_near, preferred_element_type=F32)


def _operand_rows():
    T, nb, rb = HGRN_CHUNK, GLA_BLOCKS, SUBLANES
    far = rb * nb * (nb - 1) // 2
    sizes = [("kdec", T), ("q_state", T), ("q_far", T), ("q_near", T), ("q_cross", T), ("k_cross", T),
             ("v", T), ("k_far", far), ("v_far", far), ("k_near", (rb // 2) * T)]
    rows, off = {}, 0
    for name, n in sizes:
        rows[name] = (off, n)
        off += n
    return rows, off


GLA_OPERAND_ROWS, GLA_OPERAND_TOTAL = _operand_rows()


def _gla_mask_tables():
    T, nb, rb = HGRN_CHUNK, GLA_BLOCKS, SUBLANES
    hb = rb // 2
    t = np.arange(T)
    far, cross, near = [], [], []
    for rev in (False, True):
        sblk = (nb - 1 - t // rb) if rev else t // rb
        spos = (rb - 1 - t % rb) if rev else t % rb
        same_block = t[:, None] // rb == t[None, :] // rb
        cols = [(d, j) for d in range(1, nb) for j in range(nb - d)]
        m_far = np.zeros((T, len(cols) * rb), np.float32)
        for c, (d, j) in enumerate(cols):
            m_far[sblk == j + d, c * rb:(c + 1) * rb] = 1.0
        m_cross = same_block & (spos[:, None] >= hb) & (spos[None, :] < hb)
        same_half = same_block & (spos[:, None] // hb == spos[None, :] // hb)
        m_near = np.zeros((T, hb * T), np.float32)
        for d in range(hb):
            m_near[:, d * T:(d + 1) * T] = same_half & (spos[:, None] - spos[None, :] == d)
        far.append(m_far)
        cross.append(m_cross.astype(np.float32))
        near.append(m_near)
    return tuple(jnp.asarray(np.stack(m), BF16) for m in (far, cross, near))


def _gla_kernel(q_ref, zf_ref, zb_ref, v_ref, g_ref, czf_ref, czb_ref, cv_ref, lb_ref, ng_ref,
                mfar_ref, mcross_ref, mnear_ref, o_ref, oacc_ref, ops_a_ref, ft_a_ref, ops_b_ref, ft_b_ref,
                *, n_lat, n_ctx, hp):
    T, dk = HGRN_CHUNK, HGRN_DK
    jobs = [(h, rev) for h in range(hp) for rev in (False, True)]
    zero_state = jnp.zeros((dk, dk), F32)

    def cols(h):
        return slice(h * dk, (h + 1) * dk)

    def start(rev, i, n):
        r = ((n - 1 - i) if rev else i) * T
        return r if isinstance(r, int) else pl.multiple_of(r, T)

    def lb_row(h, rev):
        return lb_ref[int(rev):int(rev) + 1, cols(h)]

    def prep_ctx(i):
        ops = []
        for h, rev in jobs:
            r = start(rev, i, n_ctx)
            z = (czb_ref if rev else czf_ref)[0, pl.ds(r, T), cols(h)].astype(F32)
            ops.append(_gla_operands(z, cv_ref[0, pl.ds(r, T), cols(h)], None, lb_row(h, rev), rev))
        return tuple(ops)

    def prep_lat(i):
        ops = []
        for h, rev in jobs:
            r = start(rev, i, n_lat)
            qz = q_ref[0, pl.ds(r, T), cols(h)].astype(F32)
            z = (zb_ref if rev else zf_ref)[0, pl.ds(r, T), cols(h)].astype(F32)
            ops.append(_gla_operands(z, v_ref[0, pl.ds(r, T), cols(h)], qz * jax.nn.sigmoid(qz),
                                     lb_row(h, rev), rev))
        return tuple(ops)

    def stash(ops, buf):
        ops_ref, ft_ref = buf
        for j, o in enumerate(ops):
            for name, (off, n) in GLA_OPERAND_ROWS.items():
                ops_ref[j, off:off + n, :] = o[name]
            ft_ref[j] = jnp.broadcast_to(o["ftot"], (SUBLANES, dk))

    def fetch(buf):
        ops_ref, ft_ref = buf
        out = []
        for j in range(len(jobs)):
            o = {name: ops_ref[j, off:off + n, :] for name, (off, n) in GLA_OPERAND_ROWS.items()}
            o["ftot"] = ft_ref[j, 0:1, :]
            out.append(o)
        return out

    def ctx_step(i, sts):
        return tuple(_gla_first_dots(o, st)[0] for o, st in zip(prep_ctx(i), sts))

    def lat_step(i, sts, second_half, cur, nxt):
        ops = fetch(cur)
        firsts = [_gla_first_dots(o, st) for o, st in zip(ops, sts)]
        stash(prep_lat(jnp.minimum(i + 1, n_lat - 1)), nxt)
        outs = [_gla_second_dots(o, f[1], tuple(m[int(rev)] for m in (mfar_ref, mcross_ref, mnear_ref)))
                for o, f, (_, rev) in zip(ops, firsts, jobs)]
        for (h, rev), o in zip(jobs, outs):
            r = start(rev, i, n_lat)
            if second_half:
                o = o + oacc_ref[pl.ds(r, T), cols(h)]
                gz = g_ref[0, pl.ds(r, T), cols(h)].astype(F32)
                o = _rms(o) * ng_ref[...] * (gz * jax.nn.sigmoid(gz))
                o_ref[0, pl.ds(r, T), cols(h)] = o.astype(o_ref.dtype)
            else:
                oacc_ref[pl.ds(r, T), cols(h)] = o
        return tuple(f[0] for f in firsts)

    buf_a, buf_b = (ops_a_ref, ft_a_ref), (ops_b_ref, ft_b_ref)

    def lat_pair(k, sts, second_half):
        sts = lat_step(2 * k, sts, second_half, buf_a, buf_b)
        return lat_step(2 * k + 1, sts, second_half, buf_b, buf_a)

    sts = lax.fori_loop(0, n_ctx, ctx_step, (zero_state,) * len(jobs))
    stash(prep_lat(0), buf_a)
    quarter = n_lat // 4
    sts = lax.fori_loop(0, quarter, functools.partial(lat_pair, second_half=False), sts)
    lax.fori_loop(quarter, 2 * quarter, functools.partial(lat_pair, second_half=True), sts)


GLA_HEADS_PER_STEP = 2


def _gla(p_lat, p_ctx, lb, norm_g, n_heads):
    B, L, _ = p_lat.shape
    Lc = p_ctx.shape[1]
    hp = GLA_HEADS_PER_STEP
    steps = n_heads // hp
    dv = HGRN_DK
    w = hp * dv

    def col(group):
        return lambda b, h: (b, 0, group * steps + h)

    lat_specs = [pl.BlockSpec((1, L, w), col(gidx)) for gidx in range(5)]
    ctx_specs = [pl.BlockSpec((1, Lc, w), col(gidx)) for gidx in (1, 2, 3)]
    masks = _gla_mask_tables()
    n_lat = L // HGRN_CHUNK
    assert n_lat % 4 == 0 and n_heads % hp == 0
    operand_bufs = [pltpu.VMEM((2 * hp, GLA_OPERAND_TOTAL, dv), BF16), pltpu.VMEM((2 * hp, SUBLANES, dv), F32)]
    return pl.pallas_call(
        functools.partial(_gla_kernel, n_lat=n_lat, n_ctx=Lc // HGRN_CHUNK, hp=hp),
        grid=(B, steps),
        in_specs=lat_specs + ctx_specs + [
            pl.BlockSpec((2, w), lambda b, h: (0, h)),
            pl.BlockSpec((1, dv), lambda b, h: (0, 0)),
        ] + [pl.BlockSpec(m.shape, lambda b, h: (0, 0, 0)) for m in masks],
        out_specs=pl.BlockSpec((1, L, w), lambda b, h: (b, 0, h)),
        out_shape=jax.ShapeDtypeStruct((B, L, n_heads * dv), BF16),
        scratch_shapes=[pltpu.VMEM((L, w), F32)] + operand_bufs * 2,
        compiler_params=_cparams(("parallel", "parallel")),
        name="hgrn2_bidirectional",
    )(*([p_lat] * 5), *([p_ctx] * 3), lb, norm_g.reshape(1, dv), *masks)


def _rope_tables(n_tokens):
    rows = n_tokens // GRID_W
    row = jnp.repeat(jnp.arange(rows, dtype=F32), GRID_W)
    colp = jnp.tile(jnp.arange(GRID_W, dtype=F32), rows)
    inv = ROPE_THETA ** (-jnp.arange(AXIS_FREQS, dtype=F32) / AXIS_FREQS)
    ang = jnp.stack([row[:, None] * inv, colp[:, None] * inv], axis=1)
    cos, sin = jnp.cos(ang), jnp.sin(ang)
    zero = jnp.zeros_like(sin)
    full = lambda a, b: jnp.stack([a, b], axis=2).reshape(n_tokens, HEAD_DIM)
    return full(cos, cos), full(-sin, zero), full(zero, sin)


def _identity_rope(n_tokens):
    return (jnp.ones((n_tokens, HEAD_DIM), F32), jnp.zeros((n_tokens, HEAD_DIM), F32),
            jnp.zeros((n_tokens, HEAD_DIM), F32))


def kernel(x, c, ctx, c_ctx, mod_w, mod_b, norm_g, ab_w_in, ab_w_out, attn_q_norm, attn_k_norm,
           s5_lam_re, s5_lam_im, s5_log_dt, s5_b_re, s5_b_im, s5_c_re, s5_c_im, s5_d,
           s5_glu_w, s5_glu_b, c_w_in, c_w_out, hgrn_lb_logits, hgrn_norm,
           ffn_w_up, ffn_conv_w, ffn_conv_b, ffn_w_down):
    B, L, D = x.shape
    Lc = ctx.shape[1]
    depth = mod_w.shape[0]
    attn_w = ATTN_HEADS * HEAD_DIM
    kv_w = ATTN_KV_HEADS * HEAD_DIM
    qk_w = attn_w + kv_w

    lb_all = jnp.cumsum(jax.nn.softmax(hgrn_lb_logits.astype(F32), axis=0), axis=0)
    lb_all = lb_all - lb_all[:1]

    cvec = jnp.concatenate([c, c_ctx[None], jnp.zeros((SUBLANES - (B + 1) % SUBLANES, D), F32)], axis=0)
    mods = _modulation(cvec, mod_w, mod_b)

    rope_lat = _rope_tables(L)
    rope_ctx = _identity_rope(Lc)
    w_up = ffn_w_up.astype(BF16)
    w_down = ffn_w_down.astype(BF16)

    for l in range(depth):
        last = l == depth - 1
        m_lat = [mods[l, :B, k * D:(k + 1) * D][:, None, :] for k in range(6)]
        m_ctx = [mods[l, B:B + 1, k * D:(k + 1) * D][:, None, :] for k in range(6)]
        g = norm_g[l]
        if l % 2 == 0:
            e = l // 2
            w_in = ab_w_in[e].astype(BF16)
            gain = jnp.concatenate([jnp.tile(attn_q_norm[e] * (HEAD_DIM ** -0.5 * math.log2(math.e)), ATTN_HEADS),
                                    jnp.tile(attn_k_norm[e], ATTN_KV_HEADS),
                                    jnp.ones((kv_w,), F32)]).reshape(1, qk_w + kv_w)
            p_lat, u_lat = _proj_ab(x, g[0], m_lat[0], m_lat[1], w_in, gain, rope_lat, tm=1024)
            p_ctx, u_ctx = _proj_ab(ctx, g[0], m_ctx[0], m_ctx[1], w_in, gain, rope_ctx, tm=1024)
            a_lat = _attention(p_lat, p_ctx, tq=512)
            tables = _s5_tables(s5_lam_re[e], s5_lam_im[e], s5_log_dt[e], s5_b_re[e], s5_b_im[e],
                                s5_c_re[e], s5_c_im[e], s5_d[e])
            y = _s5(u_ctx, u_lat, tables)
            glu_w = s5_glu_w[e].astype(BF16)
            w_out = ab_w_out[e].astype(BF16)
            s_lat = _glu(y, Lc // S5_CHUNK, L, glu_w, s5_glu_b[e], tm=256)
            x = _out_res([a_lat, s_lat], w_out, x, m_lat[2], g[1], tm=512)
            if not last:
                a_ctx = _attention(p_ctx, None, tq=128)
                s_ctx = _glu(y, 0, Lc, glu_w, s5_glu_b[e], tm=256)
                ctx = _out_res([a_ctx, s_ctx], w_out, ctx, m_ctx[2], g[1], tm=512)
        else:
            o_idx = l // 2
            w_in = c_w_in[o_idx].astype(BF16)
            n_heads = c_w_out.shape[1] // HGRN_DK
            p_lat = _proj(x, g[0], m_lat[0], m_lat[1], w_in, tm=1024, tn=2048)
            p_ctx = _proj(ctx, g[0], m_ctx[0], m_ctx[1], w_in, tm=1024, tn=2048)
            o_lat = _gla(p_lat, p_ctx, lb_all[l], hgrn_norm[o_idx], n_heads)
            x = _out_res([o_lat], c_w_out[o_idx].astype(BF16), x, m_lat[2], g[1], tm=512)
            assert last, "context outputs of the HGRN2 mixer are only needed by a following layer"
        x = _conv_ffn(x, g[2], m_lat[3], m_lat[4], w_up, ffn_conv_w[l], ffn_conv_b[l], w_down,
                      m_lat[5], g[3], layer=l, tm=512, tf=512)
        if not last:
            ctx = _conv_ffn(ctx, g[2], m_ctx[3], m_ctx[4], w_up, ffn_conv_w[l], ffn_conv_b[l], w_down,
                            m_ctx[5], g[3], layer=l, tm=512, tf=512)
    return x
```

```python
import functools
import math
from typing import NamedTuple

import jax
import jax.numpy as jnp
import numpy as np
from jax import lax
from jax.experimental import pallas as pl
from jax.experimental.pallas import tpu as pltpu

F32 = jnp.float32
BF16 = jnp.bfloat16
EPS = 1e-6

LANES = 128
SUBLANES = 8
BF16_ROWS = 16

HEAD_DIM = 128
ATTN_HEADS = 8
ATTN_KV_HEADS = 2
ATTN_GROUP = ATTN_HEADS // ATTN_KV_HEADS
GRID_W = 64
ROPE_THETA = 10000.0
AXIS_FREQS = HEAD_DIM // 4

S5_GROUP = 16
S5_STATE = 64
S5_CHUNK = 16

HGRN_DK = 128
HGRN_CHUNK = 64

VMEM_LIMIT = 56 * 1024 * 1024


class _Tiles(NamedTuple):
    proj_tm: int = 1024
    proj_tn: int = 2048
    attn_tq: int = 512
    attn_tq_ctx: int = 128
    glu_tm: int = 256
    out_tm: int = 512
    ffn_tm: int = 512
    ffn_tf: int = 512


TILES = _Tiles()


def _cparams(sem):
    return pltpu.CompilerParams(dimension_semantics=sem, vmem_limit_bytes=VMEM_LIMIT)


def _vec_map(vec, nd_grid):
    shared = vec.shape[0] == 1
    if nd_grid == 2:
        return lambda g, i: (0 if shared else g, 0, 0)
    return lambda g, i, j: (0 if shared else g, 0, 0)


def _rms(x):
    return x * lax.rsqrt(jnp.mean(x * x, axis=-1, keepdims=True) + EPS)


def _mod_kernel(s_ref, w_ref, b_ref, o_ref):
    s = s_ref[...]
    a = s * jax.nn.sigmoid(s)
    o_ref[0] = jnp.dot(a.astype(BF16), w_ref[0].astype(BF16), preferred_element_type=F32) + b_ref[0]


def _modulation(cvec, mod_w, mod_b):
    depth, d, n = mod_w.shape
    rows = cvec.shape[0]
    tn = 1024
    return pl.pallas_call(
        _mod_kernel,
        grid=(depth, n // tn),
        in_specs=[
            pl.BlockSpec((rows, d), lambda l, j: (0, 0)),
            pl.BlockSpec((1, d, tn), lambda l, j: (l, 0, j)),
            pl.BlockSpec((1, 1, tn), lambda l, j: (l, 0, j)),
        ],
        out_specs=pl.BlockSpec((1, rows, tn), lambda l, j: (l, 0, j)),
        out_shape=jax.ShapeDtypeStruct((depth, rows, n), F32),
        compiler_params=_cparams(("parallel", "parallel")),
        name="modulation",
    )(cvec, mod_w, mod_b.reshape(depth, 1, n))


NORM_ROWS = 32


def _row_blocks(n_rows):
    step = NORM_ROWS if n_rows % NORM_ROWS == 0 else n_rows
    return [slice(r, r + step) for r in range(0, n_rows, step)]


def _norm_mod_to(h_ref, x_ref, g_ref, sh_ref, sc_ref):
    gain = g_ref[...] * (1.0 + sc_ref[0])
    for rows in _row_blocks(h_ref.shape[0]):
        h_ref[rows] = (_rms(x_ref[0, rows]) * gain + sh_ref[0]).astype(BF16)


def _proj_kernel(x_ref, g_ref, sh_ref, sc_ref, w_ref, o_ref, h_ref):
    @pl.when(pl.program_id(2) == 0)
    def _():
        _norm_mod_to(h_ref, x_ref, g_ref, sh_ref, sc_ref)

    o_ref[0] = jnp.dot(h_ref[...], w_ref[...], preferred_element_type=F32).astype(o_ref.dtype)


def _proj_specs(x, g, shift, scale, w, tm, tn):
    G, T, D = x.shape
    in_specs = [
        pl.BlockSpec((1, tm, D), lambda b, i, j: (b, i, 0)),
        pl.BlockSpec((1, D), lambda b, i, j: (0, 0)),
        pl.BlockSpec((1, 1, D), _vec_map(shift, 3)),
        pl.BlockSpec((1, 1, D), _vec_map(scale, 3)),
        pl.BlockSpec((D, tn), lambda b, i, j: (0, j)),
    ]
    return in_specs, [x, g.reshape(1, D), shift, scale, w]


def _proj(x, g, shift, scale, w, *, tm, tn, out_dtype=BF16):
    G, T, D = x.shape
    N = w.shape[1]
    tm = min(tm, T)
    in_specs, args = _proj_specs(x, g, shift, scale, w, tm, tn)
    return pl.pallas_call(
        _proj_kernel,
        grid=(G, T // tm, N // tn),
        in_specs=in_specs,
        out_specs=pl.BlockSpec((1, tm, tn), lambda b, i, j: (b, i, j)),
        out_shape=jax.ShapeDtypeStruct((G, T, N), out_dtype),
        scratch_shapes=[pltpu.VMEM((tm, D), BF16)],
        compiler_params=_cparams(("parallel", "parallel", "arbitrary")),
        name="norm_mod_proj",
    )(*args)


QK_HEADS = ATTN_HEADS + ATTN_KV_HEADS


def _proj_ab_kernel(x_ref, g_ref, sh_ref, sc_ref, w_ref, gain_ref, cos_ref, s1_ref, s2_ref,
                    qk_ref, v_ref, u_ref, h_ref):
    j = pl.program_id(2)

    @pl.when(j == 0)
    def _():
        _norm_mod_to(h_ref, x_ref, g_ref, sh_ref, sc_ref)

    acc = jnp.dot(h_ref[...], w_ref[...], preferred_element_type=F32)

    def head(hh):
        return slice(hh * HEAD_DIM, (hh + 1) * HEAD_DIM)

    @pl.when(j == 0)
    def _():
        for hh in range(QK_HEADS):
            y = _rms(acc[:, head(hh)]) * gain_ref[:, head(hh)]
            y = (y * cos_ref[...] + pltpu.roll(y, HEAD_DIM - AXIS_FREQS, 1) * s1_ref[...]
                 + pltpu.roll(y, AXIS_FREQS, 1) * s2_ref[...])
            qk_ref[0, :, head(hh)] = y.astype(qk_ref.dtype)

    @pl.when(j == 1)
    def _():
        v_ref[0] = acc[:, :ATTN_KV_HEADS * HEAD_DIM].astype(v_ref.dtype)
        slabs, chunks = u_ref.shape[:2]
        for s in range(slabs):
            u_ref[s, :, 0] = acc[:, head(ATTN_KV_HEADS + s)].astype(u_ref.dtype).reshape(chunks, S5_CHUNK, LANES)


def _proj_ab(x, g, shift, scale, w, gain, rope, *, tm):
    G, T, D = x.shape
    tm = min(tm, T)
    tn = QK_HEADS * HEAD_DIM
    assert w.shape[1] == 2 * tn
    v_w = ATTN_KV_HEADS * HEAD_DIM
    slabs = (tn - v_w) // LANES
    in_specs, args = _proj_specs(x, g, shift, scale, w, tm, tn)
    in_specs += [pl.BlockSpec((1, tn), lambda b, i, j: (0, 0))]
    in_specs += [pl.BlockSpec((tm, HEAD_DIM), lambda b, i, j: (i, 0))] * 3
    return pl.pallas_call(
        _proj_ab_kernel,
        grid=(G, T // tm, 2),
        in_specs=in_specs,
        out_specs=[
            pl.BlockSpec((1, tm, tn), lambda b, i, j: (b, i, 0)),
            pl.BlockSpec((1, tm, v_w), lambda b, i, j: (b, i, 0)),
            pl.BlockSpec((slabs, tm // S5_CHUNK, 1, S5_CHUNK, LANES), lambda b, i, j: (0, i, b, 0, 0)),
        ],
        out_shape=[
            jax.ShapeDtypeStruct((G, T, tn), BF16),
            jax.ShapeDtypeStruct((G, T, v_w), BF16),
            jax.ShapeDtypeStruct((slabs, T // S5_CHUNK, G, S5_CHUNK, LANES), BF16),
        ],
        scratch_shapes=[pltpu.VMEM((tm, D), BF16)],
        compiler_params=_cparams(("parallel", "parallel", "arbitrary")),
        name="norm_mod_proj_ab",
    )(*args, gain, *rope)


ATTN_KEY_CHUNK = 512


def _attn_kernel(q_ref, k_ref, v_ref, *rest, tq, has_prefix):
    if has_prefix:
        kp_ref, vp_ref, o_ref = rest
    else:
        (o_ref,) = rest
    nt = (((1,), (1,)), ((), ()))
    q = jnp.concatenate(
        [q_ref[0, :, g * HEAD_DIM:(g + 1) * HEAD_DIM] for g in range(ATTN_GROUP)], axis=0)
    n_keys = k_ref.shape[1]
    kc = min(ATTN_KEY_CHUNK, n_keys)
    chunks = [(kp_ref, vp_ref, 0, kp_ref.shape[1])] if has_prefix else []
    chunks += [(k_ref, v_ref, c * kc, kc) for c in range(n_keys // kc)]

    def scores(chunk):
        kr, _, start, size = chunk
        return lax.dot_general(q, kr[0, start:start + size, :], nt, preferred_element_type=F32)

    m = jnp.full((q.shape[0], 1), -1e30, F32)
    l = jnp.zeros((q.shape[0], 1), F32)
    o = jnp.zeros((q.shape[0], HEAD_DIM), F32)
    s_next = scores(chunks[0])
    for ci, (_, vr, start, size) in enumerate(chunks):
        s = s_next
        if ci + 1 < len(chunks):
            s_next = scores(chunks[ci + 1])
        m_new = jnp.maximum(m, jnp.max(s, axis=-1, keepdims=True))
        alpha = jnp.exp2(m - m_new)
        p = jnp.exp2(s - m_new)
        l = alpha * l + jnp.sum(p, axis=-1, keepdims=True)
        o = alpha * o + jnp.dot(p.astype(BF16), vr[0, start:start + size, :], preferred_element_type=F32)
        m = m_new
    o = o * (1.0 / l)
    for g in range(ATTN_GROUP):
        o_ref[0, :, g * HEAD_DIM:(g + 1) * HEAD_DIM] = o[g * tq:(g + 1) * tq].astype(o_ref.dtype)


def _attention(qk, v, prefix, *, tq):
    B, L, _ = qk.shape
    tq = min(tq, L)
    gw = ATTN_GROUP * HEAD_DIM
    k_blk = ATTN_HEADS
    in_specs = [
        pl.BlockSpec((1, tq, gw), lambda b, h, i: (b, i, h)),
        pl.BlockSpec((1, L, HEAD_DIM), lambda b, h, i: (b, 0, k_blk + h)),
        pl.BlockSpec((1, L, HEAD_DIM), lambda b, h, i: (b, 0, h)),
    ]
    args = [qk, qk, v]
    if prefix is not None:
        Lp = prefix[0].shape[1]
        in_specs += [
            pl.BlockSpec((1, Lp, HEAD_DIM), lambda b, h, i: (b, 0, k_blk + h)),
            pl.BlockSpec((1, Lp, HEAD_DIM), lambda b, h, i: (b, 0, h)),
        ]
        args += list(prefix)
    return pl.pallas_call(
        functools.partial(_attn_kernel, tq=tq, has_prefix=prefix is not None),
        grid=(B, ATTN_KV_HEADS, L // tq),
        in_specs=in_specs,
        out_specs=pl.BlockSpec((1, tq, gw), lambda b, h, i: (b, i, h)),
        out_shape=jax.ShapeDtypeStruct((B, L, ATTN_HEADS * HEAD_DIM), BF16),
        compiler_params=_cparams(("parallel", "parallel", "arbitrary")),
        name="gqa_attention",
    )(*args)


S5_SLAB_GROUPS = LANES // S5_GROUP
S5_KD = S5_CHUNK * LANES
S5_PAIR_COLS = 4 * LANES


def _s5_end_kernel(u_ref, w_ref, e_ref):
    e_ref[...] = jnp.dot(u_ref[0], w_ref[0], preferred_element_type=F32)


def _s5_scan_kernel(e_ref, cf_ref, cb_ref, h_ref, *, batch, nc_ctx, nc_all):
    cps = SUBLANES // batch
    g_ctx, g_all = nc_ctx // cps, nc_all // cps

    def advance(hr, hi, er, ei, c_ref):
        ar, ai = c_ref[0:1, :], c_ref[1:2, :]
        return ar * hr - ai * hi + er, ar * hi + ai * hr + ei

    def sweep(r, state, col0, c_ref, order):
        e8 = e_ref[pl.ds(r, SUBLANES), col0:col0 + 2 * LANES]
        hr, hi = state
        out_r, out_i = [None] * cps, [None] * cps
        for c in order:
            out_r[c], out_i[c] = hr, hi
            rows = slice(c * batch, (c + 1) * batch)
            hr, hi = advance(hr, hi, e8[rows, :LANES], e8[rows, LANES:], c_ref)
        h_ref[pl.ds(r, SUBLANES), col0:col0 + LANES] = jnp.concatenate(out_r, axis=0).astype(h_ref.dtype)
        h_ref[pl.ds(r, SUBLANES), col0 + LANES:col0 + 2 * LANES] = (
            jnp.concatenate(out_i, axis=0).astype(h_ref.dtype))
        return hr, hi

    def body(i, carry):
        fwd, bwd = carry
        fwd = sweep(pl.multiple_of(i * SUBLANES, SUBLANES), fwd, 0, cf_ref, range(cps))
        gi = jnp.where(i < g_ctx, g_ctx - 1 - i, g_all - 1 - (i - g_ctx))
        bwd = sweep(pl.multiple_of(gi * SUBLANES, SUBLANES), bwd, 2 * LANES, cb_ref, reversed(range(cps)))
        return fwd, bwd

    zero = jnp.zeros((batch, LANES), F32)
    lax.fori_loop(0, g_all, body, ((zero, zero), (zero, zero)))


def _s5_out_kernel(u_ref, us_ref, lag_ref, h_ref, win_ref, d_ref, y_ref, tz_ref):
    T = S5_CHUNK
    t_blocks = tz_ref.shape[1] // LANES
    t0 = pl.program_id(1) * t_blocks
    for tt in range(t_blocks):
        for s in range(T):
            tz_ref[s * LANES:(s + 1) * LANES, tt * LANES:(tt + 1) * LANES] = lag_ref[0, t0 + tt - s + (T - 1)]
    y = jnp.dot(u_ref[0], tz_ref[...], preferred_element_type=F32)
    y += jnp.dot(h_ref[...].astype(BF16), win_ref[0], preferred_element_type=F32)
    y += us_ref[0].astype(F32) * d_ref[0]
    y_ref[0] = y.astype(y_ref.dtype)


def _s5_tables(lam_re, lam_im, log_dt, b_re, b_im, c_re, c_im, d_skip):
    T, C, P = S5_CHUNK, S5_GROUP, S5_STATE
    G = lam_re.shape[1]
    hi = lax.Precision.HIGHEST
    lam = lax.complex(jnp.minimum(lam_re.astype(F32), -1e-4), lam_im.astype(F32))
    dt = jnp.exp(log_dt.astype(F32))[..., None]
    lam_dt = lam * dt
    lam_bar = jnp.exp(lam_dt)
    bmat = lax.complex(b_re.astype(F32), b_im.astype(F32))
    b_bar = ((lam_bar - 1.0) / lam)[..., None] * bmat
    cmat = lax.complex(c_re.astype(F32), c_im.astype(F32))
    steps = jnp.arange(T + 1, dtype=F32)
    pw = jnp.exp(lam_dt[:, None] * steps[None, :, None, None])

    kern = jnp.real(jnp.einsum('zgcp,zjgp,zgpd->zjgcd', cmat, pw[:, :T], b_bar, precision=hi))

    def end_w(p_sel, bb):
        w = p_sel[:, :, None, :] * bb.transpose(0, 2, 1)[None]
        return [jnp.real(w), jnp.imag(w)]

    w_end = jnp.concatenate(end_w(pw[0, :T][::-1], b_bar[0]) + end_w(pw[1, :T], b_bar[1]), axis=-1)

    def in_w(p_sel, cm):
        z = cm.transpose(0, 2, 1)[:, :, None, :] * p_sel.transpose(1, 2, 0)[:, :, :, None]
        return [jnp.real(z), -jnp.imag(z)]

    w_in = jnp.stack(in_w(pw[0, 1:], cmat[0]) + in_w(pw[1, 1:][::-1], cmat[1]), axis=1)

    def coef(z):
        return jnp.stack([jnp.real(z).reshape(G * P), jnp.imag(z).reshape(G * P)])

    sg = S5_SLAB_GROUPS
    ngb, npair = G // sg, sg // 2
    grp = np.arange(sg)
    tok_cols_grp = np.tile(np.repeat(grp, C), T)
    state_cols_grp = (2 * np.arange(npair)[:, None, None, None] + np.arange(2)[None, None, :, None]
                      + np.zeros((1, 4, 1, P), np.int64)).reshape(-1)
    rows_gd = np.repeat(grp, C)

    lag_k = jnp.concatenate([kern[1][1:][::-1], (kern[0][0] + kern[1][0])[None], kern[0][1:]], axis=0)
    lag_src = lag_k.reshape(2 * T - 1, ngb, sg, C, C).transpose(1, 0, 2, 4, 3).reshape(ngb * (2 * T - 1), LANES, C)
    rep_c = np.tile(np.eye(C, dtype=np.float32), (1, sg))
    mask_c = (rows_gd[:, None] == np.repeat(grp, C)[None, :]).astype(np.float32)[None]
    lag_tiles = _spread(lag_src, rep_c, mask_c, lambda n: 0).reshape(ngb, 2 * T - 1, LANES, LANES)

    end_src = w_end.reshape(T, ngb, sg * C, 4 * P).transpose(1, 0, 2, 3).reshape(ngb * T, LANES, 4 * P)
    rep_state = np.tile(np.eye(4 * P, dtype=np.float32).reshape(4 * P, 1, 4, 1, P), (1, npair, 1, 2, 1))
    rep_state = rep_state.reshape(4 * P, npair * S5_PAIR_COLS)
    mask_state = (rows_gd[:, None] == state_cols_grp[None, :]).astype(np.float32)[None]
    end_slab = _spread(end_src, rep_state, mask_state, lambda n: 0).reshape(ngb, S5_KD, npair * S5_PAIR_COLS)

    in_src = w_in.reshape(ngb, npair, 2, 4, P, T * C).transpose(0, 1, 3, 2, 4, 5)
    in_src = in_src.reshape(ngb * npair * 4, 2 * P, T * C)
    rep_tok = np.tile(np.eye(T * C, dtype=np.float32).reshape(T * C, T, 1, C), (1, 1, sg, 1)).reshape(T * C, S5_KD)
    row_grp = 2 * np.arange(npair)[:, None, None] + np.repeat(np.arange(2), P)[None, :, None]
    mask_tok = (row_grp == tok_cols_grp[None, None, :]).astype(np.float32)
    in_slab = _spread(in_src, rep_tok, mask_tok, lambda n: (n // 4) % npair)
    in_slab = in_slab.reshape(ngb, npair * S5_PAIR_COLS, S5_KD)

    d_slab = jnp.tile(d_skip.astype(F32).reshape(ngb, 1, LANES), (1, 1, T))
    return lag_tiles, end_slab, in_slab, coef(pw[0, T]), coef(pw[1, T]), d_slab


SPREAD_TILES = 4


def _spread_kernel(src_ref, rep_ref, mask_ref, o_ref):
    for i in range(SPREAD_TILES):
        o_ref[i] = (jnp.dot(src_ref[i].astype(BF16), rep_ref[...], preferred_element_type=F32)
                    * mask_ref[0]).astype(o_ref.dtype)


def _spread(src, rep, mask, mask_index):
    n, rows, k = src.shape
    width = rep.shape[1]
    st = SPREAD_TILES
    assert n % st == 0
    return pl.pallas_call(
        _spread_kernel,
        grid=(n // st,),
        in_specs=[
            pl.BlockSpec((st, rows, k), lambda i: (i, 0, 0)),
            pl.BlockSpec((k, width), lambda i: (0, 0)),
            pl.BlockSpec((1, rows, width), lambda i: (mask_index(i * st), 0, 0)),
        ],
        out_specs=pl.BlockSpec((st, rows, width), lambda i: (i, 0, 0)),
        out_shape=jax.ShapeDtypeStruct((n, rows, width), BF16),
        compiler_params=_cparams(("parallel",)),
        name="s5_spread_operator",
    )(src, jnp.asarray(rep, BF16), jnp.asarray(mask, F32))


def _s5(u_ctx, u_lat, tables):
    tz, w_end, w_in, cf, cb, d_slab = tables
    ngb, nc_ctx, B = u_ctx.shape[:3]
    nc_all = nc_ctx + u_lat.shape[1]
    cps = SUBLANES // B
    assert SUBLANES % B == 0 and nc_ctx % cps == 0 and (nc_all - nc_ctx) % cps == 0
    M = nc_all * B
    kd, sw = S5_KD, w_end.shape[-1]
    n_half = 4
    hw = kd // n_half
    u = jnp.concatenate([u_ctx, u_lat], axis=1).reshape(ngb, M, kd)

    e = pl.pallas_call(
        _s5_end_kernel,
        grid=(ngb, n_half),
        in_specs=[
            pl.BlockSpec((1, M, kd), lambda g, j: (g, 0, 0)),
            pl.BlockSpec((1, kd, sw // n_half), lambda g, j: (g, 0, j)),
        ],
        out_specs=pl.BlockSpec((M, sw // n_half), lambda g, j: (0, g * n_half + j)),
        out_shape=jax.ShapeDtypeStruct((M, ngb * sw), F32),
        compiler_params=_cparams(("parallel", "parallel")),
        name="s5_chunk_end_states",
    )(u, w_end)

    h = pl.pallas_call(
        functools.partial(_s5_scan_kernel, batch=B, nc_ctx=nc_ctx, nc_all=nc_all),
        grid=(ngb * sw // S5_PAIR_COLS,),
        in_specs=[pl.BlockSpec((M, S5_PAIR_COLS), lambda j: (0, j))]
        + [pl.BlockSpec((2, LANES), lambda j: (0, j))] * 2,
        out_specs=pl.BlockSpec((M, S5_PAIR_COLS), lambda j: (0, j)),
        out_shape=jax.ShapeDtypeStruct((M, ngb * sw), F32),
        compiler_params=_cparams(("parallel",)),
        name="s5_chunk_scan",
    )(e, cf, cb)

    y = pl.pallas_call(
        _s5_out_kernel,
        grid=(ngb, n_half),
        in_specs=[
            pl.BlockSpec((1, M, kd), lambda g, j: (g, 0, 0)),
            pl.BlockSpec((1, M, hw), lambda g, j: (g, 0, j)),
            pl.BlockSpec((1,) + tz.shape[1:], lambda g, j: (g, 0, 0, 0)),
            pl.BlockSpec((M, sw), lambda g, j: (0, g)),
            pl.BlockSpec((1, sw, hw), lambda g, j: (g, 0, j)),
            pl.BlockSpec((1, 1, hw), lambda g, j: (g, 0, j)),
        ],
        out_specs=pl.BlockSpec((1, M, hw), lambda g, j: (g, 0, j)),
        out_shape=jax.ShapeDtypeStruct((ngb, M, kd), BF16),
        scratch_shapes=[pltpu.VMEM((kd, hw), BF16)],
        compiler_params=_cparams(("parallel", "parallel")),
        name="s5_chunk_outputs",
    )(u, u, tz, h, w_in, d_slab)

    return y.reshape(ngb, nc_all, B, S5_CHUNK, LANES)


def _glu_kernel(y_ref, w_ref, b_ref, o_ref):
    slabs, chunks = y_ref.shape[:2]
    y = jnp.concatenate([y_ref[s, :, 0].reshape(chunks * S5_CHUNK, LANES) for s in range(slabs)], axis=1)
    z = jnp.dot(jax.nn.gelu(y.astype(F32)).astype(BF16), w_ref[...], preferred_element_type=F32) + b_ref[...]
    n = o_ref.shape[-1]
    o_ref[0] = (z[:, :n] * jax.nn.sigmoid(z[:, n:])).astype(o_ref.dtype)


def _glu(y, first_chunk, n_tokens, w, b, *, tm):
    slabs, _, B = y.shape[:3]
    W = slabs * LANES
    tm = min(tm, n_tokens)
    tc = tm // S5_CHUNK
    assert first_chunk % tc == 0
    off = first_chunk // tc
    return pl.pallas_call(
        _glu_kernel,
        grid=(B, n_tokens // tm),
        in_specs=[
            pl.BlockSpec((slabs, tc, 1, S5_CHUNK, LANES), lambda g, i: (0, off + i, g, 0, 0)),
            pl.BlockSpec((W, 2 * W), lambda g, i: (0, 0)),
            pl.BlockSpec((1, 2 * W), lambda g, i: (0, 0)),
        ],
        out_specs=pl.BlockSpec((1, tm, W), lambda g, i: (g, i, 0)),
        out_shape=jax.ShapeDtypeStruct((B, n_tokens, W), BF16),
        compiler_params=_cparams(("parallel", "parallel")),
        name="s5_glu",
    )(y, w, b.reshape(1, 2 * W))


def _out_res_kernel(*refs, widths):
    n = len(widths)
    lhs = refs[:n]
    w_ref, x_ref, gate_ref, g_ref, o_ref = refs[n:]
    y = None
    off = 0
    for r, k in zip(lhs, widths):
        part = jnp.dot(r[0], w_ref[off:off + k, :], preferred_element_type=F32)
        y = part if y is None else y + part
        off += k
    gain = gate_ref[0] * g_ref[...]
    for rows in _row_blocks(y.shape[0]):
        o_ref[0, rows] = x_ref[0, rows] + _rms(y[rows]) * gain


def _out_res(parts, w, x, gate, g, *, tm):
    G, T, D = x.shape
    tm = min(tm, T)
    widths = tuple(p.shape[-1] for p in parts)
    K = sum(widths)
    in_specs = [pl.BlockSpec((1, tm, k), lambda b, i: (b, i, 0)) for k in widths]
    in_specs += [
        pl.BlockSpec((K, D), lambda b, i: (0, 0)),
        pl.BlockSpec((1, tm, D), lambda b, i: (b, i, 0)),
        pl.BlockSpec((1, 1, D), _vec_map(gate, 2)),
        pl.BlockSpec((1, D), lambda b, i: (0, 0)),
    ]
    return pl.pallas_call(
        functools.partial(_out_res_kernel, widths=widths),
        grid=(G, T // tm),
        in_specs=in_specs,
        out_specs=pl.BlockSpec((1, tm, D), lambda b, i: (b, i, 0)),
        out_shape=jax.ShapeDtypeStruct((G, T, D), F32),
        compiler_params=_cparams(("parallel", "parallel")),
        name="out_proj_residual",
    )(*parts, w, x, gate, g.reshape(1, D))


HALO = BF16_ROWS


def _ffn_kernel(x_ref, xp_ref, xn_ref, g2_ref, sh_ref, sc_ref, wa_ref, wv_ref, cwa_ref, cwv_ref,
                cba_ref, cbv_ref, wd_ref, gate_ref, g3_ref, o_ref, h_ref, acc_ref, ua_ref, uv_ref, *, tm):
    i = pl.program_id(1)
    f = pl.program_id(2)

    def normmod(x):
        return _rms(x) * (g2_ref[...] * (1.0 + sc_ref[0])) + sh_ref[0]

    @pl.when(f == 0)
    def _():
        d = x_ref.shape[-1]
        hp = jnp.where(i == 0, 0.0, normmod(xp_ref[0]))[SUBLANES - 1:SUBLANES]
        hn = jnp.where(i == pl.num_programs(1) - 1, 0.0, normmod(xn_ref[0]))[0:1]
        row = lax.broadcasted_iota(jnp.int32, (HALO, d), 0)
        halo = jnp.where(row == 0, hn, jnp.where(row == HALO - 1, hp, 0.0))
        for rows in _row_blocks(tm):
            h_ref[rows] = normmod(x_ref[0, rows]).astype(BF16)
        h_ref[tm:] = halo.astype(BF16)
        acc_ref[...] = jnp.zeros_like(acc_ref)

    h = h_ref[...]
    ua_ref[...] = jnp.dot(h, wa_ref[...], preferred_element_type=F32)
    uv_ref[...] = jnp.dot(h, wv_ref[...], preferred_element_type=F32)

    def conv(u_ref, cw_ref, cb_ref):
        prev = pltpu.roll(u_ref[...], 1, 0)[0:tm]
        return (prev * cw_ref[0:1, :] + u_ref[0:tm] * cw_ref[1:2, :]
                + u_ref[1:tm + 1] * cw_ref[2:3, :] + cb_ref[...])

    a = conv(ua_ref, cwa_ref, cba_ref)
    v = conv(uv_ref, cwv_ref, cbv_ref)
    gated = (a * jax.nn.sigmoid(a) * v).astype(BF16)
    acc_ref[...] += jnp.dot(gated, wd_ref[...], preferred_element_type=F32)

    @pl.when(f == pl.num_programs(2) - 1)
    def _():
        gain = gate_ref[0] * g3_ref[...]
        for rows in _row_blocks(tm):
            o_ref[0, rows] = x_ref[0, rows] + _rms(acc_ref[rows]) * gain


def _conv_ffn(x, g2, shift, scale, w_up, conv_w, conv_b, w_down, gate, g3, *, layer, tm, tf):
    G, T, D = x.shape
    F = w_down.shape[1]
    tm = min(tm, T)
    nf = F // tf
    n_i = T // tm
    rb = tm // SUBLANES
    last_rb = T // SUBLANES - 1
    conv_b = conv_b.reshape(1, 2 * F)
    return pl.pallas_call(
        functools.partial(_ffn_kernel, tm=tm),
        grid=(G, n_i, nf),
        in_specs=[
            pl.BlockSpec((1, tm, D), lambda b, i, f: (b, i, 0)),
            pl.BlockSpec((1, SUBLANES, D), lambda b, i, f: (b, jnp.maximum(i * rb - 1, 0), 0)),
            pl.BlockSpec((1, SUBLANES, D), lambda b, i, f: (b, jnp.minimum((i + 1) * rb, last_rb), 0)),
            pl.BlockSpec((1, D), lambda b, i, f: (0, 0)),
            pl.BlockSpec((1, 1, D), _vec_map(shift, 3)),
            pl.BlockSpec((1, 1, D), _vec_map(scale, 3)),
            pl.BlockSpec((None, D, tf), lambda b, i, f: (layer, 0, f)),
            pl.BlockSpec((None, D, tf), lambda b, i, f: (layer, 0, nf + f)),
            pl.BlockSpec((3, tf), lambda b, i, f: (0, f)),
            pl.BlockSpec((3, tf), lambda b, i, f: (0, nf + f)),
            pl.BlockSpec((1, tf), lambda b, i, f: (0, f)),
            pl.BlockSpec((1, tf), lambda b, i, f: (0, nf + f)),
            pl.BlockSpec((None, tf, D), lambda b, i, f: (layer, f, 0)),
            pl.BlockSpec((1, 1, D), _vec_map(gate, 3)),
            pl.BlockSpec((1, D), lambda b, i, f: (0, 0)),
        ],
        out_specs=pl.BlockSpec((1, tm, D), lambda b, i, f: (b, i, 0)),
        out_shape=jax.ShapeDtypeStruct((G, T, D), F32),
        scratch_shapes=[
            pltpu.VMEM((tm + HALO, D), BF16),
            pltpu.VMEM((tm, D), F32),
            pltpu.VMEM((tm + HALO, tf), F32),
            pltpu.VMEM((tm + HALO, tf), F32),
        ],
        compiler_params=_cparams(("parallel", "parallel", "arbitrary")),
        name="conv_ffn",
    )(x, x, x, g2.reshape(1, D), shift, scale, w_up, w_up, conv_w, conv_w, conv_b, conv_b, w_down,
      gate, g3.reshape(1, D))


GLA_BLOCKS = HGRN_CHUNK // SUBLANES


def _gla_operands(z, v, q, lb, rev):
    nb, rb = GLA_BLOCKS, SUBLANES
    order = list(range(nb))[::-1] if rev else list(range(nb))
    scan_of = {b: j for j, b in enumerate(order)}
    pos = lax.broadcasted_iota(jnp.int32, (rb, HGRN_DK), 0)
    if rev:
        pos = rb - 1 - pos

    def prev(x, k):
        return pltpu.roll(x, (rb - k) if rev else k, 0)

    def nxt(x, k):
        return pltpu.roll(x, k if rev else (rb - k), 0)

    def blocks(x):
        return [x[rb * b:rb * (b + 1)] for b in range(nb)]

    def rows(bl):
        return jnp.concatenate(bl, axis=0)

    f = lb + (1.0 - lb) * jax.nn.sigmoid(z)
    fb = blocks(f)
    kb = blocks(1.0 - f)

    hb = rb // 2
    hpos = pos & (hb - 1)
    lo_last = hb if rev else hb - 1
    hi_first = hb - 1 if rev else hb
    p4, s4, p8, s8, bt = [], [], [], [], []
    for b in range(nb):
        x = y = fb[b]
        for k in (1, 2):
            x = x * jnp.where(hpos >= k, prev(x, k), 1.0)
            y = y * jnp.where(hpos <= hb - 1 - k, nxt(y, k), 1.0)
        lo_tot = x[lo_last:lo_last + 1, :]
        hi_tot = y[hi_first:hi_first + 1, :]
        sfx = jnp.where(hpos <= hb - 2, nxt(y, 1), 1.0)
        p4.append(x)
        s4.append(sfx)
        p8.append(x * jnp.where(pos >= hb, lo_tot, 1.0))
        s8.append(sfx * jnp.where(pos < hb, hi_tot, 1.0))
        bt.append(lo_tot * hi_tot)
    bts = [bt[order[j]] for j in range(nb)]
    ones = jnp.ones_like(bts[0])
    before = [ones]
    for j in range(1, nb):
        before.append(before[j - 1] * bts[j - 1])
    after = [ones] * nb
    for j in range(nb - 2, -1, -1):
        after[j] = after[j + 1] * bts[j + 1]
    ftot = before[nb - 1] * bts[nb - 1]

    kbase = [kb[b] * s8[b] for b in range(nb)]
    ops = {"v": v, "ftot": ftot,
           "kdec": rows([kbase[b] * after[scan_of[b]] for b in range(nb)]).astype(BF16)}
    if q is None:
        return ops

    qb = blocks(q)
    q8 = [qb[b] * p8[b] for b in range(nb)]
    ops["q_state"] = rows([q8[b] * before[scan_of[b]] for b in range(nb)]).astype(BF16)
    ops["q_far"] = rows(q8).astype(BF16)
    ops["q_near"] = q.astype(BF16)

    vb = blocks(v.astype(F32))
    kcols, vcols = [], []
    chain = {}
    for d in range(1, nb):
        for j in range(nb - d):
            chain[j] = kbase[order[j]] if d == 1 else chain[j] * bts[j + d - 1]
            kcols.append(chain[j])
            vcols.append(vb[order[j]])
    ops["k_far"] = rows(kcols).astype(BF16)
    ops["v_far"] = rows(vcols).astype(BF16)

    ops["q_cross"] = rows([qb[b] * p4[b] for b in range(nb)]).astype(BF16)
    ops["k_cross"] = rows([kb[b] * s4[b] for b in range(nb)]).astype(BF16)

    kvar = []
    cur = kb
    for d in range(hb):
        if d:
            cur = [cur[b] * nxt(fb[b], d) for b in range(nb)]
        kvar.append(rows(cur).astype(BF16))
    ops["k_near"] = jnp.concatenate(kvar, axis=0)
    return ops


def _gla_first_dots(ops, st):
    nt = (((1,), (1,)), ((), ()))
    st_new = st * ops["ftot"] + lax.dot_general(ops["v"], ops["kdec"], (((0,), (0,)), ((), ())),
                                                preferred_element_type=F32)
    if "q_near" not in ops:
        return st_new, None
    o = lax.dot_general(ops["q_state"], st.astype(BF16), nt, preferred_element_type=F32)
    far = lax.dot_general(ops["q_far"], ops["k_far"], nt, preferred_element_type=F32)
    cross = lax.dot_general(ops["q_cross"], ops["k_cross"], nt, preferred_element_type=F32)
    near = lax.dot_general(ops["q_near"], ops["k_near"], nt, preferred_element_type=F32)
    return st_new, (o, far, cross, near)


def _gla_second_dots(ops, scores, masks):
    o, far, cross, near = scores
    m_far, m_cross, m_near = masks
    o = o + jnp.dot(far.astype(BF16) * m_far, ops["v_far"], preferred_element_type=F32)
    o = o + jnp.dot(cross.astype(BF16) * m_cross, ops["v"], preferred_element_type=F32)
    v_near = jnp.concatenate([ops["v"]] * (near.shape[1] // HGRN_CHUNK), axis=0)
    return o + jnp.dot(near.astype(BF16) * m_near, v_near, preferred_element_type=F32)


def _operand_rows():
    T, nb, rb = HGRN_CHUNK, GLA_BLOCKS, SUBLANES
    far = rb * nb * (nb - 1) // 2
    sizes = [("kdec", T), ("q_state", T), ("q_far", T), ("q_near", T), ("q_cross", T), ("k_cross", T),
             ("v", T), ("k_far", far), ("v_far", far), ("k_near", (rb // 2) * T)]
    rows, off = {}, 0
    for name, n in sizes:
        rows[name] = (off, n)
        off += n
    return rows, off


GLA_OPERAND_ROWS, GLA_OPERAND_TOTAL = _operand_rows()


def _gla_mask_tables():
    T, nb, rb = HGRN_CHUNK, GLA_BLOCKS, SUBLANES
    hb = rb // 2
    t = np.arange(T)
    far, cross, near = [], [], []
    for rev in (False, True):
        sblk = (nb - 1 - t // rb) if rev else t // rb
        spos = (rb - 1 - t % rb) if rev else t % rb
        same_block = t[:, None] // rb == t[None, :] // rb
        cols = [(d, j) for d in range(1, nb) for j in range(nb - d)]
        m_far = np.zeros((T, len(cols) * rb), np.float32)
        for c, (d, j) in enumerate(cols):
            m_far[sblk == j + d, c * rb:(c + 1) * rb] = 1.0
        m_cross = same_block & (spos[:, None] >= hb) & (spos[None, :] < hb)
        same_half = same_block & (spos[:, None] // hb == spos[None, :] // hb)
        m_near = np.zeros((T, hb * T), np.float32)
        for d in range(hb):
            m_near[:, d * T:(d + 1) * T] = same_half & (spos[:, None] - spos[None, :] == d)
        far.append(m_far)
        cross.append(m_cross.astype(np.float32))
        near.append(m_near)
    return tuple(jnp.asarray(np.stack(m), BF16) for m in (far, cross, near))


def _gla_kernel(q_ref, zf_ref, zb_ref, v_ref, g_ref, czf_ref, czb_ref, cv_ref, lb_ref, ng_ref,
                mfar_ref, mcross_ref, mnear_ref, o_ref, oacc_ref, ops_a_ref, ft_a_ref, ops_b_ref, ft_b_ref,
                *, n_lat, n_ctx, hp):
    T, dk = HGRN_CHUNK, HGRN_DK
    jobs = [(h, rev) for h in range(hp) for rev in (False, True)]
    zero_state = jnp.zeros((dk, dk), F32)

    def cols(h):
        return slice(h * dk, (h + 1) * dk)

    def start(rev, i, n):
        r = ((n - 1 - i) if rev else i) * T
        return r if isinstance(r, int) else pl.multiple_of(r, T)

    def lb_row(h, rev):
        return lb_ref[int(rev):int(rev) + 1, cols(h)]

    def prep_ctx(i):
        ops = []
        for h, rev in jobs:
            r = start(rev, i, n_ctx)
            z = (czb_ref if rev else czf_ref)[0, pl.ds(r, T), cols(h)].astype(F32)
            ops.append(_gla_operands(z, cv_ref[0, pl.ds(r, T), cols(h)], None, lb_row(h, rev), rev))
        return tuple(ops)

    def prep_lat(i):
        ops = []
        for h, rev in jobs:
            r = start(rev, i, n_lat)
            qz = q_ref[0, pl.ds(r, T), cols(h)].astype(F32)
            z = (zb_ref if rev else zf_ref)[0, pl.ds(r, T), cols(h)].astype(F32)
            ops.append(_gla_operands(z, v_ref[0, pl.ds(r, T), cols(h)], qz * jax.nn.sigmoid(qz),
                                     lb_row(h, rev), rev))
        return tuple(ops)

    def stash(ops, buf):
        ops_ref, ft_ref = buf
        for j, o in enumerate(ops):
            for name, (off, n) in GLA_OPERAND_ROWS.items():
                ops_ref[j, off:off + n, :] = o[name]
            ft_ref[j] = jnp.broadcast_to(o["ftot"], (SUBLANES, dk))

    def fetch(buf):
        ops_ref, ft_ref = buf
        out = []
        for j in range(len(jobs)):
            o = {name: ops_ref[j, off:off + n, :] for name, (off, n) in GLA_OPERAND_ROWS.items()}
            o["ftot"] = ft_ref[j, 0:1, :]
            out.append(o)
        return out

    def ctx_step(i, sts):
        return tuple(_gla_first_dots(o, st)[0] for o, st in zip(prep_ctx(i), sts))

    def lat_step(i, sts, second_half, cur, nxt):
        ops = fetch(cur)
        firsts = [_gla_first_dots(o, st) for o, st in zip(ops, sts)]
        stash(prep_lat(jnp.minimum(i + 1, n_lat - 1)), nxt)
        outs = [_gla_second_dots(o, f[1], tuple(m[int(rev)] for m in (mfar_ref, mcross_ref, mnear_ref)))
                for o, f, (_, rev) in zip(ops, firsts, jobs)]
        for (h, rev), o in zip(jobs, outs):
            r = start(rev, i, n_lat)
            if second_half:
                o = o + oacc_ref[pl.ds(r, T), cols(h)]
                gz = g_ref[0, pl.ds(r, T), cols(h)].astype(F32)
                o = _rms(o) * ng_ref[...] * (gz * jax.nn.sigmoid(gz))
                o_ref[0, pl.ds(r, T), cols(h)] = o.astype(o_ref.dtype)
            else:
                oacc_ref[pl.ds(r, T), cols(h)] = o
        return tuple(f[0] for f in firsts)

    buf_a, buf_b = (ops_a_ref, ft_a_ref), (ops_b_ref, ft_b_ref)

    def lat_pair(k, sts, second_half):
        sts = lat_step(2 * k, sts, second_half, buf_a, buf_b)
        return lat_step(2 * k + 1, sts, second_half, buf_b, buf_a)

    sts = lax.fori_loop(0, n_ctx, ctx_step, (zero_state,) * len(jobs))
    stash(prep_lat(0), buf_a)
    quarter = n_lat // 4
    sts = lax.fori_loop(0, quarter, functools.partial(lat_pair, second_half=False), sts)
    lax.fori_loop(quarter, 2 * quarter, functools.partial(lat_pair, second_half=True), sts)


GLA_HEADS_PER_STEP = 2


def _gla(p_lat, p_ctx, lb, norm_g, n_heads):
    B, L, _ = p_lat.shape
    Lc = p_ctx.shape[1]
    hp = GLA_HEADS_PER_STEP
    steps = n_heads // hp
    dv = HGRN_DK
    w = hp * dv

    def col(group):
        return lambda b, h: (b, 0, group * steps + h)

    lat_specs = [pl.BlockSpec((1, L, w), col(gidx)) for gidx in range(5)]
    ctx_specs = [pl.BlockSpec((1, Lc, w), col(gidx)) for gidx in (1, 2, 3)]
    masks = _gla_mask_tables()
    n_lat = L // HGRN_CHUNK
    assert n_lat % 4 == 0 and n_heads % hp == 0
    operand_bufs = [pltpu.VMEM((2 * hp, GLA_OPERAND_TOTAL, dv), BF16), pltpu.VMEM((2 * hp, SUBLANES, dv), F32)]
    return pl.pallas_call(
        functools.partial(_gla_kernel, n_lat=n_lat, n_ctx=Lc // HGRN_CHUNK, hp=hp),
        grid=(B, steps),
        in_specs=lat_specs + ctx_specs + [
            pl.BlockSpec((2, w), lambda b, h: (0, h)),
            pl.BlockSpec((1, dv), lambda b, h: (0, 0)),
        ] + [pl.BlockSpec(m.shape, lambda b, h: (0, 0, 0)) for m in masks],
        out_specs=pl.BlockSpec((1, L, w), lambda b, h: (b, 0, h)),
        out_shape=jax.ShapeDtypeStruct((B, L, n_heads * dv), BF16),
        scratch_shapes=[pltpu.VMEM((L, w), F32)] + operand_bufs * 2,
        compiler_params=_cparams(("parallel", "parallel")),
        name="hgrn2_bidirectional",
    )(*([p_lat] * 5), *([p_ctx] * 3), lb, norm_g.reshape(1, dv), *masks)


def _rope_tables(n_tokens):
    rows = n_tokens // GRID_W
    row = jnp.repeat(jnp.arange(rows, dtype=F32), GRID_W)
    colp = jnp.tile(jnp.arange(GRID_W, dtype=F32), rows)
    inv = ROPE_THETA ** (-jnp.arange(AXIS_FREQS, dtype=F32) / AXIS_FREQS)
    ang = jnp.stack([row[:, None] * inv, colp[:, None] * inv], axis=1)
    cos, sin = jnp.cos(ang), jnp.sin(ang)
    zero = jnp.zeros_like(sin)
    full = lambda a, b: jnp.stack([a, b], axis=2).reshape(n_tokens, HEAD_DIM)
    return full(cos, cos), full(-sin, zero), full(zero, sin)


def _identity_rope(n_tokens):
    return (jnp.ones((n_tokens, HEAD_DIM), F32), jnp.zeros((n_tokens, HEAD_DIM), F32),
            jnp.zeros((n_tokens, HEAD_DIM), F32))


def kernel(x, c, ctx, c_ctx, mod_w, mod_b, norm_g, ab_w_in, ab_w_out, attn_q_norm, attn_k_norm,
           s5_lam_re, s5_lam_im, s5_log_dt, s5_b_re, s5_b_im, s5_c_re, s5_c_im, s5_d,
           s5_glu_w, s5_glu_b, c_w_in, c_w_out, hgrn_lb_logits, hgrn_norm,
           ffn_w_up, ffn_conv_w, ffn_conv_b, ffn_w_down):
    B, L, D = x.shape
    Lc = ctx.shape[1]
    depth = mod_w.shape[0]
    qk_w = QK_HEADS * HEAD_DIM
    t = TILES

    lb_all = jnp.cumsum(jax.nn.softmax(hgrn_lb_logits.astype(F32), axis=0), axis=0)
    lb_all = lb_all - lb_all[:1]

    cvec = jnp.concatenate([c, c_ctx[None], jnp.zeros((SUBLANES - (B + 1) % SUBLANES, D), F32)], axis=0)
    mods = _modulation(cvec, mod_w, mod_b)

    rope_lat = _rope_tables(L)
    rope_ctx = _identity_rope(Lc)
    w_up = ffn_w_up.astype(BF16)
    w_down = ffn_w_down.astype(BF16)

    for l in range(depth):
        last = l == depth - 1
        m_lat = [mods[l, :B, k * D:(k + 1) * D][:, None, :] for k in range(6)]
        m_ctx = [mods[l, B:B + 1, k * D:(k + 1) * D][:, None, :] for k in range(6)]
        g = norm_g[l]
        if l % 2 == 0:
            e = l // 2
            w_in = ab_w_in[e].astype(BF16)
            gain = jnp.concatenate([jnp.tile(attn_q_norm[e] * (HEAD_DIM ** -0.5 * math.log2(math.e)), ATTN_HEADS),
                                    jnp.tile(attn_k_norm[e], ATTN_KV_HEADS)]).reshape(1, qk_w)
            qk_lat, v_lat, u_lat = _proj_ab(x, g[0], m_lat[0], m_lat[1], w_in, gain, rope_lat, tm=t.proj_tm)
            qk_ctx, v_ctx, u_ctx = _proj_ab(ctx, g[0], m_ctx[0], m_ctx[1], w_in, gain, rope_ctx, tm=t.proj_tm)
            a_lat = _attention(qk_lat, v_lat, (qk_ctx, v_ctx), tq=t.attn_tq)
            tables = _s5_tables(s5_lam_re[e], s5_lam_im[e], s5_log_dt[e], s5_b_re[e], s5_b_im[e],
                                s5_c_re[e], s5_c_im[e], s5_d[e])
            y = _s5(u_ctx, u_lat, tables)
            glu_w = s5_glu_w[e].astype(BF16)
            w_out = ab_w_out[e].astype(BF16)
            s_lat = _glu(y, Lc // S5_CHUNK, L, glu_w, s5_glu_b[e], tm=t.glu_tm)
            x = _out_res([a_lat, s_lat], w_out, x, m_lat[2], g[1], tm=t.out_tm)
            if not last:
                a_ctx = _attention(qk_ctx, v_ctx, None, tq=t.attn_tq_ctx)
                s_ctx = _glu(y, 0, Lc, glu_w, s5_glu_b[e], tm=t.glu_tm)
                ctx = _out_res([a_ctx, s_ctx], w_out, ctx, m_ctx[2], g[1], tm=t.out_tm)
        else:
            o_idx = l // 2
            w_in = c_w_in[o_idx].astype(BF16)
            n_heads = c_w_out.shape[1] // HGRN_DK
            p_lat = _proj(x, g[0], m_lat[0], m_lat[1], w_in, tm=t.proj_tm, tn=t.proj_tn)
            p_ctx = _proj(ctx, g[0], m_ctx[0], m_ctx[1], w_in, tm=t.proj_tm, tn=t.proj_tn)
            o_lat = _gla(p_lat, p_ctx, lb_all[l], hgrn_norm[o_idx], n_heads)
            x = _out_res([o_lat], c_w_out[o_idx].astype(BF16), x, m_lat[2], g[1], tm=t.out_tm)
            assert last, "context outputs of the HGRN2 mixer are only needed by a following layer"
        x = _conv_ffn(x, g[2], m_lat[3], m_lat[4], w_up, ffn_conv_w[l], ffn_conv_b[l], w_down,
                      m_lat[5], g[3], layer=l, tm=t.ffn_tm, tf=t.ffn_tf)
        if not last:
            ctx = _conv_ffn(ctx, g[2], m_ctx[3], m_ctx[4], w_up, ffn_conv_w[l], ffn_conv_b[l], w_down,
                            m_ctx[5], g[3], layer=l, tm=t.ffn_tm, tf=t.ffn_tf)
    return x
```

```python
import functools
import math
from typing import NamedTuple

import jax
import jax.numpy as jnp
import numpy as np
from jax import lax
from jax.experimental import pallas as pl
from jax.experimental.pallas import tpu as pltpu

F32 = jnp.float32
BF16 = jnp.bfloat16
EPS = 1e-6

LANES = 128
SUBLANES = 8
BF16_ROWS = 16

HEAD_DIM = 128
ATTN_HEADS = 8
ATTN_KV_HEADS = 2
ATTN_GROUP = ATTN_HEADS // ATTN_KV_HEADS
GRID_W = 64
ROPE_THETA = 10000.0
AXIS_FREQS = HEAD_DIM // 4

S5_GROUP = 16
S5_STATE = 64
S5_CHUNK = 16

HGRN_DK = 128
HGRN_CHUNK = 64

VMEM_LIMIT = 56 * 1024 * 1024


class _Tiles(NamedTuple):
    proj_tm: int = 1024
    proj_tn: int = 2048
    attn_tq: int = 512
    attn_tq_ctx: int = 256
    glu_tm: int = 256
    out_tm: int = 512
    ffn_tm: int = 512
    ffn_tf: int = 512


TILES = _Tiles()


def _cparams(sem):
    return pltpu.CompilerParams(dimension_semantics=sem, vmem_limit_bytes=VMEM_LIMIT)


def _vec_map(vec, nd_grid):
    shared = vec.shape[0] == 1
    if nd_grid == 2:
        return lambda g, i: (0 if shared else g, 0, 0)
    return lambda g, i, j: (0 if shared else g, 0, 0)


def _rms(x):
    return x * lax.rsqrt(jnp.mean(x * x, axis=-1, keepdims=True) + EPS)


def _mod_kernel(s_ref, w_ref, b_ref, o_ref):
    s = s_ref[...]
    a = s * jax.nn.sigmoid(s)
    o_ref[0] = jnp.dot(a.astype(BF16), w_ref[0].astype(BF16), preferred_element_type=F32) + b_ref[0]


def _modulation(cvec, mod_w, mod_b):
    depth, d, n = mod_w.shape
    rows = cvec.shape[0]
    tn = 1024
    return pl.pallas_call(
        _mod_kernel,
        grid=(depth, n // tn),
        in_specs=[
            pl.BlockSpec((rows, d), lambda l, j: (0, 0)),
            pl.BlockSpec((1, d, tn), lambda l, j: (l, 0, j)),
            pl.BlockSpec((1, 1, tn), lambda l, j: (l, 0, j)),
        ],
        out_specs=pl.BlockSpec((1, rows, tn), lambda l, j: (l, 0, j)),
        out_shape=jax.ShapeDtypeStruct((depth, rows, n), F32),
        compiler_params=_cparams(("parallel", "parallel")),
        name="modulation",
    )(cvec, mod_w, mod_b.reshape(depth, 1, n))


NORM_ROWS = 32


def _row_blocks(n_rows):
    step = NORM_ROWS if n_rows % NORM_ROWS == 0 else n_rows
    return [slice(r, r + step) for r in range(0, n_rows, step)]


def _norm_mod_to(h_ref, x_ref, g_ref, sh_ref, sc_ref):
    gain = g_ref[...] * (1.0 + sc_ref[0])
    for rows in _row_blocks(h_ref.shape[0]):
        h_ref[rows] = (_rms(x_ref[0, rows]) * gain + sh_ref[0]).astype(BF16)


def _proj_kernel(x_ref, g_ref, sh_ref, sc_ref, w_ref, o_ref, h_ref):
    @pl.when(pl.program_id(2) == 0)
    def _():
        _norm_mod_to(h_ref, x_ref, g_ref, sh_ref, sc_ref)

    o_ref[0] = jnp.dot(h_ref[...], w_ref[...], preferred_element_type=F32).astype(o_ref.dtype)


def _proj_specs(x, g, shift, scale, w, tm, tn):
    G, T, D = x.shape
    in_specs = [
        pl.BlockSpec((1, tm, D), lambda b, i, j: (b, i, 0)),
        pl.BlockSpec((1, D), lambda b, i, j: (0, 0)),
        pl.BlockSpec((1, 1, D), _vec_map(shift, 3)),
        pl.BlockSpec((1, 1, D), _vec_map(scale, 3)),
        pl.BlockSpec((D, tn), lambda b, i, j: (0, j)),
    ]
    return in_specs, [x, g.reshape(1, D), shift, scale, w]


def _proj(x, g, shift, scale, w, *, tm, tn, out_dtype=BF16):
    G, T, D = x.shape
    N = w.shape[1]
    tm = min(tm, T)
    in_specs, args = _proj_specs(x, g, shift, scale, w, tm, tn)
    return pl.pallas_call(
        _proj_kernel,
        grid=(G, T // tm, N // tn),
        in_specs=in_specs,
        out_specs=pl.BlockSpec((1, tm, tn), lambda b, i, j: (b, i, j)),
        out_shape=jax.ShapeDtypeStruct((G, T, N), out_dtype),
        scratch_shapes=[pltpu.VMEM((tm, D), BF16)],
        compiler_params=_cparams(("parallel", "parallel", "arbitrary")),
        name="norm_mod_proj",
    )(*args)


QK_HEADS = ATTN_HEADS + ATTN_KV_HEADS


def _proj_ab_kernel(x_ref, g_ref, sh_ref, sc_ref, w_ref, gain_ref, cos_ref, s1_ref, s2_ref,
                    qk_ref, v_ref, u_ref, h_ref):
    j = pl.program_id(2)

    @pl.when(j == 0)
    def _():
        _norm_mod_to(h_ref, x_ref, g_ref, sh_ref, sc_ref)

    acc = jnp.dot(h_ref[...], w_ref[...], preferred_element_type=F32)

    def head(hh):
        return slice(hh * HEAD_DIM, (hh + 1) * HEAD_DIM)

    @pl.when(j == 0)
    def _():
        for hh in range(QK_HEADS):
            y = _rms(acc[:, head(hh)]) * gain_ref[:, head(hh)]
            y = (y * cos_ref[...] + pltpu.roll(y, HEAD_DIM - AXIS_FREQS, 1) * s1_ref[...]
                 + pltpu.roll(y, AXIS_FREQS, 1) * s2_ref[...])
            qk_ref[0, :, head(hh)] = y.astype(qk_ref.dtype)

    @pl.when(j == 1)
    def _():
        v_ref[0] = acc[:, :ATTN_KV_HEADS * HEAD_DIM].astype(v_ref.dtype)
        slabs, chunks = u_ref.shape[:2]
        for s in range(slabs):
            u_ref[s, :, 0] = acc[:, head(ATTN_KV_HEADS + s)].astype(u_ref.dtype).reshape(chunks, S5_CHUNK, LANES)


def _proj_ab(x, g, shift, scale, w, gain, rope, *, tm):
    G, T, D = x.shape
    tm = min(tm, T)
    tn = QK_HEADS * HEAD_DIM
    assert w.shape[1] == 2 * tn
    v_w = ATTN_KV_HEADS * HEAD_DIM
    slabs = (tn - v_w) // LANES
    in_specs, args = _proj_specs(x, g, shift, scale, w, tm, tn)
    in_specs += [pl.BlockSpec((1, tn), lambda b, i, j: (0, 0))]
    in_specs += [pl.BlockSpec((tm, HEAD_DIM), lambda b, i, j: (i, 0))] * 3
    return pl.pallas_call(
        _proj_ab_kernel,
        grid=(G, T // tm, 2),
        in_specs=in_specs,
        out_specs=[
            pl.BlockSpec((1, tm, tn), lambda b, i, j: (b, i, 0)),
            pl.BlockSpec((1, tm, v_w), lambda b, i, j: (b, i, 0)),
            pl.BlockSpec((slabs, tm // S5_CHUNK, 1, S5_CHUNK, LANES), lambda b, i, j: (0, i, b, 0, 0)),
        ],
        out_shape=[
            jax.ShapeDtypeStruct((G, T, tn), BF16),
            jax.ShapeDtypeStruct((G, T, v_w), BF16),
            jax.ShapeDtypeStruct((slabs, T // S5_CHUNK, G, S5_CHUNK, LANES), BF16),
        ],
        scratch_shapes=[pltpu.VMEM((tm, D), BF16)],
        compiler_params=_cparams(("parallel", "parallel", "arbitrary")),
        name="norm_mod_proj_ab",
    )(*args, gain, *rope)


ATTN_KEY_CHUNK = 1024


def _attn_kernel(q_ref, k_ref, v_ref, *rest, tq, has_prefix):
    if has_prefix:
        kp_ref, vp_ref, o_ref = rest
    else:
        (o_ref,) = rest
    nt = (((1,), (1,)), ((), ()))
    q = jnp.concatenate(
        [q_ref[0, :, g * HEAD_DIM:(g + 1) * HEAD_DIM] for g in range(ATTN_GROUP)], axis=0)
    n_keys = k_ref.shape[1]
    kc = min(ATTN_KEY_CHUNK, n_keys)
    chunks = [(kp_ref, vp_ref, 0, kp_ref.shape[1])] if has_prefix else []
    chunks += [(k_ref, v_ref, c * kc, kc) for c in range(n_keys // kc)]

    def scores(chunk):
        kr, _, start, size = chunk
        return lax.dot_general(q, kr[0, start:start + size, :], nt, preferred_element_type=F32)

    m = jnp.full((q.shape[0], 1), -1e30, F32)
    l = jnp.zeros((q.shape[0], 1), F32)
    o = jnp.zeros((q.shape[0], HEAD_DIM), F32)
    s_next = scores(chunks[0])
    for ci, (_, vr, start, size) in enumerate(chunks):
        s = s_next
        if ci + 1 < len(chunks):
            s_next = scores(chunks[ci + 1])
        m_new = jnp.maximum(m, jnp.max(s, axis=-1, keepdims=True))
        alpha = jnp.exp2(m - m_new)
        p = jnp.exp2(s - m_new)
        l = alpha * l + jnp.sum(p, axis=-1, keepdims=True)
        o = alpha * o + jnp.dot(p.astype(BF16), vr[0, start:start + size, :], preferred_element_type=F32)
        m = m_new
    o = o * (1.0 / l)
    for g in range(ATTN_GROUP):
        o_ref[0, :, g * HEAD_DIM:(g + 1) * HEAD_DIM] = o[g * tq:(g + 1) * tq].astype(o_ref.dtype)


def _attention(qk, v, prefix, *, tq):
    B, L, _ = qk.shape
    tq = min(tq, L)
    gw = ATTN_GROUP * HEAD_DIM
    k_blk = ATTN_HEADS
    in_specs = [
        pl.BlockSpec((1, tq, gw), lambda b, h, i: (b, i, h)),
        pl.BlockSpec((1, L, HEAD_DIM), lambda b, h, i: (b, 0, k_blk + h)),
        pl.BlockSpec((1, L, HEAD_DIM), lambda b, h, i: (b, 0, h)),
    ]
    args = [qk, qk, v]
    if prefix is not None:
        Lp = prefix[0].shape[1]
        in_specs += [
            pl.BlockSpec((1, Lp, HEAD_DIM), lambda b, h, i: (b, 0, k_blk + h)),
            pl.BlockSpec((1, Lp, HEAD_DIM), lambda b, h, i: (b, 0, h)),
        ]
        args += list(prefix)
    return pl.pallas_call(
        functools.partial(_attn_kernel, tq=tq, has_prefix=prefix is not None),
        grid=(B, ATTN_KV_HEADS, L // tq),
        in_specs=in_specs,
        out_specs=pl.BlockSpec((1, tq, gw), lambda b, h, i: (b, i, h)),
        out_shape=jax.ShapeDtypeStruct((B, L, ATTN_HEADS * HEAD_DIM), BF16),
        compiler_params=_cparams(("parallel", "parallel", "arbitrary")),
        name="gqa_attention",
    )(*args)


S5_SLAB_GROUPS = LANES // S5_GROUP
S5_KD = S5_CHUNK * LANES
S5_PAIR_COLS = 4 * LANES


def _s5_end_kernel(u_ref, w_ref, e_ref):
    e_ref[...] = jnp.dot(u_ref[0], w_ref[0], preferred_element_type=F32)


def _s5_scan_kernel(e_ref, cf_ref, cb_ref, h_ref, *, batch, nc_ctx, nc_all):
    cps = SUBLANES // batch
    g_ctx, g_all = nc_ctx // cps, nc_all // cps

    def advance(hr, hi, er, ei, c_ref):
        ar, ai = c_ref[0:1, :], c_ref[1:2, :]
        return ar * hr - ai * hi + er, ar * hi + ai * hr + ei

    def sweep(r, state, col0, c_ref, order):
        e8 = e_ref[pl.ds(r, SUBLANES), col0:col0 + 2 * LANES]
        hr, hi = state
        out_r, out_i = [None] * cps, [None] * cps
        for c in order:
            out_r[c], out_i[c] = hr, hi
            rows = slice(c * batch, (c + 1) * batch)
            hr, hi = advance(hr, hi, e8[rows, :LANES], e8[rows, LANES:], c_ref)
        h_ref[pl.ds(r, SUBLANES), col0:col0 + LANES] = jnp.concatenate(out_r, axis=0).astype(h_ref.dtype)
        h_ref[pl.ds(r, SUBLANES), col0 + LANES:col0 + 2 * LANES] = (
            jnp.concatenate(out_i, axis=0).astype(h_ref.dtype))
        return hr, hi

    def body(i, carry):
        fwd, bwd = carry
        fwd = sweep(pl.multiple_of(i * SUBLANES, SUBLANES), fwd, 0, cf_ref, range(cps))
        gi = jnp.where(i < g_ctx, g_ctx - 1 - i, g_all - 1 - (i - g_ctx))
        bwd = sweep(pl.multiple_of(gi * SUBLANES, SUBLANES), bwd, 2 * LANES, cb_ref, reversed(range(cps)))
        return fwd, bwd

    zero = jnp.zeros((batch, LANES), F32)
    lax.fori_loop(0, g_all, body, ((zero, zero), (zero, zero)))


def _s5_out_kernel(u_ref, us_ref, lag_ref, h_ref, win_ref, d_ref, y_ref, tz_ref):
    T = S5_CHUNK
    t_blocks = tz_ref.shape[1] // LANES
    t0 = pl.program_id(1) * t_blocks
    for tt in range(t_blocks):
        for s in range(T):
            tz_ref[s * LANES:(s + 1) * LANES, tt * LANES:(tt + 1) * LANES] = lag_ref[0, t0 + tt - s + (T - 1)]
    y = jnp.dot(u_ref[0], tz_ref[...], preferred_element_type=F32)
    y += jnp.dot(h_ref[...].astype(BF16), win_ref[0], preferred_element_type=F32)
    y += us_ref[0].astype(F32) * d_ref[0]
    y_ref[0] = y.astype(y_ref.dtype)


def _s5_tables(lam_re, lam_im, log_dt, b_re, b_im, c_re, c_im, d_skip):
    T, C, P = S5_CHUNK, S5_GROUP, S5_STATE
    G = lam_re.shape[1]
    hi = lax.Precision.HIGHEST
    lam = lax.complex(jnp.minimum(lam_re.astype(F32), -1e-4), lam_im.astype(F32))
    dt = jnp.exp(log_dt.astype(F32))[..., None]
    lam_dt = lam * dt
    lam_bar = jnp.exp(lam_dt)
    bmat = lax.complex(b_re.astype(F32), b_im.astype(F32))
    b_bar = ((lam_bar - 1.0) / lam)[..., None] * bmat
    cmat = lax.complex(c_re.astype(F32), c_im.astype(F32))
    steps = jnp.arange(T + 1, dtype=F32)
    pw = jnp.exp(lam_dt[:, None] * steps[None, :, None, None])

    kern = jnp.real(jnp.einsum('zgcp,zjgp,zgpd->zjgcd', cmat, pw[:, :T], b_bar, precision=hi))

    def end_w(p_sel, bb):
        w = p_sel[:, :, None, :] * bb.transpose(0, 2, 1)[None]
        return [jnp.real(w), jnp.imag(w)]

    w_end = jnp.concatenate(end_w(pw[0, :T][::-1], b_bar[0]) + end_w(pw[1, :T], b_bar[1]), axis=-1)

    def in_w(p_sel, cm):
        z = cm.transpose(0, 2, 1)[:, :, None, :] * p_sel.transpose(1, 2, 0)[:, :, :, None]
        return [jnp.real(z), -jnp.imag(z)]

    w_in = jnp.stack(in_w(pw[0, 1:], cmat[0]) + in_w(pw[1, 1:][::-1], cmat[1]), axis=1)

    def coef(z):
        return jnp.stack([jnp.real(z).reshape(G * P), jnp.imag(z).reshape(G * P)])

    sg = S5_SLAB_GROUPS
    ngb, npair = G // sg, sg // 2
    grp = np.arange(sg)
    tok_cols_grp = np.tile(np.repeat(grp, C), T)
    state_cols_grp = (2 * np.arange(npair)[:, None, None, None] + np.arange(2)[None, None, :, None]
                      + np.zeros((1, 4, 1, P), np.int64)).reshape(-1)
    rows_gd = np.repeat(grp, C)

    lag_k = jnp.concatenate([kern[1][1:][::-1], (kern[0][0] + kern[1][0])[None], kern[0][1:]], axis=0)
    lag_src = lag_k.reshape(2 * T - 1, ngb, sg, C, C).transpose(1, 0, 2, 4, 3).reshape(ngb * (2 * T - 1), LANES, C)
    rep_c = np.tile(np.eye(C, dtype=np.float32), (1, sg))
    mask_c = (rows_gd[:, None] == np.repeat(grp, C)[None, :]).astype(np.float32)[None]
    lag_tiles = _spread(lag_src, rep_c, mask_c, lambda n: 0).reshape(ngb, 2 * T - 1, LANES, LANES)

    end_src = w_end.reshape(T, ngb, sg * C, 4 * P).transpose(1, 0, 2, 3).reshape(ngb * T, LANES, 4 * P)
    rep_state = np.tile(np.eye(4 * P, dtype=np.float32).reshape(4 * P, 1, 4, 1, P), (1, npair, 1, 2, 1))
    rep_state = rep_state.reshape(4 * P, npair * S5_PAIR_COLS)
    mask_state = (rows_gd[:, None] == state_cols_grp[None, :]).astype(np.float32)[None]
    end_slab = _spread(end_src, rep_state, mask_state, lambda n: 0).reshape(ngb, S5_KD, npair * S5_PAIR_COLS)

    in_src = w_in.reshape(ngb, npair, 2, 4, P, T * C).transpose(0, 1, 3, 2, 4, 5)
    in_src = in_src.reshape(ngb * npair * 4, 2 * P, T * C)
    rep_tok = np.tile(np.eye(T * C, dtype=np.float32).reshape(T * C, T, 1, C), (1, 1, sg, 1)).reshape(T * C, S5_KD)
    row_grp = 2 * np.arange(npair)[:, None, None] + np.repeat(np.arange(2), P)[None, :, None]
    mask_tok = (row_grp == tok_cols_grp[None, None, :]).astype(np.float32)
    in_slab = _spread(in_src, rep_tok, mask_tok, lambda n: (n // 4) % npair)
    in_slab = in_slab.reshape(ngb, npair * S5_PAIR_COLS, S5_KD)

    d_slab = jnp.tile(d_skip.astype(F32).reshape(ngb, 1, LANES), (1, 1, T))
    return lag_tiles, end_slab, in_slab, coef(pw[0, T]), coef(pw[1, T]), d_slab


SPREAD_TILES = 4


def _spread_kernel(src_ref, rep_ref, mask_ref, o_ref):
    for i in range(SPREAD_TILES):
        o_ref[i] = (jnp.dot(src_ref[i].astype(BF16), rep_ref[...], preferred_element_type=F32)
                    * mask_ref[0]).astype(o_ref.dtype)


def _spread(src, rep, mask, mask_index):
    n, rows, k = src.shape
    width = rep.shape[1]
    st = SPREAD_TILES
    assert n % st == 0
    return pl.pallas_call(
        _spread_kernel,
        grid=(n // st,),
        in_specs=[
            pl.BlockSpec((st, rows, k), lambda i: (i, 0, 0)),
            pl.BlockSpec((k, width), lambda i: (0, 0)),
            pl.BlockSpec((1, rows, width), lambda i: (mask_index(i * st), 0, 0)),
        ],
        out_specs=pl.BlockSpec((st, rows, width), lambda i: (i, 0, 0)),
        out_shape=jax.ShapeDtypeStruct((n, rows, width), BF16),
        compiler_params=_cparams(("parallel",)),
        name="s5_spread_operator",
    )(src, jnp.asarray(rep, BF16), jnp.asarray(mask, F32))


def _s5(u_ctx, u_lat, tables):
    tz, w_end, w_in, cf, cb, d_slab = tables
    ngb, nc_ctx, B = u_ctx.shape[:3]
    nc_all = nc_ctx + u_lat.shape[1]
    cps = SUBLANES // B
    assert SUBLANES % B == 0 and nc_ctx % cps == 0 and (nc_all - nc_ctx) % cps == 0
    M = nc_all * B
    kd, sw = S5_KD, w_end.shape[-1]
    n_half = 4
    hw = kd // n_half
    u = jnp.concatenate([u_ctx, u_lat], axis=1).reshape(ngb, M, kd)

    e = pl.pallas_call(
        _s5_end_kernel,
        grid=(ngb, n_half),
        in_specs=[
            pl.BlockSpec((1, M, kd), lambda g, j: (g, 0, 0)),
            pl.BlockSpec((1, kd, sw // n_half), lambda g, j: (g, 0, j)),
        ],
        out_specs=pl.BlockSpec((M, sw // n_half), lambda g, j: (0, g * n_half + j)),
        out_shape=jax.ShapeDtypeStruct((M, ngb * sw), F32),
        compiler_params=_cparams(("parallel", "parallel")),
        name="s5_chunk_end_states",
    )(u, w_end)

    h = pl.pallas_call(
        functools.partial(_s5_scan_kernel, batch=B, nc_ctx=nc_ctx, nc_all=nc_all),
        grid=(ngb * sw // S5_PAIR_COLS,),
        in_specs=[pl.BlockSpec((M, S5_PAIR_COLS), lambda j: (0, j))]
        + [pl.BlockSpec((2, LANES), lambda j: (0, j))] * 2,
        out_specs=pl.BlockSpec((M, S5_PAIR_COLS), lambda j: (0, j)),
        out_shape=jax.ShapeDtypeStruct((M, ngb * sw), F32),
        compiler_params=_cparams(("parallel",)),
        name="s5_chunk_scan",
    )(e, cf, cb)

    y = pl.pallas_call(
        _s5_out_kernel,
        grid=(ngb, n_half),
        in_specs=[
            pl.BlockSpec((1, M, kd), lambda g, j: (g, 0, 0)),
            pl.BlockSpec((1, M, hw), lambda g, j: (g, 0, j)),
            pl.BlockSpec((1,) + tz.shape[1:], lambda g, j: (g, 0, 0, 0)),
            pl.BlockSpec((M, sw), lambda g, j: (0, g)),
            pl.BlockSpec((1, sw, hw), lambda g, j: (g, 0, j)),
            pl.BlockSpec((1, 1, hw), lambda g, j: (g, 0, j)),
        ],
        out_specs=pl.BlockSpec((1, M, hw), lambda g, j: (g, 0, j)),
        out_shape=jax.ShapeDtypeStruct((ngb, M, kd), BF16),
        scratch_shapes=[pltpu.VMEM((kd, hw), BF16)],
        compiler_params=_cparams(("parallel", "parallel")),
        name="s5_chunk_outputs",
    )(u, u, tz, h, w_in, d_slab)

    return y.reshape(ngb, nc_all, B, S5_CHUNK, LANES)


def _glu_kernel(y_ref, w_ref, b_ref, o_ref):
    slabs, chunks = y_ref.shape[:2]
    y = jnp.concatenate([y_ref[s, :, 0].reshape(chunks * S5_CHUNK, LANES) for s in range(slabs)], axis=1)
    z = jnp.dot(jax.nn.gelu(y.astype(F32)).astype(BF16), w_ref[...], preferred_element_type=F32) + b_ref[...]
    n = o_ref.shape[-1]
    o_ref[0] = (z[:, :n] * jax.nn.sigmoid(z[:, n:])).astype(o_ref.dtype)


def _glu(y, first_chunk, n_tokens, w, b, *, tm):
    slabs, _, B = y.shape[:3]
    W = slabs * LANES
    tm = min(tm, n_tokens)
    tc = tm // S5_CHUNK
    assert first_chunk % tc == 0
    off = first_chunk // tc
    return pl.pallas_call(
        _glu_kernel,
        grid=(B, n_tokens // tm),
        in_specs=[
            pl.BlockSpec((slabs, tc, 1, S5_CHUNK, LANES), lambda g, i: (0, off + i, g, 0, 0)),
            pl.BlockSpec((W, 2 * W), lambda g, i: (0, 0)),
            pl.BlockSpec((1, 2 * W), lambda g, i: (0, 0)),
        ],
        out_specs=pl.BlockSpec((1, tm, W), lambda g, i: (g, i, 0)),
        out_shape=jax.ShapeDtypeStruct((B, n_tokens, W), BF16),
        compiler_params=_cparams(("parallel", "parallel")),
        name="s5_glu",
    )(y, w, b.reshape(1, 2 * W))


def _out_res_kernel(*refs, widths):
    n = len(widths)
    lhs = refs[:n]
    w_ref, x_ref, gate_ref, g_ref, o_ref = refs[n:]
    y = None
    off = 0
    for r, k in zip(lhs, widths):
        part = jnp.dot(r[0], w_ref[off:off + k, :], preferred_element_type=F32)
        y = part if y is None else y + part
        off += k
    gain = gate_ref[0] * g_ref[...]
    for rows in _row_blocks(y.shape[0]):
        o_ref[0, rows] = x_ref[0, rows] + _rms(y[rows]) * gain


def _out_res(parts, w, x, gate, g, *, tm):
    G, T, D = x.shape
    tm = min(tm, T)
    widths = tuple(p.shape[-1] for p in parts)
    K = sum(widths)
    in_specs = [pl.BlockSpec((1, tm, k), lambda b, i: (b, i, 0)) for k in widths]
    in_specs += [
        pl.BlockSpec((K, D), lambda b, i: (0, 0)),
        pl.BlockSpec((1, tm, D), lambda b, i: (b, i, 0)),
        pl.BlockSpec((1, 1, D), _vec_map(gate, 2)),
        pl.BlockSpec((1, D), lambda b, i: (0, 0)),
    ]
    return pl.pallas_call(
        functools.partial(_out_res_kernel, widths=widths),
        grid=(G, T // tm),
        in_specs=in_specs,
        out_specs=pl.BlockSpec((1, tm, D), lambda b, i: (b, i, 0)),
        out_shape=jax.ShapeDtypeStruct((G, T, D), F32),
        compiler_params=_cparams(("parallel", "parallel")),
        name="out_proj_residual",
    )(*parts, w, x, gate, g.reshape(1, D))


HALO = BF16_ROWS


def _ffn_kernel(x_ref, xp_ref, xn_ref, g2_ref, sh_ref, sc_ref, wa_ref, wv_ref, cwa_ref, cwv_ref,
                cba_ref, cbv_ref, wd_ref, gate_ref, g3_ref, o_ref, h_ref, acc_ref, ua_ref, uv_ref, *, tm):
    i = pl.program_id(1)
    f = pl.program_id(2)

    def normmod(x):
        return _rms(x) * (g2_ref[...] * (1.0 + sc_ref[0])) + sh_ref[0]

    @pl.when(f == 0)
    def _():
        d = x_ref.shape[-1]
        hp = jnp.where(i == 0, 0.0, normmod(xp_ref[0]))[SUBLANES - 1:SUBLANES]
        hn = jnp.where(i == pl.num_programs(1) - 1, 0.0, normmod(xn_ref[0]))[0:1]
        row = lax.broadcasted_iota(jnp.int32, (HALO, d), 0)
        halo = jnp.where(row == 0, hn, jnp.where(row == HALO - 1, hp, 0.0))
        for rows in _row_blocks(tm):
            h_ref[rows] = normmod(x_ref[0, rows]).astype(BF16)
        h_ref[tm:] = halo.astype(BF16)
        acc_ref[...] = jnp.zeros_like(acc_ref)

    h = h_ref[...]
    ua_ref[...] = jnp.dot(h, wa_ref[...], preferred_element_type=F32)
    uv_ref[...] = jnp.dot(h, wv_ref[...], preferred_element_type=F32)

    def conv(u_ref, cw_ref, cb_ref):
        prev = pltpu.roll(u_ref[...], 1, 0)[0:tm]
        return (prev * cw_ref[0:1, :] + u_ref[0:tm] * cw_ref[1:2, :]
                + u_ref[1:tm + 1] * cw_ref[2:3, :] + cb_ref[...])

    a = conv(ua_ref, cwa_ref, cba_ref)
    v = conv(uv_ref, cwv_ref, cbv_ref)
    gated = (a * jax.nn.sigmoid(a) * v).astype(BF16)
    acc_ref[...] += jnp.dot(gated, wd_ref[...], preferred_element_type=F32)

    @pl.when(f == pl.num_programs(2) - 1)
    def _():
        gain = gate_ref[0] * g3_ref[...]
        for rows in _row_blocks(tm):
            o_ref[0, rows] = x_ref[0, rows] + _rms(acc_ref[rows]) * gain


def _conv_ffn(x, g2, shift, scale, w_up, conv_w, conv_b, w_down, gate, g3, *, layer, tm, tf):
    G, T, D = x.shape
    F = w_down.shape[1]
    tm = min(tm, T)
    nf = F // tf
    n_i = T // tm
    rb = tm // SUBLANES
    last_rb = T // SUBLANES - 1
    conv_b = conv_b.reshape(1, 2 * F)
    return pl.pallas_call(
        functools.partial(_ffn_kernel, tm=tm),
        grid=(G, n_i, nf),
        in_specs=[
            pl.BlockSpec((1, tm, D), lambda b, i, f: (b, i, 0)),
            pl.BlockSpec((1, SUBLANES, D), lambda b, i, f: (b, jnp.maximum(i * rb - 1, 0), 0)),
            pl.BlockSpec((1, SUBLANES, D), lambda b, i, f: (b, jnp.minimum((i + 1) * rb, last_rb), 0)),
            pl.BlockSpec((1, D), lambda b, i, f: (0, 0)),
            pl.BlockSpec((1, 1, D), _vec_map(shift, 3)),
            pl.BlockSpec((1, 1, D), _vec_map(scale, 3)),
            pl.BlockSpec((None, D, tf), lambda b, i, f: (layer, 0, f)),
            pl.BlockSpec((None, D, tf), lambda b, i, f: (layer, 0, nf + f)),
            pl.BlockSpec((3, tf), lambda b, i, f: (0, f)),
            pl.BlockSpec((3, tf), lambda b, i, f: (0, nf + f)),
            pl.BlockSpec((1, tf), lambda b, i, f: (0, f)),
            pl.BlockSpec((1, tf), lambda b, i, f: (0, nf + f)),
            pl.BlockSpec((None, tf, D), lambda b, i, f: (layer, f, 0)),
            pl.BlockSpec((1, 1, D), _vec_map(gate, 3)),
            pl.BlockSpec((1, D), lambda b, i, f: (0, 0)),
        ],
        out_specs=pl.BlockSpec((1, tm, D), lambda b, i, f: (b, i, 0)),
        out_shape=jax.ShapeDtypeStruct((G, T, D), F32),
        scratch_shapes=[
            pltpu.VMEM((tm + HALO, D), BF16),
            pltpu.VMEM((tm, D), F32),
            pltpu.VMEM((tm + HALO, tf), F32),
            pltpu.VMEM((tm + HALO, tf), F32),
        ],
        compiler_params=_cparams(("parallel", "parallel", "arbitrary")),
        name="conv_ffn",
    )(x, x, x, g2.reshape(1, D), shift, scale, w_up, w_up, conv_w, conv_w, conv_b, conv_b, w_down,
      gate, g3.reshape(1, D))


GLA_BLOCKS = HGRN_CHUNK // SUBLANES


def _gla_operands(z, v, q, lb, rev):
    nb, rb = GLA_BLOCKS, SUBLANES
    order = list(range(nb))[::-1] if rev else list(range(nb))
    scan_of = {b: j for j, b in enumerate(order)}
    pos = lax.broadcasted_iota(jnp.int32, (rb, HGRN_DK), 0)
    if rev:
        pos = rb - 1 - pos

    def prev(x, k):
        return pltpu.roll(x, (rb - k) if rev else k, 0)

    def nxt(x, k):
        return pltpu.roll(x, k if rev else (rb - k), 0)

    def blocks(x):
        return [x[rb * b:rb * (b + 1)] for b in range(nb)]

    def rows(bl):
        return jnp.concatenate(bl, axis=0)

    f = lb + (1.0 - lb) * jax.nn.sigmoid(z)
    fb = blocks(f)
    kb = blocks(1.0 - f)

    hb = rb // 2
    hpos = pos & (hb - 1)
    lo_last = hb if rev else hb - 1
    hi_first = hb - 1 if rev else hb
    p4, s4, p8, s8, bt = [], [], [], [], []
    for b in range(nb):
        x = y = fb[b]
        for k in (1, 2):
            x = x * jnp.where(hpos >= k, prev(x, k), 1.0)
            y = y * jnp.where(hpos <= hb - 1 - k, nxt(y, k), 1.0)
        lo_tot = x[lo_last:lo_last + 1, :]
        hi_tot = y[hi_first:hi_first + 1, :]
        sfx = jnp.where(hpos <= hb - 2, nxt(y, 1), 1.0)
        p4.append(x)
        s4.append(sfx)
        p8.append(x * jnp.where(pos >= hb, lo_tot, 1.0))
        s8.append(sfx * jnp.where(pos < hb, hi_tot, 1.0))
        bt.append(lo_tot * hi_tot)
    bts = [bt[order[j]] for j in range(nb)]
    ones = jnp.ones_like(bts[0])
    before = [ones]
    for j in range(1, nb):
        before.append(before[j - 1] * bts[j - 1])
    after = [ones] * nb
    for j in range(nb - 2, -1, -1):
        after[j] = after[j + 1] * bts[j + 1]
    ftot = before[nb - 1] * bts[nb - 1]

    kbase = [kb[b] * s8[b] for b in range(nb)]
    ops = {"v": v, "ftot": ftot,
           "kdec": rows([kbase[b] * after[scan_of[b]] for b in range(nb)]).astype(BF16)}
    if q is None:
        return ops

    qb = blocks(q)
    q8 = [qb[b] * p8[b] for b in range(nb)]
    ops["q_state"] = rows([q8[b] * before[scan_of[b]] for b in range(nb)]).astype(BF16)
    ops["q_far"] = rows(q8).astype(BF16)
    ops["q_near"] = q.astype(BF16)

    vb = blocks(v.astype(F32))
    kcols, vcols = [], []
    chain = {}
    for d in range(1, nb):
        for j in range(nb - d):
            chain[j] = kbase[order[j]] if d == 1 else chain[j] * bts[j + d - 1]
            kcols.append(chain[j])
            vcols.append(vb[order[j]])
    ops["k_far"] = rows(kcols).astype(BF16)
    ops["v_far"] = rows(vcols).astype(BF16)

    ops["q_cross"] = rows([qb[b] * p4[b] for b in range(nb)]).astype(BF16)
    ops["k_cross"] = rows([kb[b] * s4[b] for b in range(nb)]).astype(BF16)

    kvar = []
    cur = kb
    for d in range(hb):
        if d:
            cur = [cur[b] * nxt(fb[b], d) for b in range(nb)]
        kvar.append(rows(cur).astype(BF16))
    ops["k_near"] = jnp.concatenate(kvar, axis=0)
    return ops


def _gla_first_dots(ops, st):
    nt = (((1,), (1,)), ((), ()))
    st_new = st * ops["ftot"] + lax.dot_general(ops["v"], ops["kdec"], (((0,), (0,)), ((), ())),
                                                preferred_element_type=F32)
    if "q_near" not in ops:
        return st_new, None
    o = lax.dot_general(ops["q_state"], st.astype(BF16), nt, preferred_element_type=F32)
    far = lax.dot_general(ops["q_far"], ops["k_far"], nt, preferred_element_type=F32)
    cross = lax.dot_general(ops["q_cross"], ops["k_cross"], nt, preferred_element_type=F32)
    near = lax.dot_general(ops["q_near"], ops["k_near"], nt, preferred_element_type=F32)
    return st_new, (o, far, cross, near)


def _gla_second_dots(ops, scores, masks):
    o, far, cross, near = scores
    m_far, m_cross, m_near = masks
    o = o + jnp.dot(far.astype(BF16) * m_far, ops["v_far"], preferred_element_type=F32)
    o = o + jnp.dot(cross.astype(BF16) * m_cross, ops["v"], preferred_element_type=F32)
    v_near = jnp.concatenate([ops["v"]] * (near.shape[1] // HGRN_CHUNK), axis=0)
    return o + jnp.dot(near.astype(BF16) * m_near, v_near, preferred_element_type=F32)


def _operand_rows():
    T, nb, rb = HGRN_CHUNK, GLA_BLOCKS, SUBLANES
    far = rb * nb * (nb - 1) // 2
    sizes = [("kdec", T), ("q_state", T), ("q_far", T), ("q_near", T), ("q_cross", T), ("k_cross", T),
             ("v", T), ("k_far", far), ("v_far", far), ("k_near", (rb // 2) * T)]
    rows, off = {}, 0
    for name, n in sizes:
        rows[name] = (off, n)
        off += n
    return rows, off


GLA_OPERAND_ROWS, GLA_OPERAND_TOTAL = _operand_rows()


def _gla_mask_tables():
    T, nb, rb = HGRN_CHUNK, GLA_BLOCKS, SUBLANES
    hb = rb // 2
    t = np.arange(T)
    far, cross, near = [], [], []
    for rev in (False, True):
        sblk = (nb - 1 - t // rb) if rev else t // rb
        spos = (rb - 1 - t % rb) if rev else t % rb
        same_block = t[:, None] // rb == t[None, :] // rb
        cols = [(d, j) for d in range(1, nb) for j in range(nb - d)]
        m_far = np.zeros((T, len(cols) * rb), np.float32)
        for c, (d, j) in enumerate(cols):
            m_far[sblk == j + d, c * rb:(c + 1) * rb] = 1.0
        m_cross = same_block & (spos[:, None] >= hb) & (spos[None, :] < hb)
        same_half = same_block & (spos[:, None] // hb == spos[None, :] // hb)
        m_near = np.zeros((T, hb * T), np.float32)
        for d in range(hb):
            m_near[:, d * T:(d + 1) * T] = same_half & (spos[:, None] - spos[None, :] == d)
        far.append(m_far)
        cross.append(m_cross.astype(np.float32))
        near.append(m_near)
    return tuple(jnp.asarray(np.stack(m), BF16) for m in (far, cross, near))


def _gla_kernel(q_ref, zf_ref, zb_ref, v_ref, g_ref, czf_ref, czb_ref, cv_ref, lb_ref, ng_ref,
                mfar_ref, mcross_ref, mnear_ref, o_ref, oacc_ref, ops_a_ref, ft_a_ref, ops_b_ref, ft_b_ref,
                *, n_lat, n_ctx, hp):
    T, dk = HGRN_CHUNK, HGRN_DK
    jobs = [(h, rev) for h in range(hp) for rev in (False, True)]
    zero_state = jnp.zeros((dk, dk), F32)

    def cols(h):
        return slice(h * dk, (h + 1) * dk)

    def start(rev, i, n):
        r = ((n - 1 - i) if rev else i) * T
        return r if isinstance(r, int) else pl.multiple_of(r, T)

    def lb_row(h, rev):
        return lb_ref[int(rev):int(rev) + 1, cols(h)]

    def prep_ctx(i):
        ops = []
        for h, rev in jobs:
            r = start(rev, i, n_ctx)
            z = (czb_ref if rev else czf_ref)[0, pl.ds(r, T), cols(h)].astype(F32)
            ops.append(_gla_operands(z, cv_ref[0, pl.ds(r, T), cols(h)], None, lb_row(h, rev), rev))
        return tuple(ops)

    def prep_lat(i):
        ops = []
        for h, rev in jobs:
            r = start(rev, i, n_lat)
            qz = q_ref[0, pl.ds(r, T), cols(h)].astype(F32)
            z = (zb_ref if rev else zf_ref)[0, pl.ds(r, T), cols(h)].astype(F32)
            ops.append(_gla_operands(z, v_ref[0, pl.ds(r, T), cols(h)], qz * jax.nn.sigmoid(qz),
                                     lb_row(h, rev), rev))
        return tuple(ops)

    def stash(ops, buf):
        ops_ref, ft_ref = buf
        for j, o in enumerate(ops):
            for name, (off, n) in GLA_OPERAND_ROWS.items():
                ops_ref[j, off:off + n, :] = o[name]
            ft_ref[j] = jnp.broadcast_to(o["ftot"], (SUBLANES, dk))

    def fetch(buf):
        ops_ref, ft_ref = buf
        out = []
        for j in range(len(jobs)):
            o = {name: ops_ref[j, off:off + n, :] for name, (off, n) in GLA_OPERAND_ROWS.items()}
            o["ftot"] = ft_ref[j, 0:1, :]
            out.append(o)
        return out

    def ctx_step(i, sts):
        return tuple(_gla_first_dots(o, st)[0] for o, st in zip(prep_ctx(i), sts))

    def lat_step(i, sts, second_half, cur, nxt):
        ops = fetch(cur)
        firsts = [_gla_first_dots(o, st) for o, st in zip(ops, sts)]
        stash(prep_lat(jnp.minimum(i + 1, n_lat - 1)), nxt)
        outs = [_gla_second_dots(o, f[1], tuple(m[int(rev)] for m in (mfar_ref, mcross_ref, mnear_ref)))
                for o, f, (_, rev) in zip(ops, firsts, jobs)]
        for (h, rev), o in zip(jobs, outs):
            r = start(rev, i, n_lat)
            if second_half:
                o = o + oacc_ref[pl.ds(r, T), cols(h)]
                gz = g_ref[0, pl.ds(r, T), cols(h)].astype(F32)
                o = _rms(o) * ng_ref[...] * (gz * jax.nn.sigmoid(gz))
                o_ref[0, pl.ds(r, T), cols(h)] = o.astype(o_ref.dtype)
            else:
                oacc_ref[pl.ds(r, T), cols(h)] = o
        return tuple(f[0] for f in firsts)

    buf_a, buf_b = (ops_a_ref, ft_a_ref), (ops_b_ref, ft_b_ref)

    def lat_pair(k, sts, second_half):
        sts = lat_step(2 * k, sts, second_half, buf_a, buf_b)
        return lat_step(2 * k + 1, sts, second_half, buf_b, buf_a)

    sts = lax.fori_loop(0, n_ctx, ctx_step, (zero_state,) * len(jobs))
    stash(prep_lat(0), buf_a)
    quarter = n_lat // 4
    sts = lax.fori_loop(0, quarter, functools.partial(lat_pair, second_half=False), sts)
    lax.fori_loop(quarter, 2 * quarter, functools.partial(lat_pair, second_half=True), sts)


GLA_HEADS_PER_STEP = 2


def _gla(p_lat, p_ctx, lb, norm_g, n_heads):
    B, L, _ = p_lat.shape
    Lc = p_ctx.shape[1]
    hp = GLA_HEADS_PER_STEP
    steps = n_heads // hp
    dv = HGRN_DK
    w = hp * dv

    def col(group):
        return lambda b, h: (b, 0, group * steps + h)

    lat_specs = [pl.BlockSpec((1, L, w), col(gidx)) for gidx in range(5)]
    ctx_specs = [pl.BlockSpec((1, Lc, w), col(gidx)) for gidx in (1, 2, 3)]
    masks = _gla_mask_tables()
    n_lat = L // HGRN_CHUNK
    assert n_lat % 4 == 0 and n_heads % hp == 0
    operand_bufs = [pltpu.VMEM((2 * hp, GLA_OPERAND_TOTAL, dv), BF16), pltpu.VMEM((2 * hp, SUBLANES, dv), F32)]
    return pl.pallas_call(
        functools.partial(_gla_kernel, n_lat=n_lat, n_ctx=Lc // HGRN_CHUNK, hp=hp),
        grid=(B, steps),
        in_specs=lat_specs + ctx_specs + [
            pl.BlockSpec((2, w), lambda b, h: (0, h)),
            pl.BlockSpec((1, dv), lambda b, h: (0, 0)),
        ] + [pl.BlockSpec(m.shape, lambda b, h: (0, 0, 0)) for m in masks],
        out_specs=pl.BlockSpec((1, L, w), lambda b, h: (b, 0, h)),
        out_shape=jax.ShapeDtypeStruct((B, L, n_heads * dv), BF16),
        scratch_shapes=[pltpu.VMEM((L, w), F32)] + operand_bufs * 2,
        compiler_params=_cparams(("parallel", "parallel")),
        name="hgrn2_bidirectional",
    )(*([p_lat] * 5), *([p_ctx] * 3), lb, norm_g.reshape(1, dv), *masks)


def _rope_tables(n_tokens):
    rows = n_tokens // GRID_W
    row = jnp.repeat(jnp.arange(rows, dtype=F32), GRID_W)
    colp = jnp.tile(jnp.arange(GRID_W, dtype=F32), rows)
    inv = ROPE_THETA ** (-jnp.arange(AXIS_FREQS, dtype=F32) / AXIS_FREQS)
    ang = jnp.stack([row[:, None] * inv, colp[:, None] * inv], axis=1)
    cos, sin = jnp.cos(ang), jnp.sin(ang)
    zero = jnp.zeros_like(sin)
    full = lambda a, b: jnp.stack([a, b], axis=2).reshape(n_tokens, HEAD_DIM)
    return full(cos, cos), full(-sin, zero), full(zero, sin)


def _identity_rope(n_tokens):
    return (jnp.ones((n_tokens, HEAD_DIM), F32), jnp.zeros((n_tokens, HEAD_DIM), F32),
            jnp.zeros((n_tokens, HEAD_DIM), F32))


def kernel(x, c, ctx, c_ctx, mod_w, mod_b, norm_g, ab_w_in, ab_w_out, attn_q_norm, attn_k_norm,
           s5_lam_re, s5_lam_im, s5_log_dt, s5_b_re, s5_b_im, s5_c_re, s5_c_im, s5_d,
           s5_glu_w, s5_glu_b, c_w_in, c_w_out, hgrn_lb_logits, hgrn_norm,
           ffn_w_up, ffn_conv_w, ffn_conv_b, ffn_w_down):
    B, L, D = x.shape
    Lc = ctx.shape[1]
    depth = mod_w.shape[0]
    qk_w = QK_HEADS * HEAD_DIM
    t = TILES

    lb_all = jnp.cumsum(jax.nn.softmax(hgrn_lb_logits.astype(F32), axis=0), axis=0)
    lb_all = lb_all - lb_all[:1]

    cvec = jnp.concatenate([c, c_ctx[None], jnp.zeros((SUBLANES - (B + 1) % SUBLANES, D), F32)], axis=0)
    mods = _modulation(cvec, mod_w, mod_b)

    rope_lat = _rope_tables(L)
    rope_ctx = _identity_rope(Lc)
    w_up = ffn_w_up.astype(BF16)
    w_down = ffn_w_down.astype(BF16)

    for l in range(depth):
        last = l == depth - 1
        m_lat = [mods[l, :B, k * D:(k + 1) * D][:, None, :] for k in range(6)]
        m_ctx = [mods[l, B:B + 1, k * D:(k + 1) * D][:, None, :] for k in range(6)]
        g = norm_g[l]
        if l % 2 == 0:
            e = l // 2
            w_in = ab_w_in[e].astype(BF16)
            gain = jnp.concatenate([jnp.tile(attn_q_norm[e] * (HEAD_DIM ** -0.5 * math.log2(math.e)), ATTN_HEADS),
                                    jnp.tile(attn_k_norm[e], ATTN_KV_HEADS)]).reshape(1, qk_w)
            qk_lat, v_lat, u_lat = _proj_ab(x, g[0], m_lat[0], m_lat[1], w_in, gain, rope_lat, tm=t.proj_tm)
            qk_ctx, v_ctx, u_ctx = _proj_ab(ctx, g[0], m_ctx[0], m_ctx[1], w_in, gain, rope_ctx, tm=t.proj_tm)
            a_lat = _attention(qk_lat, v_lat, (qk_ctx, v_ctx), tq=t.attn_tq)
            tables = _s5_tables(s5_lam_re[e], s5_lam_im[e], s5_log_dt[e], s5_b_re[e], s5_b_im[e],
                                s5_c_re[e], s5_c_im[e], s5_d[e])
            y = _s5(u_ctx, u_lat, tables)
            glu_w = s5_glu_w[e].astype(BF16)
            w_out = ab_w_out[e].astype(BF16)
            s_lat = _glu(y, Lc // S5_CHUNK, L, glu_w, s5_glu_b[e], tm=t.glu_tm)
            x = _out_res([a_lat, s_lat], w_out, x, m_lat[2], g[1], tm=t.out_tm)
            if not last:
                a_ctx = _attention(qk_ctx, v_ctx, None, tq=t.attn_tq_ctx)
                s_ctx = _glu(y, 0, Lc, glu_w, s5_glu_b[e], tm=t.glu_tm)
                ctx = _out_res([a_ctx, s_ctx], w_out, ctx, m_ctx[2], g[1], tm=t.out_tm)
        else:
            o_idx = l // 2
            w_in = c_w_in[o_idx].astype(BF16)
            n_heads = c_w_out.shape[1] // HGRN_DK
            p_lat = _proj(x, g[0], m_lat[0], m_lat[1], w_in, tm=t.proj_tm, tn=t.proj_tn)
            p_ctx = _proj(ctx, g[0], m_ctx[0], m_ctx[1], w_in, tm=t.proj_tm, tn=t.proj_tn)
            o_lat = _gla(p_lat, p_ctx, lb_all[l], hgrn_norm[o_idx], n_heads)
            x = _out_res([o_lat], c_w_out[o_idx].astype(BF16), x, m_lat[2], g[1], tm=t.out_tm)
            assert last, "context outputs of the HGRN2 mixer are only needed by a following layer"
        x = _conv_ffn(x, g[2], m_lat[3], m_lat[4], w_up, ffn_conv_w[l], ffn_conv_b[l], w_down,
                      m_lat[5], g[3], layer=l, tm=t.ffn_tm, tf=t.ffn_tf)
        if not last:
            ctx = _conv_ffn(ctx, g[2], m_ctx[3], m_ctx[4], w_up, ffn_conv_w[l], ffn_conv_b[l], w_down,
                            m_ctx[5], g[3], layer=l, tm=t.ffn_tm, tf=t.ffn_tf)
    return x
```

```python
import functools
import math
from typing import NamedTuple

import jax
import jax.numpy as jnp
import numpy as np
from jax import lax
from jax.experimental import pallas as pl
from jax.experimental.pallas import tpu as pltpu

F32 = jnp.float32
BF16 = jnp.bfloat16
EPS = 1e-6

LANES = 128
SUBLANES = 8
BF16_ROWS = 16

HEAD_DIM = 128
ATTN_HEADS = 8
ATTN_KV_HEADS = 2
ATTN_GROUP = ATTN_HEADS // ATTN_KV_HEADS
GRID_W = 64
ROPE_THETA = 10000.0
AXIS_FREQS = HEAD_DIM // 4

S5_GROUP = 16
S5_STATE = 64
S5_CHUNK = 16

HGRN_DK = 128
HGRN_CHUNK = 64

VMEM_LIMIT = 56 * 1024 * 1024


class _Tiles(NamedTuple):
    proj_tm: int = 1024
    proj_tn: int = 2048
    attn_tq: int = 512
    attn_tq_ctx: int = 256
    glu_tm: int = 256
    out_tm: int = 512
    ffn_tm: int = 512
    ffn_tf: int = 512


TILES = _Tiles()


def _cparams(sem):
    return pltpu.CompilerParams(dimension_semantics=sem, vmem_limit_bytes=VMEM_LIMIT)


def _vec_map(vec, nd_grid):
    shared = vec.shape[0] == 1
    if nd_grid == 2:
        return lambda g, i: (0 if shared else g, 0, 0)
    return lambda g, i, j: (0 if shared else g, 0, 0)


def _rms(x):
    return x * lax.rsqrt(jnp.mean(x * x, axis=-1, keepdims=True) + EPS)


def _mod_kernel(s_ref, w_ref, b_ref, o_ref):
    s = s_ref[...]
    a = s * jax.nn.sigmoid(s)
    o_ref[0] = jnp.dot(a.astype(BF16), w_ref[0].astype(BF16), preferred_element_type=F32) + b_ref[0]


def _modulation(cvec, mod_w, mod_b):
    depth, d, n = mod_w.shape
    rows = cvec.shape[0]
    tn = 1024
    return pl.pallas_call(
        _mod_kernel,
        grid=(depth, n // tn),
        in_specs=[
            pl.BlockSpec((rows, d), lambda l, j: (0, 0)),
            pl.BlockSpec((1, d, tn), lambda l, j: (l, 0, j)),
            pl.BlockSpec((1, 1, tn), lambda l, j: (l, 0, j)),
        ],
        out_specs=pl.BlockSpec((1, rows, tn), lambda l, j: (l, 0, j)),
        out_shape=jax.ShapeDtypeStruct((depth, rows, n), F32),
        compiler_params=_cparams(("parallel", "parallel")),
        name="modulation",
    )(cvec, mod_w, mod_b.reshape(depth, 1, n))


NORM_ROWS = 32


def _row_blocks(n_rows):
    step = NORM_ROWS if n_rows % NORM_ROWS == 0 else n_rows
    return [slice(r, r + step) for r in range(0, n_rows, step)]


def _norm_mod_to(h_ref, x_ref, g_ref, sh_ref, sc_ref):
    gain = g_ref[...] * (1.0 + sc_ref[0])
    for rows in _row_blocks(h_ref.shape[0]):
        h_ref[rows] = (_rms(x_ref[0, rows]) * gain + sh_ref[0]).astype(BF16)


def _proj_kernel(x_ref, g_ref, sh_ref, sc_ref, w_ref, o_ref, h_ref):
    @pl.when(pl.program_id(2) == 0)
    def _():
        _norm_mod_to(h_ref, x_ref, g_ref, sh_ref, sc_ref)

    o_ref[0] = jnp.dot(h_ref[...], w_ref[...], preferred_element_type=F32).astype(o_ref.dtype)


def _proj_specs(x, g, shift, scale, w, tm, tn):
    G, T, D = x.shape
    in_specs = [
        pl.BlockSpec((1, tm, D), lambda b, i, j: (b, i, 0)),
        pl.BlockSpec((1, D), lambda b, i, j: (0, 0)),
        pl.BlockSpec((1, 1, D), _vec_map(shift, 3)),
        pl.BlockSpec((1, 1, D), _vec_map(scale, 3)),
        pl.BlockSpec((D, tn), lambda b, i, j: (0, j)),
    ]
    return in_specs, [x, g.reshape(1, D), shift, scale, w]


def _proj(x, g, shift, scale, w, *, tm, tn, out_dtype=BF16):
    G, T, D = x.shape
    N = w.shape[1]
    tm = min(tm, T)
    in_specs, args = _proj_specs(x, g, shift, scale, w, tm, tn)
    return pl.pallas_call(
        _proj_kernel,
        grid=(G, T // tm, N // tn),
        in_specs=in_specs,
        out_specs=pl.BlockSpec((1, tm, tn), lambda b, i, j: (b, i, j)),
        out_shape=jax.ShapeDtypeStruct((G, T, N), out_dtype),
        scratch_shapes=[pltpu.VMEM((tm, D), BF16)],
        compiler_params=_cparams(("parallel", "parallel", "arbitrary")),
        name="norm_mod_proj",
    )(*args)


QK_HEADS = ATTN_HEADS + ATTN_KV_HEADS


def _proj_ab_kernel(x_ref, g_ref, sh_ref, sc_ref, w_ref, gain_ref, cos_ref, s1_ref, s2_ref,
                    qk_ref, v_ref, u_ref, h_ref):
    j = pl.program_id(2)

    @pl.when(j == 0)
    def _():
        _norm_mod_to(h_ref, x_ref, g_ref, sh_ref, sc_ref)

    acc = jnp.dot(h_ref[...], w_ref[...], preferred_element_type=F32)

    def head(hh):
        return slice(hh * HEAD_DIM, (hh + 1) * HEAD_DIM)

    @pl.when(j == 0)
    def _():
        for hh in range(QK_HEADS):
            y = _rms(acc[:, head(hh)]) * gain_ref[:, head(hh)]
            y = (y * cos_ref[...] + pltpu.roll(y, HEAD_DIM - AXIS_FREQS, 1) * s1_ref[...]
                 + pltpu.roll(y, AXIS_FREQS, 1) * s2_ref[...])
            qk_ref[0, :, head(hh)] = y.astype(qk_ref.dtype)

    @pl.when(j == 1)
    def _():
        v_ref[0] = acc[:, :ATTN_KV_HEADS * HEAD_DIM].astype(v_ref.dtype)
        slabs, chunks = u_ref.shape[:2]
        for s in range(slabs):
            u_ref[s, :, 0] = acc[:, head(ATTN_KV_HEADS + s)].astype(u_ref.dtype).reshape(chunks, S5_CHUNK, LANES)


def _proj_ab(x, g, shift, scale, w, gain, rope, *, tm):
    G, T, D = x.shape
    tm = min(tm, T)
    tn = QK_HEADS * HEAD_DIM
    assert w.shape[1] == 2 * tn
    v_w = ATTN_KV_HEADS * HEAD_DIM
    slabs = (tn - v_w) // LANES
    in_specs, args = _proj_specs(x, g, shift, scale, w, tm, tn)
    in_specs += [pl.BlockSpec((1, tn), lambda b, i, j: (0, 0))]
    in_specs += [pl.BlockSpec((tm, HEAD_DIM), lambda b, i, j: (i, 0))] * 3
    return pl.pallas_call(
        _proj_ab_kernel,
        grid=(G, T // tm, 2),
        in_specs=in_specs,
        out_specs=[
            pl.BlockSpec((1, tm, tn), lambda b, i, j: (b, i, 0)),
            pl.BlockSpec((1, tm, v_w), lambda b, i, j: (b, i, 0)),
            pl.BlockSpec((slabs, tm // S5_CHUNK, 1, S5_CHUNK, LANES), lambda b, i, j: (0, i, b, 0, 0)),
        ],
        out_shape=[
            jax.ShapeDtypeStruct((G, T, tn), BF16),
            jax.ShapeDtypeStruct((G, T, v_w), BF16),
            jax.ShapeDtypeStruct((slabs, T // S5_CHUNK, G, S5_CHUNK, LANES), BF16),
        ],
        scratch_shapes=[pltpu.VMEM((tm, D), BF16)],
        compiler_params=_cparams(("parallel", "parallel", "arbitrary")),
        name="norm_mod_proj_ab",
    )(*args, gain, *rope)


ATTN_KEY_CHUNK = 1024


def _attn_kernel(q_ref, k_ref, v_ref, *rest, tq, has_prefix):
    if has_prefix:
        kp_ref, vp_ref, o_ref = rest
    else:
        (o_ref,) = rest
    nt = (((1,), (1,)), ((), ()))
    q = jnp.concatenate(
        [q_ref[0, :, g * HEAD_DIM:(g + 1) * HEAD_DIM] for g in range(ATTN_GROUP)], axis=0)
    n_keys = k_ref.shape[1]
    kc = min(ATTN_KEY_CHUNK, n_keys)
    chunks = [(kp_ref, vp_ref, 0, kp_ref.shape[1])] if has_prefix else []
    chunks += [(k_ref, v_ref, c * kc, kc) for c in range(n_keys // kc)]

    def scores(chunk):
        kr, _, start, size = chunk
        return lax.dot_general(q, kr[0, start:start + size, :], nt, preferred_element_type=F32)

    m = jnp.full((q.shape[0], 1), -1e30, F32)
    l = jnp.zeros((q.shape[0], 1), F32)
    o = jnp.zeros((q.shape[0], HEAD_DIM), F32)
    s_next = scores(chunks[0])
    for ci, (_, vr, start, size) in enumerate(chunks):
        s = s_next
        if ci + 1 < len(chunks):
            s_next = scores(chunks[ci + 1])
        m_new = jnp.maximum(m, jnp.max(s, axis=-1, keepdims=True))
        alpha = jnp.exp2(m - m_new)
        p = jnp.exp2(s - m_new)
        l = alpha * l + jnp.sum(p, axis=-1, keepdims=True)
        o = alpha * o + jnp.dot(p.astype(BF16), vr[0, start:start + size, :], preferred_element_type=F32)
        m = m_new
    o = o * (1.0 / l)
    for g in range(ATTN_GROUP):
        o_ref[0, :, g * HEAD_DIM:(g + 1) * HEAD_DIM] = o[g * tq:(g + 1) * tq].astype(o_ref.dtype)


def _attention(qk, v, prefix, *, tq):
    B, L, _ = qk.shape
    tq = min(tq, L)
    gw = ATTN_GROUP * HEAD_DIM
    k_blk = ATTN_HEADS
    in_specs = [
        pl.BlockSpec((1, tq, gw), lambda b, h, i: (b, i, h)),
        pl.BlockSpec((1, L, HEAD_DIM), lambda b, h, i: (b, 0, k_blk + h)),
        pl.BlockSpec((1, L, HEAD_DIM), lambda b, h, i: (b, 0, h)),
    ]
    args = [qk, qk, v]
    if prefix is not None:
        Lp = prefix[0].shape[1]
        in_specs += [
            pl.BlockSpec((1, Lp, HEAD_DIM), lambda b, h, i: (b, 0, k_blk + h)),
            pl.BlockSpec((1, Lp, HEAD_DIM), lambda b, h, i: (b, 0, h)),
        ]
        args += list(prefix)
    return pl.pallas_call(
        functools.partial(_attn_kernel, tq=tq, has_prefix=prefix is not None),
        grid=(B, ATTN_KV_HEADS, L // tq),
        in_specs=in_specs,
        out_specs=pl.BlockSpec((1, tq, gw), lambda b, h, i: (b, i, h)),
        out_shape=jax.ShapeDtypeStruct((B, L, ATTN_HEADS * HEAD_DIM), BF16),
        compiler_params=_cparams(("parallel", "parallel", "arbitrary")),
        name="gqa_attention",
    )(*args)


S5_SLAB_GROUPS = LANES // S5_GROUP
S5_KD = S5_CHUNK * LANES
S5_PAIR_COLS = 4 * LANES


def _s5_end_kernel(u_ref, w_ref, e_ref):
    e_ref[...] = jnp.dot(u_ref[0], w_ref[0], preferred_element_type=F32)


def _s5_scan_kernel(e_ref, cf_ref, cb_ref, h_ref, *, batch, nc_ctx, nc_all):
    cps = SUBLANES // batch
    g_ctx, g_all = nc_ctx // cps, nc_all // cps

    def advance(hr, hi, er, ei, c_ref):
        ar, ai = c_ref[0:1, :], c_ref[1:2, :]
        return ar * hr - ai * hi + er, ar * hi + ai * hr + ei

    def sweep(r, state, col0, c_ref, order):
        e8 = e_ref[pl.ds(r, SUBLANES), col0:col0 + 2 * LANES]
        hr, hi = state
        out_r, out_i = [None] * cps, [None] * cps
        for c in order:
            out_r[c], out_i[c] = hr, hi
            rows = slice(c * batch, (c + 1) * batch)
            hr, hi = advance(hr, hi, e8[rows, :LANES], e8[rows, LANES:], c_ref)
        h_ref[pl.ds(r, SUBLANES), col0:col0 + LANES] = jnp.concatenate(out_r, axis=0).astype(h_ref.dtype)
        h_ref[pl.ds(r, SUBLANES), col0 + LANES:col0 + 2 * LANES] = (
            jnp.concatenate(out_i, axis=0).astype(h_ref.dtype))
        return hr, hi

    def body(i, carry):
        fwd, bwd = carry
        fwd = sweep(pl.multiple_of(i * SUBLANES, SUBLANES), fwd, 0, cf_ref, range(cps))
        gi = jnp.where(i < g_ctx, g_ctx - 1 - i, g_all - 1 - (i - g_ctx))
        bwd = sweep(pl.multiple_of(gi * SUBLANES, SUBLANES), bwd, 2 * LANES, cb_ref, reversed(range(cps)))
        return fwd, bwd

    zero = jnp.zeros((batch, LANES), F32)
    lax.fori_loop(0, g_all, body, ((zero, zero), (zero, zero)))


def _s5_out_kernel(u_ref, us_ref, lag_ref, h_ref, win_ref, d_ref, y_ref, tz_ref):
    T = S5_CHUNK
    t_blocks = tz_ref.shape[1] // LANES
    t0 = pl.program_id(1) * t_blocks
    for tt in range(t_blocks):
        for s in range(T):
            tz_ref[s * LANES:(s + 1) * LANES, tt * LANES:(tt + 1) * LANES] = lag_ref[0, t0 + tt - s + (T - 1)]
    y = jnp.dot(u_ref[0], tz_ref[...], preferred_element_type=F32)
    y += jnp.dot(h_ref[...].astype(BF16), win_ref[0], preferred_element_type=F32)
    y += us_ref[0].astype(F32) * d_ref[0]
    y_ref[0] = y.astype(y_ref.dtype)


def _s5_tables(lam_re, lam_im, log_dt, b_re, b_im, c_re, c_im, d_skip):
    T, C, P = S5_CHUNK, S5_GROUP, S5_STATE
    G = lam_re.shape[1]
    hi = lax.Precision.HIGHEST
    lam = lax.complex(jnp.minimum(lam_re.astype(F32), -1e-4), lam_im.astype(F32))
    dt = jnp.exp(log_dt.astype(F32))[..., None]
    lam_dt = lam * dt
    lam_bar = jnp.exp(lam_dt)
    bmat = lax.complex(b_re.astype(F32), b_im.astype(F32))
    b_bar = ((lam_bar - 1.0) / lam)[..., None] * bmat
    cmat = lax.complex(c_re.astype(F32), c_im.astype(F32))
    steps = jnp.arange(T + 1, dtype=F32)
    pw = jnp.exp(lam_dt[:, None] * steps[None, :, None, None])

    kern = jnp.real(jnp.einsum('zgcp,zjgp,zgpd->zjgcd', cmat, pw[:, :T], b_bar, precision=hi))

    def end_w(p_sel, bb):
        w = p_sel[:, :, None, :] * bb.transpose(0, 2, 1)[None]
        return [jnp.real(w), jnp.imag(w)]

    w_end = jnp.concatenate(end_w(pw[0, :T][::-1], b_bar[0]) + end_w(pw[1, :T], b_bar[1]), axis=-1)

    def in_w(p_sel, cm):
        z = cm.transpose(0, 2, 1)[:, :, None, :] * p_sel.transpose(1, 2, 0)[:, :, :, None]
        return [jnp.real(z), -jnp.imag(z)]

    w_in = jnp.stack(in_w(pw[0, 1:], cmat[0]) + in_w(pw[1, 1:][::-1], cmat[1]), axis=1)

    def coef(z):
        return jnp.stack([jnp.real(z).reshape(G * P), jnp.imag(z).reshape(G * P)])

    sg = S5_SLAB_GROUPS
    ngb, npair = G // sg, sg // 2
    grp = np.arange(sg)
    tok_cols_grp = np.tile(np.repeat(grp, C), T)
    state_cols_grp = (2 * np.arange(npair)[:, None, None, None] + np.arange(2)[None, None, :, None]
                      + np.zeros((1, 4, 1, P), np.int64)).reshape(-1)
    rows_gd = np.repeat(grp, C)

    lag_k = jnp.concatenate([kern[1][1:][::-1], (kern[0][0] + kern[1][0])[None], kern[0][1:]], axis=0)
    lag_src = lag_k.reshape(2 * T - 1, ngb, sg, C, C).transpose(1, 0, 2, 4, 3).reshape(ngb * (2 * T - 1), LANES, C)
    rep_c = np.tile(np.eye(C, dtype=np.float32), (1, sg))
    mask_c = (rows_gd[:, None] == np.repeat(grp, C)[None, :]).astype(np.float32)[None]
    lag_tiles = _spread(lag_src, rep_c, mask_c, lambda n: 0).reshape(ngb, 2 * T - 1, LANES, LANES)

    end_src = w_end.reshape(T, ngb, sg * C, 4 * P).transpose(1, 0, 2, 3).reshape(ngb * T, LANES, 4 * P)
    rep_state = np.tile(np.eye(4 * P, dtype=np.float32).reshape(4 * P, 1, 4, 1, P), (1, npair, 1, 2, 1))
    rep_state = rep_state.reshape(4 * P, npair * S5_PAIR_COLS)
    mask_state = (rows_gd[:, None] == state_cols_grp[None, :]).astype(np.float32)[None]
    end_slab = _spread(end_src, rep_state, mask_state, lambda n: 0).reshape(ngb, S5_KD, npair * S5_PAIR_COLS)

    in_src = w_in.reshape(ngb, npair, 2, 4, P, T * C).transpose(0, 1, 3, 2, 4, 5)
    in_src = in_src.reshape(ngb * npair * 4, 2 * P, T * C)
    rep_tok = np.tile(np.eye(T * C, dtype=np.float32).reshape(T * C, T, 1, C), (1, 1, sg, 1)).reshape(T * C, S5_KD)
    row_grp = 2 * np.arange(npair)[:, None, None] + np.repeat(np.arange(2), P)[None, :, None]
    mask_tok = (row_grp == tok_cols_grp[None, None, :]).astype(np.float32)
    in_slab = _spread(in_src, rep_tok, mask_tok, lambda n: (n // 4) % npair)
    in_slab = in_slab.reshape(ngb, npair * S5_PAIR_COLS, S5_KD)

    d_slab = jnp.tile(d_skip.astype(F32).reshape(ngb, 1, LANES), (1, 1, T))
    return lag_tiles, end_slab, in_slab, coef(pw[0, T]), coef(pw[1, T]), d_slab


SPREAD_TILES = 4


def _spread_kernel(src_ref, rep_ref, mask_ref, o_ref):
    for i in range(SPREAD_TILES):
        o_ref[i] = (jnp.dot(src_ref[i].astype(BF16), rep_ref[...], preferred_element_type=F32)
                    * mask_ref[0]).astype(o_ref.dtype)


def _spread(src, rep, mask, mask_index):
    n, rows, k = src.shape
    width = rep.shape[1]
    st = SPREAD_TILES
    assert n % st == 0
    return pl.pallas_call(
        _spread_kernel,
        grid=(n // st,),
        in_specs=[
            pl.BlockSpec((st, rows, k), lambda i: (i, 0, 0)),
            pl.BlockSpec((k, width), lambda i: (0, 0)),
            pl.BlockSpec((1, rows, width), lambda i: (mask_index(i * st), 0, 0)),
        ],
        out_specs=pl.BlockSpec((st, rows, width), lambda i: (i, 0, 0)),
        out_shape=jax.ShapeDtypeStruct((n, rows, width), BF16),
        compiler_params=_cparams(("parallel",)),
        name="s5_spread_operator",
    )(src, jnp.asarray(rep, BF16), jnp.asarray(mask, F32))


def _s5(u_ctx, u_lat, tables):
    tz, w_end, w_in, cf, cb, d_slab = tables
    ngb, nc_ctx, B = u_ctx.shape[:3]
    nc_all = nc_ctx + u_lat.shape[1]
    cps = SUBLANES // B
    assert SUBLANES % B == 0 and nc_ctx % cps == 0 and (nc_all - nc_ctx) % cps == 0
    M = nc_all * B
    kd, sw = S5_KD, w_end.shape[-1]
    n_half = 4
    hw = kd // n_half
    u = jnp.concatenate([u_ctx, u_lat], axis=1).reshape(ngb, M, kd)

    e = pl.pallas_call(
        _s5_end_kernel,
        grid=(ngb, n_half),
        in_specs=[
            pl.BlockSpec((1, M, kd), lambda g, j: (g, 0, 0)),
            pl.BlockSpec((1, kd, sw // n_half), lambda g, j: (g, 0, j)),
        ],
        out_specs=pl.BlockSpec((M, sw // n_half), lambda g, j: (0, g * n_half + j)),
        out_shape=jax.ShapeDtypeStruct((M, ngb * sw), F32),
        compiler_params=_cparams(("parallel", "parallel")),
        name="s5_chunk_end_states",
    )(u, w_end)

    h = pl.pallas_call(
        functools.partial(_s5_scan_kernel, batch=B, nc_ctx=nc_ctx, nc_all=nc_all),
        grid=(ngb * sw // S5_PAIR_COLS,),
        in_specs=[pl.BlockSpec((M, S5_PAIR_COLS), lambda j: (0, j))]
        + [pl.BlockSpec((2, LANES), lambda j: (0, j))] * 2,
        out_specs=pl.BlockSpec((M, S5_PAIR_COLS), lambda j: (0, j)),
        out_shape=jax.ShapeDtypeStruct((M, ngb * sw), F32),
        compiler_params=_cparams(("parallel",)),
        name="s5_chunk_scan",
    )(e, cf, cb)

    y = pl.pallas_call(
        _s5_out_kernel,
        grid=(ngb, n_half),
        in_specs=[
            pl.BlockSpec((1, M, kd), lambda g, j: (g, 0, 0)),
            pl.BlockSpec((1, M, hw), lambda g, j: (g, 0, j)),
            pl.BlockSpec((1,) + tz.shape[1:], lambda g, j: (g, 0, 0, 0)),
            pl.BlockSpec((M, sw), lambda g, j: (0, g)),
            pl.BlockSpec((1, sw, hw), lambda g, j: (g, 0, j)),
            pl.BlockSpec((1, 1, hw), lambda g, j: (g, 0, j)),
        ],
        out_specs=pl.BlockSpec((1, M, hw), lambda g, j: (g, 0, j)),
        out_shape=jax.ShapeDtypeStruct((ngb, M, kd), BF16),
        scratch_shapes=[pltpu.VMEM((kd, hw), BF16)],
        compiler_params=_cparams(("parallel", "parallel")),
        name="s5_chunk_outputs",
    )(u, u, tz, h, w_in, d_slab)

    return y.reshape(ngb, nc_all, B, S5_CHUNK, LANES)


def _glu_kernel(y_ref, w_ref, b_ref, o_ref):
    slabs, chunks = y_ref.shape[:2]
    y = jnp.concatenate([y_ref[s, :, 0].reshape(chunks * S5_CHUNK, LANES) for s in range(slabs)], axis=1)
    z = jnp.dot(jax.nn.gelu(y.astype(F32)).astype(BF16), w_ref[...], preferred_element_type=F32) + b_ref[...]
    n = o_ref.shape[-1]
    o_ref[0] = (z[:, :n] * jax.nn.sigmoid(z[:, n:])).astype(o_ref.dtype)


def _glu(y, first_chunk, n_tokens, w, b, *, tm):
    slabs, _, B = y.shape[:3]
    W = slabs * LANES
    tm = min(tm, n_tokens)
    tc = tm // S5_CHUNK
    assert first_chunk % tc == 0
    off = first_chunk // tc
    return pl.pallas_call(
        _glu_kernel,
        grid=(B, n_tokens // tm),
        in_specs=[
            pl.BlockSpec((slabs, tc, 1, S5_CHUNK, LANES), lambda g, i: (0, off + i, g, 0, 0)),
            pl.BlockSpec((W, 2 * W), lambda g, i: (0, 0)),
            pl.BlockSpec((1, 2 * W), lambda g, i: (0, 0)),
        ],
        out_specs=pl.BlockSpec((1, tm, W), lambda g, i: (g, i, 0)),
        out_shape=jax.ShapeDtypeStruct((B, n_tokens, W), BF16),
        compiler_params=_cparams(("parallel", "parallel")),
        name="s5_glu",
    )(y, w, b.reshape(1, 2 * W))


def _out_res_kernel(*refs, widths):
    n = len(widths)
    lhs = refs[:n]
    w_ref, x_ref, gate_ref, g_ref, o_ref = refs[n:]
    y = None
    off = 0
    for r, k in zip(lhs, widths):
        part = jnp.dot(r[0], w_ref[off:off + k, :], preferred_element_type=F32)
        y = part if y is None else y + part
        off += k
    gain = gate_ref[0] * g_ref[...]
    for rows in _row_blocks(y.shape[0]):
        o_ref[0, rows] = x_ref[0, rows] + _rms(y[rows]) * gain


def _out_res(parts, w, x, gate, g, *, tm):
    G, T, D = x.shape
    tm = min(tm, T)
    widths = tuple(p.shape[-1] for p in parts)
    K = sum(widths)
    in_specs = [pl.BlockSpec((1, tm, k), lambda b, i: (b, i, 0)) for k in widths]
    in_specs += [
        pl.BlockSpec((K, D), lambda b, i: (0, 0)),
        pl.BlockSpec((1, tm, D), lambda b, i: (b, i, 0)),
        pl.BlockSpec((1, 1, D), _vec_map(gate, 2)),
        pl.BlockSpec((1, D), lambda b, i: (0, 0)),
    ]
    return pl.pallas_call(
        functools.partial(_out_res_kernel, widths=widths),
        grid=(G, T // tm),
        in_specs=in_specs,
        out_specs=pl.BlockSpec((1, tm, D), lambda b, i: (b, i, 0)),
        out_shape=jax.ShapeDtypeStruct((G, T, D), F32),
        compiler_params=_cparams(("parallel", "parallel")),
        name="out_proj_residual",
    )(*parts, w, x, gate, g.reshape(1, D))


HALO = BF16_ROWS


def _ffn_kernel(x_ref, xp_ref, xn_ref, g2_ref, sh_ref, sc_ref, wa_ref, wv_ref, cwa_ref, cwv_ref,
                cba_ref, cbv_ref, wd_ref, gate_ref, g3_ref, o_ref, h_ref, acc_ref, ua_ref, uv_ref, *, tm):
    i = pl.program_id(1)
    f = pl.program_id(2)

    def normmod(x):
        return _rms(x) * (g2_ref[...] * (1.0 + sc_ref[0])) + sh_ref[0]

    @pl.when(f == 0)
    def _():
        d = x_ref.shape[-1]
        hp = jnp.where(i == 0, 0.0, normmod(xp_ref[0]))[SUBLANES - 1:SUBLANES]
        hn = jnp.where(i == pl.num_programs(1) - 1, 0.0, normmod(xn_ref[0]))[0:1]
        row = lax.broadcasted_iota(jnp.int32, (HALO, d), 0)
        halo = jnp.where(row == 0, hn, jnp.where(row == HALO - 1, hp, 0.0))
        for rows in _row_blocks(tm):
            h_ref[rows] = normmod(x_ref[0, rows]).astype(BF16)
        h_ref[tm:] = halo.astype(BF16)
        acc_ref[...] = jnp.zeros_like(acc_ref)

    h = h_ref[...]
    ua_ref[...] = jnp.dot(h, wa_ref[...], preferred_element_type=F32)
    uv_ref[...] = jnp.dot(h, wv_ref[...], preferred_element_type=F32)

    def conv(u_ref, cw_ref, cb_ref):
        prev = pltpu.roll(u_ref[...], 1, 0)[0:tm]
        return (prev * cw_ref[0:1, :] + u_ref[0:tm] * cw_ref[1:2, :]
                + u_ref[1:tm + 1] * cw_ref[2:3, :] + cb_ref[...])

    a = conv(ua_ref, cwa_ref, cba_ref)
    v = conv(uv_ref, cwv_ref, cbv_ref)
    gated = (a * jax.nn.sigmoid(a) * v).astype(BF16)
    acc_ref[...] += jnp.dot(gated, wd_ref[...], preferred_element_type=F32)

    @pl.when(f == pl.num_programs(2) - 1)
    def _():
        gain = gate_ref[0] * g3_ref[...]
        for rows in _row_blocks(tm):
            o_ref[0, rows] = x_ref[0, rows] + _rms(acc_ref[rows]) * gain


def _conv_ffn(x, g2, shift, scale, w_up, conv_w, conv_b, w_down, gate, g3, *, layer, tm, tf):
    G, T, D = x.shape
    F = w_down.shape[1]
    tm = min(tm, T)
    nf = F // tf
    n_i = T // tm
    rb = tm // SUBLANES
    last_rb = T // SUBLANES - 1
    conv_b = conv_b.reshape(1, 2 * F)
    return pl.pallas_call(
        functools.partial(_ffn_kernel, tm=tm),
        grid=(G, n_i, nf),
        in_specs=[
            pl.BlockSpec((1, tm, D), lambda b, i, f: (b, i, 0)),
            pl.BlockSpec((1, SUBLANES, D), lambda b, i, f: (b, jnp.maximum(i * rb - 1, 0), 0)),
            pl.BlockSpec((1, SUBLANES, D), lambda b, i, f: (b, jnp.minimum((i + 1) * rb, last_rb), 0)),
            pl.BlockSpec((1, D), lambda b, i, f: (0, 0)),
            pl.BlockSpec((1, 1, D), _vec_map(shift, 3)),
            pl.BlockSpec((1, 1, D), _vec_map(scale, 3)),
            pl.BlockSpec((None, D, tf), lambda b, i, f: (layer, 0, f)),
            pl.BlockSpec((None, D, tf), lambda b, i, f: (layer, 0, nf + f)),
            pl.BlockSpec((3, tf), lambda b, i, f: (0, f)),
            pl.BlockSpec((3, tf), lambda b, i, f: (0, nf + f)),
            pl.BlockSpec((1, tf), lambda b, i, f: (0, f)),
            pl.BlockSpec((1, tf), lambda b, i, f: (0, nf + f)),
            pl.BlockSpec((None, tf, D), lambda b, i, f: (layer, f, 0)),
            pl.BlockSpec((1, 1, D), _vec_map(gate, 3)),
            pl.BlockSpec((1, D), lambda b, i, f: (0, 0)),
        ],
        out_specs=pl.BlockSpec((1, tm, D), lambda b, i, f: (b, i, 0)),
        out_shape=jax.ShapeDtypeStruct((G, T, D), F32),
        scratch_shapes=[
            pltpu.VMEM((tm + HALO, D), BF16),
            pltpu.VMEM((tm, D), F32),
            pltpu.VMEM((tm + HALO, tf), F32),
            pltpu.VMEM((tm + HALO, tf), F32),
        ],
        compiler_params=_cparams(("parallel", "parallel", "arbitrary")),
        name="conv_ffn",
    )(x, x, x, g2.reshape(1, D), shift, scale, w_up, w_up, conv_w, conv_w, conv_b, conv_b, w_down,
      gate, g3.reshape(1, D))


GLA_BLOCKS = HGRN_CHUNK // SUBLANES


def _gla_operands(z, v, q, lb, rev):
    nb, rb = GLA_BLOCKS, SUBLANES
    order = list(range(nb))[::-1] if rev else list(range(nb))
    scan_of = {b: j for j, b in enumerate(order)}
    pos = lax.broadcasted_iota(jnp.int32, (rb, HGRN_DK), 0)
    if rev:
        pos = rb - 1 - pos

    def prev(x, k):
        return pltpu.roll(x, (rb - k) if rev else k, 0)

    def nxt(x, k):
        return pltpu.roll(x, k if rev else (rb - k), 0)

    def blocks(x):
        return [x[rb * b:rb * (b + 1)] for b in range(nb)]

    def rows(bl):
        return jnp.concatenate(bl, axis=0)

    f = lb + (1.0 - lb) * jax.nn.sigmoid(z)
    fb = blocks(f)
    kb = blocks(1.0 - f)

    hb = rb // 2
    hpos = pos & (hb - 1)
    lo_last = hb if rev else hb - 1
    hi_first = hb - 1 if rev else hb
    p4, s4, p8, s8, bt = [], [], [], [], []
    for b in range(nb):
        x = y = fb[b]
        for k in (1, 2):
            x = x * jnp.where(hpos >= k, prev(x, k), 1.0)
            y = y * jnp.where(hpos <= hb - 1 - k, nxt(y, k), 1.0)
        lo_tot = x[lo_last:lo_last + 1, :]
        hi_tot = y[hi_first:hi_first + 1, :]
        sfx = jnp.where(hpos <= hb - 2, nxt(y, 1), 1.0)
        p4.append(x)
        s4.append(sfx)
        p8.append(x * jnp.where(pos >= hb, lo_tot, 1.0))
        s8.append(sfx * jnp.where(pos < hb, hi_tot, 1.0))
        bt.append(lo_tot * hi_tot)
    bts = [bt[order[j]] for j in range(nb)]
    ones = jnp.ones_like(bts[0])
    before = [ones]
    for j in range(1, nb):
        before.append(before[j - 1] * bts[j - 1])
    after = [ones] * nb
    for j in range(nb - 2, -1, -1):
        after[j] = after[j + 1] * bts[j + 1]
    ftot = before[nb - 1] * bts[nb - 1]

    kbase = [kb[b] * s8[b] for b in range(nb)]
    ops = {"v": v, "ftot": ftot,
           "kdec": rows([kbase[b] * after[scan_of[b]] for b in range(nb)]).astype(BF16)}
    if q is None:
        return ops

    qb = blocks(q)
    q8 = [qb[b] * p8[b] for b in range(nb)]
    ops["q_state"] = rows([q8[b] * before[scan_of[b]] for b in range(nb)]).astype(BF16)
    ops["q_far"] = rows(q8).astype(BF16)
    ops["q_near"] = q.astype(BF16)

    vb = blocks(v.astype(F32))
    kcols, vcols = [], []
    chain = {}
    for d in range(1, nb):
        for j in range(nb - d):
            chain[j] = kbase[order[j]] if d == 1 else chain[j] * bts[j + d - 1]
            kcols.append(chain[j])
            vcols.append(vb[order[j]])
    ops["k_far"] = rows(kcols).astype(BF16)
    ops["v_far"] = rows(vcols).astype(BF16)

    ops["q_cross"] = rows([qb[b] * p4[b] for b in range(nb)]).astype(BF16)
    ops["k_cross"] = rows([kb[b] * s4[b] for b in range(nb)]).astype(BF16)

    kvar = []
    cur = kb
    for d in range(hb):
        if d:
            cur = [cur[b] * nxt(fb[b], d) for b in range(nb)]
        kvar.append(rows(cur).astype(BF16))
    ops["k_near"] = jnp.concatenate(kvar, axis=0)
    return ops


def _gla_first_dots(ops, st):
    nt = (((1,), (1,)), ((), ()))
    scores = None
    if "q_near" in ops:
        near = lax.dot_general(ops["q_near"], ops["k_near"], nt, preferred_element_type=F32)
        far = lax.dot_general(ops["q_far"], ops["k_far"], nt, preferred_element_type=F32)
        cross = lax.dot_general(ops["q_cross"], ops["k_cross"], nt, preferred_element_type=F32)
        o = lax.dot_general(ops["q_state"], st.astype(BF16), nt, preferred_element_type=F32)
        scores = (o, far, cross, near)
    st_new = st * ops["ftot"] + lax.dot_general(ops["v"], ops["kdec"], (((0,), (0,)), ((), ())),
                                                preferred_element_type=F32)
    return st_new, scores


def _gla_second_dots(ops, scores, masks):
    o, far, cross, near = scores
    m_far, m_cross, m_near = masks
    o = o + jnp.dot(far.astype(BF16) * m_far, ops["v_far"], preferred_element_type=F32)
    o = o + jnp.dot(cross.astype(BF16) * m_cross, ops["v"], preferred_element_type=F32)
    v_near = jnp.concatenate([ops["v"]] * (near.shape[1] // HGRN_CHUNK), axis=0)
    return o + jnp.dot(near.astype(BF16) * m_near, v_near, preferred_element_type=F32)


def _operand_rows():
    T, nb, rb = HGRN_CHUNK, GLA_BLOCKS, SUBLANES
    far = rb * nb * (nb - 1) // 2
    sizes = [("kdec", T), ("q_state", T), ("q_far", T), ("q_near", T), ("q_cross", T), ("k_cross", T),
             ("v", T), ("k_far", far), ("v_far", far), ("k_near", (rb // 2) * T)]
    rows, off = {}, 0
    for name, n in sizes:
        rows[name] = (off, n)
        off += n
    return rows, off


GLA_OPERAND_ROWS, GLA_OPERAND_TOTAL = _operand_rows()


def _gla_mask_tables():
    T, nb, rb = HGRN_CHUNK, GLA_BLOCKS, SUBLANES
    hb = rb // 2
    t = np.arange(T)
    far, cross, near = [], [], []
    for rev in (False, True):
        sblk = (nb - 1 - t // rb) if rev else t // rb
        spos = (rb - 1 - t % rb) if rev else t % rb
        same_block = t[:, None] // rb == t[None, :] // rb
        cols = [(d, j) for d in range(1, nb) for j in range(nb - d)]
        m_far = np.zeros((T, len(cols) * rb), np.float32)
        for c, (d, j) in enumerate(cols):
            m_far[sblk == j + d, c * rb:(c + 1) * rb] = 1.0
        m_cross = same_block & (spos[:, None] >= hb) & (spos[None, :] < hb)
        same_half = same_block & (spos[:, None] // hb == spos[None, :] // hb)
        m_near = np.zeros((T, hb * T), np.float32)
        for d in range(hb):
            m_near[:, d * T:(d + 1) * T] = same_half & (spos[:, None] - spos[None, :] == d)
        far.append(m_far)
        cross.append(m_cross.astype(np.float32))
        near.append(m_near)
    return tuple(jnp.asarray(np.stack(m), BF16) for m in (far, cross, near))


def _gla_kernel(q_ref, zf_ref, zb_ref, v_ref, g_ref, czf_ref, czb_ref, cv_ref, lb_ref, ng_ref,
                mfar_ref, mcross_ref, mnear_ref, o_ref, oacc_ref, ops_a_ref, ft_a_ref, ops_b_ref, ft_b_ref,
                *, n_lat, n_ctx, hp):
    T, dk = HGRN_CHUNK, HGRN_DK
    jobs = [(h, rev) for h in range(hp) for rev in (False, True)]
    zero_state = jnp.zeros((dk, dk), F32)

    def cols(h):
        return slice(h * dk, (h + 1) * dk)

    def start(rev, i, n):
        r = ((n - 1 - i) if rev else i) * T
        return r if isinstance(r, int) else pl.multiple_of(r, T)

    def lb_row(h, rev):
        return lb_ref[int(rev):int(rev) + 1, cols(h)]

    def prep_ctx(i):
        ops = []
        for h, rev in jobs:
            r = start(rev, i, n_ctx)
            z = (czb_ref if rev else czf_ref)[0, pl.ds(r, T), cols(h)].astype(F32)
            ops.append(_gla_operands(z, cv_ref[0, pl.ds(r, T), cols(h)], None, lb_row(h, rev), rev))
        return tuple(ops)

    def prep_lat(i):
        ops = []
        for h, rev in jobs:
            r = start(rev, i, n_lat)
            qz = q_ref[0, pl.ds(r, T), cols(h)].astype(F32)
            z = (zb_ref if rev else zf_ref)[0, pl.ds(r, T), cols(h)].astype(F32)
            ops.append(_gla_operands(z, v_ref[0, pl.ds(r, T), cols(h)], qz * jax.nn.sigmoid(qz),
                                     lb_row(h, rev), rev))
        return tuple(ops)

    def stash(ops, buf):
        ops_ref, ft_ref = buf
        for j, o in enumerate(ops):
            for name, (off, n) in GLA_OPERAND_ROWS.items():
                ops_ref[j, off:off + n, :] = o[name]
            ft_ref[j] = jnp.broadcast_to(o["ftot"], (SUBLANES, dk))

    def fetch(buf):
        ops_ref, ft_ref = buf
        out = []
        for j in range(len(jobs)):
            o = {name: ops_ref[j, off:off + n, :] for name, (off, n) in GLA_OPERAND_ROWS.items()}
            o["ftot"] = ft_ref[j, 0:1, :]
            out.append(o)
        return out

    def ctx_step(i, sts):
        return tuple(_gla_first_dots(o, st)[0] for o, st in zip(prep_ctx(i), sts))

    def lat_step(i, sts, second_half, cur, nxt):
        ops = fetch(cur)
        firsts = [_gla_first_dots(o, st) for o, st in zip(ops, sts)]
        stash(prep_lat(jnp.minimum(i + 1, n_lat - 1)), nxt)
        outs = [_gla_second_dots(o, f[1], tuple(m[int(rev)] for m in (mfar_ref, mcross_ref, mnear_ref)))
                for o, f, (_, rev) in zip(ops, firsts, jobs)]
        for (h, rev), o in zip(jobs, outs):
            r = start(rev, i, n_lat)
            if second_half:
                o = o + oacc_ref[pl.ds(r, T), cols(h)]
                gz = g_ref[0, pl.ds(r, T), cols(h)].astype(F32)
                o = _rms(o) * ng_ref[...] * (gz * jax.nn.sigmoid(gz))
                o_ref[0, pl.ds(r, T), cols(h)] = o.astype(o_ref.dtype)
            else:
                oacc_ref[pl.ds(r, T), cols(h)] = o
        return tuple(f[0] for f in firsts)

    buf_a, buf_b = (ops_a_ref, ft_a_ref), (ops_b_ref, ft_b_ref)

    def lat_pair(k, sts, second_half):
        sts = lat_step(2 * k, sts, second_half, buf_a, buf_b)
        return lat_step(2 * k + 1, sts, second_half, buf_b, buf_a)

    sts = lax.fori_loop(0, n_ctx, ctx_step, (zero_state,) * len(jobs))
    stash(prep_lat(0), buf_a)
    quarter = n_lat // 4
    sts = lax.fori_loop(0, quarter, functools.partial(lat_pair, second_half=False), sts)
    lax.fori_loop(quarter, 2 * quarter, functools.partial(lat_pair, second_half=True), sts)


GLA_HEADS_PER_STEP = 2


def _gla(p_lat, p_ctx, lb, norm_g, n_heads):
    B, L, _ = p_lat.shape
    Lc = p_ctx.shape[1]
    hp = GLA_HEADS_PER_STEP
    steps = n_heads // hp
    dv = HGRN_DK
    w = hp * dv

    def col(group):
        return lambda b, h: (b, 0, group * steps + h)

    lat_specs = [pl.BlockSpec((1, L, w), col(gidx)) for gidx in range(5)]
    ctx_specs = [pl.BlockSpec((1, Lc, w), col(gidx)) for gidx in (1, 2, 3)]
    masks = _gla_mask_tables()
    n_lat = L // HGRN_CHUNK
    assert n_lat % 4 == 0 and n_heads % hp == 0
    operand_bufs = [pltpu.VMEM((2 * hp, GLA_OPERAND_TOTAL, dv), BF16), pltpu.VMEM((2 * hp, SUBLANES, dv), F32)]
    return pl.pallas_call(
        functools.partial(_gla_kernel, n_lat=n_lat, n_ctx=Lc // HGRN_CHUNK, hp=hp),
        grid=(B, steps),
        in_specs=lat_specs + ctx_specs + [
            pl.BlockSpec((2, w), lambda b, h: (0, h)),
            pl.BlockSpec((1, dv), lambda b, h: (0, 0)),
        ] + [pl.BlockSpec(m.shape, lambda b, h: (0, 0, 0)) for m in masks],
        out_specs=pl.BlockSpec((1, L, w), lambda b, h: (b, 0, h)),
        out_shape=jax.ShapeDtypeStruct((B, L, n_heads * dv), BF16),
        scratch_shapes=[pltpu.VMEM((L, w), F32)] + operand_bufs * 2,
        compiler_params=_cparams(("parallel", "parallel")),
        name="hgrn2_bidirectional",
    )(*([p_lat] * 5), *([p_ctx] * 3), lb, norm_g.reshape(1, dv), *masks)


def _rope_tables(n_tokens):
    rows = n_tokens // GRID_W
    row = jnp.repeat(jnp.arange(rows, dtype=F32), GRID_W)
    colp = jnp.tile(jnp.arange(GRID_W, dtype=F32), rows)
    inv = ROPE_THETA ** (-jnp.arange(AXIS_FREQS, dtype=F32) / AXIS_FREQS)
    ang = jnp.stack([row[:, None] * inv, colp[:, None] * inv], axis=1)
    cos, sin = jnp.cos(ang), jnp.sin(ang)
    zero = jnp.zeros_like(sin)
    full = lambda a, b: jnp.stack([a, b], axis=2).reshape(n_tokens, HEAD_DIM)
    return full(cos, cos), full(-sin, zero), full(zero, sin)


def _identity_rope(n_tokens):
    return (jnp.ones((n_tokens, HEAD_DIM), F32), jnp.zeros((n_tokens, HEAD_DIM), F32),
            jnp.zeros((n_tokens, HEAD_DIM), F32))


def kernel(x, c, ctx, c_ctx, mod_w, mod_b, norm_g, ab_w_in, ab_w_out, attn_q_norm, attn_k_norm,
           s5_lam_re, s5_lam_im, s5_log_dt, s5_b_re, s5_b_im, s5_c_re, s5_c_im, s5_d,
           s5_glu_w, s5_glu_b, c_w_in, c_w_out, hgrn_lb_logits, hgrn_norm,
           ffn_w_up, ffn_conv_w, ffn_conv_b, ffn_w_down):
    B, L, D = x.shape
    Lc = ctx.shape[1]
    depth = mod_w.shape[0]
    qk_w = QK_HEADS * HEAD_DIM
    t = TILES

    lb_all = jnp.cumsum(jax.nn.softmax(hgrn_lb_logits.astype(F32), axis=0), axis=0)
    lb_all = lb_all - lb_all[:1]

    cvec = jnp.concatenate([c, c_ctx[None], jnp.zeros((SUBLANES - (B + 1) % SUBLANES, D), F32)], axis=0)
    mods = _modulation(cvec, mod_w, mod_b)

    rope_lat = _rope_tables(L)
    rope_ctx = _identity_rope(Lc)
    w_up = ffn_w_up.astype(BF16)
    w_down = ffn_w_down.astype(BF16)

    for l in range(depth):
        last = l == depth - 1
        m_lat = [mods[l, :B, k * D:(k + 1) * D][:, None, :] for k in range(6)]
        m_ctx = [mods[l, B:B + 1, k * D:(k + 1) * D][:, None, :] for k in range(6)]
        g = norm_g[l]
        if l % 2 == 0:
            e = l // 2
            w_in = ab_w_in[e].astype(BF16)
            gain = jnp.concatenate([jnp.tile(attn_q_norm[e] * (HEAD_DIM ** -0.5 * math.log2(math.e)), ATTN_HEADS),
                                    jnp.tile(attn_k_norm[e], ATTN_KV_HEADS)]).reshape(1, qk_w)
            qk_lat, v_lat, u_lat = _proj_ab(x, g[0], m_lat[0], m_lat[1], w_in, gain, rope_lat, tm=t.proj_tm)
            qk_ctx, v_ctx, u_ctx = _proj_ab(ctx, g[0], m_ctx[0], m_ctx[1], w_in, gain, rope_ctx, tm=t.proj_tm)
            a_lat = _attention(qk_lat, v_lat, (qk_ctx, v_ctx), tq=t.attn_tq)
            tables = _s5_tables(s5_lam_re[e], s5_lam_im[e], s5_log_dt[e], s5_b_re[e], s5_b_im[e],
                                s5_c_re[e], s5_c_im[e], s5_d[e])
            y = _s5(u_ctx, u_lat, tables)
            glu_w = s5_glu_w[e].astype(BF16)
            w_out = ab_w_out[e].astype(BF16)
            s_lat = _glu(y, Lc // S5_CHUNK, L, glu_w, s5_glu_b[e], tm=t.glu_tm)
            x = _out_res([a_lat, s_lat], w_out, x, m_lat[2], g[1], tm=t.out_tm)
            if not last:
                a_ctx = _attention(qk_ctx, v_ctx, None, tq=t.attn_tq_ctx)
                s_ctx = _glu(y, 0, Lc, glu_w, s5_glu_b[e], tm=t.glu_tm)
                ctx = _out_res([a_ctx, s_ctx], w_out, ctx, m_ctx[2], g[1], tm=t.out_tm)
        else:
            o_idx = l // 2
            w_in = c_w_in[o_idx].astype(BF16)
            n_heads = c_w_out.shape[1] // HGRN_DK
            p_lat = _proj(x, g[0], m_lat[0], m_lat[1], w_in, tm=t.proj_tm, tn=t.proj_tn)
            p_ctx = _proj(ctx, g[0], m_ctx[0], m_ctx[1], w_in, tm=t.proj_tm, tn=t.proj_tn)
            o_lat = _gla(p_lat, p_ctx, lb_all[l], hgrn_norm[o_idx], n_heads)
            x = _out_res([o_lat], c_w_out[o_idx].astype(BF16), x, m_lat[2], g[1], tm=t.out_tm)
            assert last, "context outputs of the HGRN2 mixer are only needed by a following layer"
        x = _conv_ffn(x, g[2], m_lat[3], m_lat[4], w_up, ffn_conv_w[l], ffn_conv_b[l], w_down,
                      m_lat[5], g[3], layer=l, tm=t.ffn_tm, tf=t.ffn_tf)
        if not last:
            ctx = _conv_ffn(ctx, g[2], m_ctx[3], m_ctx[4], w_up, ffn_conv_w[l], ffn_conv_b[l], w_down,
                            m_ctx[5], g[3], layer=l, tm=t.ffn_tm, tf=t.ffn_tf)
    return x
```
